```python
import math
import jax, jax.numpy as jnp
from jax import lax
import numpy as np

D_MODEL = 2048
BATCH = 4
SEQ = 2048
DEPTH = 1

MEM_LEN = 256
CONV_CH = D_MODEL // 2
CONV_WIDTH = 31
DA_HEADS = 4
DA_HEAD_DIM = 128
DA_QK = DA_HEADS * 2 * DA_HEAD_DIM
DA_WIDTH = DA_HEADS * 2 * DA_HEAD_DIM
XA_HEADS = 4
XA_HEAD_DIM = D_MODEL // 8
XA_WIDTH = XA_HEADS * XA_HEAD_DIM
N_BRANCH = 3
D_FF = 4 * D_MODEL
ROPE_THETA = 10000.0
Q_BLOCK = 128
NORM_EPS = 1e-6
IN_SIZES = (2 * CONV_CH, DA_QK, DA_QK, DA_WIDTH, XA_WIDTH, N_BRANCH * D_MODEL)
D_IN = sum(IN_SIZES)

kernel_name = "hybrid_conformer_diffattn_memxattn_gated"


def lambda_init_fn(layer_idx):
    return 0.8 - 0.6 * math.exp(-0.3 * layer_idx)


def rmsnorm(x, g):
    xf = x.astype(jnp.float32)
    y = xf * lax.rsqrt(jnp.mean(xf * xf, axis=-1, keepdims=True) + NORM_EPS)
    return (y * g.astype(jnp.float32)).astype(x.dtype)


def layernorm(x, g, b):
    xf = x.astype(jnp.float32)
    mu = jnp.mean(xf, axis=-1, keepdims=True)
    var = jnp.mean(jnp.square(xf - mu), axis=-1, keepdims=True)
    y = (xf - mu) * lax.rsqrt(var + NORM_EPS)
    return (y * g.astype(jnp.float32) + b.astype(jnp.float32)).astype(x.dtype)


def rope_tables(positions):
    inv_freq = 1.0 / (ROPE_THETA ** (jnp.arange(0, DA_HEAD_DIM, 2, dtype=jnp.float32) / DA_HEAD_DIM))
    ang = positions.astype(jnp.float32)[..., None] * inv_freq
    return jnp.cos(ang), jnp.sin(ang)


def apply_rope(t, cos, sin):
    c = cos[:, :, None, None, :].astype(t.dtype)
    s = sin[:, :, None, None, :].astype(t.dtype)
    t1, t2 = jnp.split(t, 2, axis=-1)
    return jnp.concatenate([t1 * c - t2 * s, t2 * c + t1 * s], axis=-1)


def conformer_branch(u_glu, w_dw, b_dw, g_ln, b_ln, w_out):
    a, b = jnp.split(u_glu, 2, axis=-1)
    u = a * jax.nn.sigmoid(b)
    u = lax.conv_general_dilated(
        u, w_dw[:, None, :].astype(u.dtype), window_strides=(1,),
        padding=[(CONV_WIDTH - 1, 0)],
        dimension_numbers=("NWC", "WIO", "NWC"),
        feature_group_count=CONV_CH) + b_dw
    u = jax.nn.silu(layernorm(u, g_ln, b_ln))
    return u @ w_out


def diff_attention(q, k, v, lam, lambda_init, g_subln, w_out):
    bsz, seq = q.shape[0], q.shape[1]
    scale = DA_HEAD_DIM ** -0.5
    outs = []
    for blk in range(seq // Q_BLOCK):
        q0 = blk * Q_BLOCK
        kv_len = q0 + Q_BLOCK
        qb = q[:, q0:kv_len]
        kb = k[:, :kv_len]
        vb = v[:, :kv_len]
        s = jnp.einsum("bqhmd,bkhmd->bhmqk", qb, kb).astype(jnp.float32) * scale
        mask = (q0 + jnp.arange(Q_BLOCK))[:, None] >= jnp.arange(kv_len)[None, :]
        s = jnp.where(mask, s, -jnp.inf)
        p = jax.nn.softmax(s, axis=-1)
        a = p[:, :, 0] - lam * p[:, :, 1]
        outs.append(jnp.einsum("bhqk,bkhe->bqhe", a.astype(vb.dtype), vb))
    o = jnp.concatenate(outs, axis=1)
    o = rmsnorm(o, g_subln) * (1.0 - lambda_init)
    return o.reshape(bsz, seq, DA_WIDTH) @ w_out


def memory_cross_attention(xq, mem_n, w_kv, w_out):
    bsz, seq = xq.shape[0], xq.shape[1]
    q = xq.reshape(bsz, seq, XA_HEADS, XA_HEAD_DIM)
    kv = mem_n @ w_kv
    k, v = jnp.split(kv, 2, axis=-1)
    k = k.reshape(bsz, -1, XA_HEADS, XA_HEAD_DIM)
    v = v.reshape(bsz, -1, XA_HEADS, XA_HEAD_DIM)
    s = jnp.einsum("bshd,bmhd->bhsm", q, k).astype(jnp.float32) * (XA_HEAD_DIM ** -0.5)
    p = jax.nn.softmax(s, axis=-1)
    o = jnp.einsum("bhsm,bmhd->bshd", p.astype(v.dtype), v)
    return o.reshape(bsz, seq, XA_WIDTH) @ w_out


def setup_inputs(seed: int = 0) -> dict:
    key = jax.random.key(seed)
    ks = jax.random.split(key, 32)
    f32 = jnp.float32

    def nrm(k, shape, fan_in):
        return jax.random.normal(k, shape, f32) * (fan_in ** -0.5)

    def gain(k, shape):
        return 1.0 + 0.01 * jax.random.normal(k, shape, f32)

    L = DEPTH
    return {
        "x": jax.random.normal(ks[0], (BATCH, SEQ, D_MODEL), f32),
        "mem": jax.random.normal(ks[1], (BATCH, MEM_LEN, D_MODEL), f32),
        "positions": jnp.arange(SEQ, dtype=jnp.int32)[None, :]
                     + jax.random.randint(ks[2], (BATCH, 1), 0, 1024, dtype=jnp.int32),
        "g_mix": gain(ks[3], (L, D_MODEL)),
        "w_in": nrm(ks[4], (L, D_MODEL, D_IN), D_MODEL),
        "w_dw": nrm(ks[5], (L, CONV_WIDTH, CONV_CH), CONV_WIDTH),
        "b_dw": 0.01 * jax.random.normal(ks[6], (L, CONV_CH), f32),
        "g_conv_ln": gain(ks[7], (L, CONV_CH)),
        "b_conv_ln": 0.01 * jax.random.normal(ks[8], (L, CONV_CH), f32),
        "w_conv_out": nrm(ks[9], (L, CONV_CH, D_MODEL), CONV_CH),
        "lambda_q1": 0.1 * jax.random.normal(ks[10], (L, DA_HEAD_DIM), f32),
        "lambda_k1": 0.1 * jax.random.normal(ks[11], (L, DA_HEAD_DIM), f32),
        "lambda_q2": 0.1 * jax.random.normal(ks[12], (L, DA_HEAD_DIM), f32),
        "lambda_k2": 0.1 * jax.random.normal(ks[13], (L, DA_HEAD_DIM), f32),
        "g_subln": gain(ks[14], (L, 2 * DA_HEAD_DIM)),
        "w_da_out": nrm(ks[15], (L, DA_WIDTH, D_MODEL), DA_WIDTH),
        "g_mem": gain(ks[16], (L, D_MODEL)),
        "w_mem_kv": nrm(ks[17], (L, D_MODEL, 2 * XA_WIDTH), D_MODEL),
        "w_xa_out": nrm(ks[18], (L, XA_WIDTH, D_MODEL), XA_WIDTH),
        "w_mix_out": nrm(ks[19], (L, D_MODEL, D_MODEL), D_MODEL),
        "g_mlp": gain(ks[20], (L, D_MODEL)),
        "w_up": nrm(ks[21], (L, D_MODEL, D_FF), D_MODEL),
        "w_down": nrm(ks[22], (L, D_FF, D_MODEL), D_FF),
        "g_final": gain(ks[23], (D_MODEL,)),
    }


def reference(x, mem, positions, g_mix, w_in, w_dw, b_dw, g_conv_ln, b_conv_ln, w_conv_out,
              lambda_q1, lambda_k1, lambda_q2, lambda_k2, g_subln, w_da_out,
              g_mem, w_mem_kv, w_xa_out, w_mix_out, g_mlp, w_up, w_down, g_final):
    bsz, seq = x.shape[0], x.shape[1]
    cos, sin = rope_tables(positions)
    split_idx = [int(v) for v in np.cumsum(IN_SIZES)[:-1]]
    for l in range(DEPTH):
        lambda_init = lambda_init_fn(l)
        h = rmsnorm(x, g_mix[l])
        proj = h @ w_in[l]
        u_glu, q_da, k_da, v_da, q_xa, gate_pre = jnp.split(proj, split_idx, axis=-1)

        y_conv = conformer_branch(u_glu, w_dw[l], b_dw[l], g_conv_ln[l], b_conv_ln[l], w_conv_out[l])

        q = apply_rope(q_da.reshape(bsz, seq, DA_HEADS, 2, DA_HEAD_DIM), cos, sin)
        k = apply_rope(k_da.reshape(bsz, seq, DA_HEADS, 2, DA_HEAD_DIM), cos, sin)
        v = v_da.reshape(bsz, seq, DA_HEADS, 2 * DA_HEAD_DIM)
        lam = (jnp.exp(jnp.sum(lambda_q1[l].astype(jnp.float32) * lambda_k1[l].astype(jnp.float32)))
               - jnp.exp(jnp.sum(lambda_q2[l].astype(jnp.float32) * lambda_k2[l].astype(jnp.float32)))
               + lambda_init)
        y_da = diff_attention(q, k, v, lam, lambda_init, g_subln[l], w_da_out[l])

        mem_n = rmsnorm(mem, g_mem[l])
        y_xa = memory_cross_attention(q_xa, mem_n, w_mem_kv[l], w_xa_out[l])

        gates = jax.nn.sigmoid(gate_pre).reshape(bsz, seq, N_BRANCH, D_MODEL)
        merged = gates[:, :, 0] * y_conv + gates[:, :, 1] * y_da + gates[:, :, 2] * y_xa
        x = x + merged @ w_mix_out[l]

        h2 = rmsnorm(x, g_mlp[l])
        x = x + jnp.square(jax.nn.relu(h2 @ w_up[l])) @ w_down[l]
    return rmsnorm(x, g_final)
```

```python
import functools
import math

import jax
import jax.numpy as jnp
from jax import lax
from jax.experimental import pallas as pl
from jax.experimental.pallas import tpu as pltpu

F32 = jnp.float32
BF16 = jnp.bfloat16

D_MODEL = 2048
CONV_CH = 1024
CONV_WIDTH = 31
DA_HEADS = 4
DA_HEAD_DIM = 128
DA_WIDTH = 1024
XA_HEADS = 4
XA_HEAD_DIM = 256
XA_WIDTH = 1024
D_FF = 4 * D_MODEL
ROPE_THETA = 10000.0
NORM_EPS = 1e-6
LAMBDA_INIT = 0.8 - 0.6 * math.exp(-0.3 * 0)

GLU_COLS = 2 * CONV_CH
REST_COL0 = GLU_COLS
REST_COLS = 3 * 1024 + XA_WIDTH + 3 * D_MODEL
Q_OFF, K_OFF, V_OFF, XQ_OFF, GATE_OFF = 0, 1024, 2048, 3072, 4096

V7X_VMEM_LIMIT = 56 * 1024 * 1024
CONV_HALO = 32


def _cparams(n_axes):
    return pltpu.CompilerParams(
        dimension_semantics=("arbitrary",) * n_axes,
        vmem_limit_bytes=V7X_VMEM_LIMIT,
    )


def _sigmoid(x):
    return 1.0 / (1.0 + jnp.exp(-x))


def _rmsnorm_kernel(x_ref, g_ref, o_ref):
    x = x_ref[...]
    ms = jnp.mean(x * x, axis=-1, keepdims=True)
    o_ref[...] = (x * lax.rsqrt(ms + NORM_EPS) * g_ref[...]).astype(o_ref.dtype)


def _rmsnorm(x2d, g, out_dtype, tm=512):
    m, d = x2d.shape
    return pl.pallas_call(
        _rmsnorm_kernel,
        grid=(m // tm,),
        in_specs=[pl.BlockSpec((tm, d), lambda i: (i, 0)),
                  pl.BlockSpec((1, d), lambda i: (0, 0))],
        out_specs=pl.BlockSpec((tm, d), lambda i: (i, 0)),
        out_shape=jax.ShapeDtypeStruct((m, d), out_dtype),
        compiler_params=_cparams(1),
        name="rmsnorm",
    )(x2d, g.reshape(1, d))


def _rope_kernel(pos_ref, f_ref, cos_ref, sin_ref):
    ang = pos_ref[...] * f_ref[...]
    lane = lax.broadcasted_iota(jnp.int32, ang.shape, 1)
    sign = jnp.where(lane < DA_HEAD_DIM // 2, -1.0, 1.0)
    cos_ref[...] = jnp.cos(ang)
    sin_ref[...] = jnp.sin(ang) * sign


def _rope_tables(pos_col, tm=1024):
    t = pos_col.shape[0]
    inv_freq = 1.0 / (ROPE_THETA ** (jnp.arange(0, DA_HEAD_DIM, 2, dtype=F32) / DA_HEAD_DIM))
    f2 = jnp.concatenate([inv_freq, inv_freq]).reshape(1, DA_HEAD_DIM)
    return pl.pallas_call(
        _rope_kernel,
        grid=(t // tm,),
        in_specs=[pl.BlockSpec((tm, 1), lambda i: (i, 0)),
                  pl.BlockSpec((1, DA_HEAD_DIM), lambda i: (0, 0))],
        out_specs=[pl.BlockSpec((tm, DA_HEAD_DIM), lambda i: (i, 0))] * 2,
        out_shape=[jax.ShapeDtypeStruct((t, DA_HEAD_DIM), F32)] * 2,
        compiler_params=_cparams(1),
        name="rope_tables",
    )(pos_col, f2)


def _glu_kernel(h_ref, wa_ref, wb_ref, o_ref, wbf_ref, *, tn):
    @pl.when(pl.program_id(1) == 0)
    def _():
        wbf_ref[:, :tn] = wa_ref[...].astype(BF16)
        wbf_ref[:, tn:] = wb_ref[...].astype(BF16)

    acc = jnp.dot(h_ref[...], wbf_ref[...], preferred_element_type=F32)
    o_ref[...] = (acc[:, :tn] * _sigmoid(acc[:, tn:])).astype(o_ref.dtype)


def _glu_proj(h, w_in, tm=1024, tn=512):
    t, d = h.shape
    nb = CONV_CH // tn
    return pl.pallas_call(
        functools.partial(_glu_kernel, tn=tn),
        grid=(nb, t // tm),
        in_specs=[pl.BlockSpec((tm, d), lambda j, i: (i, 0)),
                  pl.BlockSpec((d, tn), lambda j, i: (0, j)),
                  pl.BlockSpec((d, tn), lambda j, i: (0, nb + j))],
        out_specs=pl.BlockSpec((tm, tn), lambda j, i: (i, j)),
        out_shape=jax.ShapeDtypeStruct((t, CONV_CH), BF16),
        scratch_shapes=[pltpu.VMEM((d, 2 * tn), BF16)],
        compiler_params=_cparams(2),
        name="glu_proj",
    )(h, w_in, w_in)


def _rest_kernel(h_ref, w_ref, cos_ref, sin_ref, o_ref, wbf_ref, *, tn):
    j = pl.program_id(0)

    @pl.when(pl.program_id(1) == 0)
    def _():
        wbf_ref[...] = w_ref[...].astype(BF16)

    acc = jnp.dot(h_ref[...], wbf_ref[...], preferred_element_type=F32)

    @pl.when(j < 2)
    def _():
        c = cos_ref[...]
        s = sin_ref[...]
        scale = jnp.where(j == 0, DA_HEAD_DIM ** -0.5, 1.0).astype(F32)
        for g in range(tn // DA_HEAD_DIM):
            sl = slice(g * DA_HEAD_DIM, (g + 1) * DA_HEAD_DIM)
            t = acc[:, sl]
            r = pltpu.roll(t, DA_HEAD_DIM // 2, axis=1)
            o_ref[:, sl] = ((t * c + r * s) * scale).astype(o_ref.dtype)

    @pl.when(j == 2)
    def _():
        o_ref[...] = acc.astype(o_ref.dtype)

    @pl.when(j == 3)
    def _():
        o_ref[...] = (acc * (XA_HEAD_DIM ** -0.5)).astype(o_ref.dtype)

    @pl.when(j >= 4)
    def _():
        o_ref[...] = _sigmoid(acc).astype(o_ref.dtype)


def _rest_proj(h, w_in, cos2, sin2, tm=1024, tn=1024):
    t, d = h.shape
    col0 = REST_COL0 // tn
    return pl.pallas_call(
        functools.partial(_rest_kernel, tn=tn),
        grid=(REST_COLS // tn, t // tm),
        in_specs=[pl.BlockSpec((tm, d), lambda j, i: (i, 0)),
                  pl.BlockSpec((d, tn), lambda j, i: (0, col0 + j)),
                  pl.BlockSpec((tm, DA_HEAD_DIM), lambda j, i: (i, 0)),
                  pl.BlockSpec((tm, DA_HEAD_DIM), lambda j, i: (i, 0))],
        out_specs=pl.BlockSpec((tm, tn), lambda j, i: (i, j)),
        out_shape=jax.ShapeDtypeStruct((t, REST_COLS), BF16),
        scratch_shapes=[pltpu.VMEM((d, tn), BF16)],
        compiler_params=_cparams(2),
        name="rest_proj",
    )(h, w_in, cos2, sin2)


def _conv_kernel(ucur_ref, uprev_ref, w_ref, bdw_ref, g_ref, b_ref, o_ref, buf_ref, sh_ref, *, ts, rc):
    si = pl.program_id(1)
    prev = uprev_ref[0].astype(F32)
    buf_ref[0:CONV_HALO, :] = jnp.where(si > 0, prev, 0.0)
    buf_ref[CONV_HALO:CONV_HALO + ts, :] = ucur_ref[0].astype(F32)
    buf_ref[CONV_HALO + ts:, :] = jnp.zeros((8, CONV_CH), F32)
    n_sh = ts + CONV_HALO
    for r in range(8):
        sh_ref[r] = buf_ref[pl.ds(r, n_sh), :]

    lead = CONV_HALO - (CONV_WIDTH - 1)

    def chunk(ci, carry):
        r0 = pl.multiple_of(ci * rc, rc)
        acc = jnp.zeros((rc, CONV_CH), F32) + bdw_ref[...]
        for k in range(CONV_WIDTH):
            off = lead + k
            a, r = off // 8, off % 8
            rows = sh_ref[r, pl.ds(pl.multiple_of(r0 + 8 * a, 8), rc), :]
            acc = acc + rows * w_ref[k:k + 1, :]
        mu = jnp.mean(acc, axis=-1, keepdims=True)
        cen = acc - mu
        var = jnp.mean(cen * cen, axis=-1, keepdims=True)
        y = cen * lax.rsqrt(var + NORM_EPS) * g_ref[...] + b_ref[...]
        o_ref[0, pl.ds(r0, rc), :] = (y * _sigmoid(y)).astype(o_ref.dtype)
        return carry

    lax.fori_loop(0, ts // rc, chunk, 0)


def _conv_branch(u3, w_dw, b_dw, g_ln, b_ln, ts=256, rc=16):
    b, s, c = u3.shape
    hb = ts // CONV_HALO
    return pl.pallas_call(
        functools.partial(_conv_kernel, ts=ts, rc=rc),
        grid=(b, s // ts),
        in_specs=[pl.BlockSpec((1, ts, c), lambda bi, si: (bi, si, 0)),
                  pl.BlockSpec((1, CONV_HALO, c), lambda bi, si: (bi, jnp.maximum(si * hb - 1, 0), 0)),
                  pl.BlockSpec((CONV_WIDTH, c), lambda bi, si: (0, 0)),
                  pl.BlockSpec((1, c), lambda bi, si: (0, 0)),
                  pl.BlockSpec((1, c), lambda bi, si: (0, 0)),
                  pl.BlockSpec((1, c), lambda bi, si: (0, 0))],
        out_specs=pl.BlockSpec((1, ts, c), lambda bi, si: (bi, si, 0)),
        out_shape=jax.ShapeDtypeStruct((b, s, c), BF16),
        scratch_shapes=[pltpu.VMEM((CONV_HALO + ts + 8, c), F32),
                        pltpu.VMEM((8, CONV_HALO + ts, c), F32)],
        compiler_params=_cparams(2),
        name="conv_branch",
    )(u3, u3, w_dw, b_dw.reshape(1, c), g_ln.reshape(1, c), b_ln.reshape(1, c))


def _da_kernel(lq1_ref, lk1_ref, lq2_ref, lk2_ref, g_ref, q_ref, k_ref, v_ref, o_ref, *, tq):
    qi = pl.program_id(2)
    hd = DA_HEAD_DIM
    q = q_ref[0]
    q1, q2 = q[:, :hd], q[:, hd:]
    nt = (((1,), (1,)), ((), ()))

    def scores(row0):
        kb = k_ref[0, pl.ds(row0, tq), :]
        s1 = lax.dot_general(q1, kb[:, :hd], nt, preferred_element_type=F32)
        s2 = lax.dot_general(q2, kb[:, hd:], nt, preferred_element_type=F32)
        return jnp.concatenate([s1, s2], axis=0)

    d0 = pl.multiple_of(qi * tq, tq)
    s = scores(d0)
    row = lax.broadcasted_iota(jnp.int32, (2 * tq, tq), 0)
    col = lax.broadcasted_iota(jnp.int32, (2 * tq, tq), 1)
    row = jnp.where(row >= tq, row - tq, row)
    s = jnp.where(row >= col, s, -jnp.inf)
    m0 = jnp.max(s, axis=-1, keepdims=True)
    p = jnp.exp(s - m0)
    l0 = jnp.sum(p, axis=-1, keepdims=True)
    acc0 = jnp.dot(p.astype(BF16), v_ref[0, pl.ds(d0, tq), :], preferred_element_type=F32)

    def body(ki, carry):
        m, l, acc = carry
        r0 = pl.multiple_of(ki * tq, tq)
        s = scores(r0)
        m_new = jnp.maximum(m, jnp.max(s, axis=-1, keepdims=True))
        alpha = jnp.exp(m - m_new)
        p = jnp.exp(s - m_new)
        l = alpha * l + jnp.sum(p, axis=-1, keepdims=True)
        acc = alpha * acc + jnp.dot(p.astype(BF16), v_ref[0, pl.ds(r0, tq), :],
                                    preferred_element_type=F32)
        return m_new, l, acc

    _, l, acc = lax.fori_loop(0, qi, body, (m0, l0, acc0))

    lam = (jnp.exp(jnp.sum(lq1_ref[...] * lk1_ref[...], axis=-1, keepdims=True))
           - jnp.exp(jnp.sum(lq2_ref[...] * lk2_ref[...], axis=-1, keepdims=True))
           + LAMBDA_INIT)
    o = acc / l
    dlt = o[:tq] - lam * o[tq:]
    ms = jnp.mean(dlt * dlt, axis=-1, keepdims=True)
    y = dlt * lax.rsqrt(ms + NORM_EPS) * g_ref[...] * (1.0 - LAMBDA_INIT)
    o_ref[0] = y.astype(o_ref.dtype)


def _diff_attention(p3, lq1, lk1, lq2, lk2, g_subln, tq=256):
    b, s, _ = p3.shape
    hw = 2 * DA_HEAD_DIM
    vec = lambda: pl.BlockSpec((1, DA_HEAD_DIM), lambda bi, h, qi: (0, 0))
    return pl.pallas_call(
        functools.partial(_da_kernel, tq=tq),
        grid=(b, DA_HEADS, s // tq),
        in_specs=[vec(), vec(), vec(), vec(),
                  pl.BlockSpec((1, hw), lambda bi, h, qi: (0, 0)),
                  pl.BlockSpec((1, tq, hw), lambda bi, h, qi: (bi, qi, Q_OFF // hw + h)),
                  pl.BlockSpec((1, s, hw), lambda bi, h, qi: (bi, 0, K_OFF // hw + h)),
                  pl.BlockSpec((1, s, hw), lambda bi, h, qi: (bi, 0, V_OFF // hw + h))],
        out_specs=pl.BlockSpec((1, tq, hw), lambda bi, h, qi: (bi, qi, h)),
        out_shape=jax.ShapeDtypeStruct((b, s, DA_WIDTH), BF16),
        compiler_params=_cparams(3),
        name="diff_attention",
    )(lq1.reshape(1, -1), lk1.reshape(1, -1), lq2.reshape(1, -1), lk2.reshape(1, -1),
      g_subln.reshape(1, hw), p3, p3, p3)


def _proj_kernel(a_ref, w_ref, o_ref, wbf_ref):
    @pl.when(pl.program_id(1) == 0)
    def _():
        wbf_ref[...] = w_ref[...].astype(BF16)

    o_ref[...] = jnp.dot(a_ref[...], wbf_ref[...], preferred_element_type=F32).astype(o_ref.dtype)


def _proj(a, w, tm=1024, tn=1024):
    m, k = a.shape
    n = w.shape[1]
    return pl.pallas_call(
        _proj_kernel,
        grid=(n // tn, m // tm),
        in_specs=[pl.BlockSpec((tm, k), lambda j, i: (i, 0)),
                  pl.BlockSpec((k, tn), lambda j, i: (0, j))],
        out_specs=pl.BlockSpec((tm, tn), lambda j, i: (i, j)),
        out_shape=jax.ShapeDtypeStruct((m, n), BF16),
        scratch_shapes=[pltpu.VMEM((k, tn), BF16)],
        compiler_params=_cparams(2),
        name="proj",
    )(a, w)


def _xa_kernel(q_ref, k_ref, v_ref, o_ref):
    nt = (((1,), (1,)), ((), ()))
    s = lax.dot_general(q_ref[0], k_ref[0], nt, preferred_element_type=F32)
    m = jnp.max(s, axis=-1, keepdims=True)
    p = jnp.exp(s - m)
    l = jnp.sum(p, axis=-1, keepdims=True)
    o = jnp.dot(p.astype(BF16), v_ref[0], preferred_element_type=F32)
    o_ref[0] = (o / l).astype(o_ref.dtype)


def _cross_attention(p3, kv3, ts=512):
    b, s, _ = p3.shape
    mem_len = kv3.shape[1]
    hd = XA_HEAD_DIM
    return pl.pallas_call(
        _xa_kernel,
        grid=(b, s // ts, XA_HEADS),
        in_specs=[pl.BlockSpec((1, ts, hd), lambda bi, si, h: (bi, si, XQ_OFF // hd + h)),
                  pl.BlockSpec((1, mem_len, hd), lambda bi, si, h: (bi, 0, h)),
                  pl.BlockSpec((1, mem_len, hd), lambda bi, si, h: (bi, 0, XA_HEADS + h))],
        out_specs=pl.BlockSpec((1, ts, hd), lambda bi, si, h: (bi, si, h)),
        out_shape=jax.ShapeDtypeStruct((b, s, XA_WIDTH), BF16),
        compiler_params=_cparams(3),
        name="cross_attention",
    )(p3, kv3, kv3)


def _merge_kernel(oc_ref, od_ref, ox_ref, g0_ref, g1_ref, g2_ref, wc_ref, wd_ref, wx_ref, o_ref, wbf_ref):
    @pl.when(pl.program_id(1) == 0)
    def _():
        wbf_ref[0] = wc_ref[...].astype(BF16)
        wbf_ref[1] = wd_ref[...].astype(BF16)
        wbf_ref[2] = wx_ref[...].astype(BF16)

    y = g0_ref[...].astype(F32) * jnp.dot(oc_ref[...], wbf_ref[0], preferred_element_type=F32)
    y = y + g1_ref[...].astype(F32) * jnp.dot(od_ref[...], wbf_ref[1], preferred_element_type=F32)
    y = y + g2_ref[...].astype(F32) * jnp.dot(ox_ref[...], wbf_ref[2], preferred_element_type=F32)
    o_ref[...] = y.astype(o_ref.dtype)


def _merge(o_conv, o_da, o_xa, p2, w_conv_out, w_da_out, w_xa_out, tm=1024, tn=512):
    t, kk = o_conv.shape
    gcol = GATE_OFF // tn
    gstride = D_MODEL // tn
    act = lambda: pl.BlockSpec((tm, kk), lambda j, i: (i, 0))
    wsp = lambda: pl.BlockSpec((kk, tn), lambda j, i: (0, j))
    gate = lambda br: pl.BlockSpec((tm, tn), lambda j, i: (i, gcol + br * gstride + j))
    return pl.pallas_call(
        _merge_kernel,
        grid=(D_MODEL // tn, t // tm),
        in_specs=[act(), act(), act(), gate(0), gate(1), gate(2), wsp(), wsp(), wsp()],
        out_specs=pl.BlockSpec((tm, tn), lambda j, i: (i, j)),
        out_shape=jax.ShapeDtypeStruct((t, D_MODEL), BF16),
        scratch_shapes=[pltpu.VMEM((3, kk, tn), BF16)],
        compiler_params=_cparams(2),
        name="merge",
    )(o_conv, o_da, o_xa, p2, p2, p2, w_conv_out, w_da_out, w_xa_out)


def _mix_kernel(a_ref, w_ref, x_ref, o_ref, wbf_ref):
    @pl.when(pl.program_id(1) == 0)
    def _():
        wbf_ref[...] = w_ref[...].astype(BF16)

    o_ref[...] = x_ref[...] + jnp.dot(a_ref[...], wbf_ref[...], preferred_element_type=F32)


def _mix(merged, w_mix, x2d, tm=1024, tn=1024):
    t, k = merged.shape
    n = w_mix.shape[1]
    return pl.pallas_call(
        _mix_kernel,
        grid=(n // tn, t // tm),
        in_specs=[pl.BlockSpec((tm, k), lambda j, i: (i, 0)),
                  pl.BlockSpec((k, tn), lambda j, i: (0, j)),
                  pl.BlockSpec((tm, tn), lambda j, i: (i, j))],
        out_specs=pl.BlockSpec((tm, tn), lambda j, i: (i, j)),
        out_shape=jax.ShapeDtypeStruct((t, n), F32),
        scratch_shapes=[pltpu.VMEM((k, tn), BF16)],
        compiler_params=_cparams(2),
        name="mix_out",
    )(merged, w_mix, x2d)


def _mlp_kernel(h_ref, wu_ref, wd_ref, o_ref):
    @pl.when(pl.program_id(1) == 0)
    def _():
        o_ref[...] = jnp.zeros_like(o_ref)

    hm = jnp.dot(h_ref[...], wu_ref[...].astype(BF16), preferred_element_type=F32)
    hm = jnp.square(jnp.maximum(hm, 0.0)).astype(BF16)
    o_ref[...] += jnp.dot(hm, wd_ref[...].astype(BF16), preferred_element_type=F32)


def _mlp(h2, w_up, w_down, tm=1024, tf=512):
    t, d = h2.shape
    ff = w_up.shape[1]
    return pl.pallas_call(
        _mlp_kernel,
        grid=(t // tm, ff // tf),
        in_specs=[pl.BlockSpec((tm, d), lambda i, f: (i, 0)),
                  pl.BlockSpec((d, tf), lambda i, f: (0, f)),
                  pl.BlockSpec((tf, d), lambda i, f: (f, 0))],
        out_specs=pl.BlockSpec((tm, d), lambda i, f: (i, 0)),
        out_shape=jax.ShapeDtypeStruct((t, d), F32),
        compiler_params=_cparams(2),
        name="mlp",
    )(h2, w_up, w_down)


def _final_kernel(x_ref, y_ref, g_ref, o_ref):
    x = x_ref[...] + y_ref[...]
    ms = jnp.mean(x * x, axis=-1, keepdims=True)
    o_ref[...] = x * lax.rsqrt(ms + NORM_EPS) * g_ref[...]


def _final(x1, y, g, tm=512):
    t, d = x1.shape
    return pl.pallas_call(
        _final_kernel,
        grid=(t // tm,),
        in_specs=[pl.BlockSpec((tm, d), lambda i: (i, 0)),
                  pl.BlockSpec((tm, d), lambda i: (i, 0)),
                  pl.BlockSpec((1, d), lambda i: (0, 0))],
        out_specs=pl.BlockSpec((tm, d), lambda i: (i, 0)),
        out_shape=jax.ShapeDtypeStruct((t, d), F32),
        compiler_params=_cparams(1),
        name="final_norm",
    )(x1, y, g.reshape(1, d))


def kernel(x, mem, positions, g_mix, w_in, w_dw, b_dw, g_conv_ln, b_conv_ln, w_conv_out, lambda_q1, lambda_k1, lambda_q2, lambda_k2, g_subln, w_da_out, g_mem, w_mem_kv, w_xa_out, w_mix_out, g_mlp, w_up, w_down, g_final):
    bsz, seq, d = x.shape
    t = bsz * seq
    x2d = x.reshape(t, d)

    h = _rmsnorm(x2d, g_mix[0], BF16)
    cos2, sin2 = _rope_tables(positions.astype(F32).reshape(t, 1))

    u = _glu_proj(h, w_in[0])
    p2 = _rest_proj(h, w_in[0], cos2, sin2)
    p3 = p2.reshape(bsz, seq, REST_COLS)

    o_conv = _conv_branch(u.reshape(bsz, seq, CONV_CH), w_dw[0], b_dw[0], g_conv_ln[0], b_conv_ln[0])
    o_da = _diff_attention(p3, lambda_q1[0], lambda_k1[0], lambda_q2[0], lambda_k2[0], g_subln[0])

    mem_n = _rmsnorm(mem.reshape(-1, d), g_mem[0], BF16)
    kv = _proj(mem_n, w_mem_kv[0])
    o_xa = _cross_attention(p3, kv.reshape(bsz, -1, 2 * XA_WIDTH))

    merged = _merge(o_conv.reshape(t, CONV_CH), o_da.reshape(t, DA_WIDTH), o_xa.reshape(t, XA_WIDTH),
                    p2, w_conv_out[0], w_da_out[0], w_xa_out[0])
    x1 = _mix(merged, w_mix_out[0], x2d)

    h2 = _rmsnorm(x1, g_mlp[0], BF16)
    y = _mlp(h2, w_up[0], w_down[0])
    out = _final(x1, y, g_final)
    return out.reshape(bsz, seq, d)
```

```python
import functools
import math

import jax
import jax.numpy as jnp
from jax import lax
from jax.experimental import pallas as pl
from jax.experimental.pallas import tpu as pltpu

F32 = jnp.float32
BF16 = jnp.bfloat16

D_MODEL = 2048
CONV_CH = 1024
CONV_WIDTH = 31
DA_HEADS = 4
DA_HEAD_DIM = 128
DA_WIDTH = 1024
XA_HEADS = 4
XA_HEAD_DIM = 256
XA_WIDTH = 1024
D_FF = 4 * D_MODEL
ROPE_THETA = 10000.0
NORM_EPS = 1e-6
LAMBDA_INIT = 0.8 - 0.6 * math.exp(-0.3 * 0)
LOG2E = math.log2(math.e)

QK_COL0 = 2 * CONV_CH
VX_COL0 = QK_COL0 + 2048
GATE_COL0 = VX_COL0 + 2048

V7X_VMEM_LIMIT = 56 * 1024 * 1024
CONV_HALO = 32
SUBLANES = 8


def _cparams(n_axes):
    return pltpu.CompilerParams(
        dimension_semantics=("arbitrary",) * n_axes,
        vmem_limit_bytes=V7X_VMEM_LIMIT,
    )


def _sigmoid(x):
    return 0.5 * jnp.tanh(0.5 * x) + 0.5


def _rms_scale(x, g):
    ms = jnp.mean(x * x, axis=-1, keepdims=True)
    return x * lax.rsqrt(ms + NORM_EPS) * g


def _rmsnorm_kernel(x_ref, g_ref, o_ref):
    o_ref[...] = _rms_scale(x_ref[...], g_ref[...]).astype(o_ref.dtype)


def _rmsnorm(x2d, g, out_dtype, tm=512):
    m, d = x2d.shape
    return pl.pallas_call(
        _rmsnorm_kernel,
        grid=(m // tm,),
        in_specs=[pl.BlockSpec((tm, d), lambda i: (i, 0)),
                  pl.BlockSpec((1, d), lambda i: (0, 0))],
        out_specs=pl.BlockSpec((tm, d), lambda i: (i, 0)),
        out_shape=jax.ShapeDtypeStruct((m, d), out_dtype),
        compiler_params=_cparams(1),
        name="rmsnorm",
    )(x2d, g.reshape(1, d))


def _rope_kernel(pos_ref, f_ref, cos_ref, sin_ref):
    ang = pos_ref[...] * f_ref[...]
    lane = lax.broadcasted_iota(jnp.int32, ang.shape, 1)
    sign = jnp.where(lane < DA_HEAD_DIM // 2, -1.0, 1.0)
    cos_ref[...] = jnp.cos(ang)
    sin_ref[...] = jnp.sin(ang) * sign


def _rope_tables(pos_col, tm=1024):
    t = pos_col.shape[0]
    inv_freq = 1.0 / (ROPE_THETA ** (jnp.arange(0, DA_HEAD_DIM, 2, dtype=F32) / DA_HEAD_DIM))
    f2 = jnp.concatenate([inv_freq, inv_freq]).reshape(1, DA_HEAD_DIM)
    return pl.pallas_call(
        _rope_kernel,
        grid=(t // tm,),
        in_specs=[pl.BlockSpec((tm, 1), lambda i: (i, 0)),
                  pl.BlockSpec((1, DA_HEAD_DIM), lambda i: (0, 0))],
        out_specs=[pl.BlockSpec((tm, DA_HEAD_DIM), lambda i: (i, 0))] * 2,
        out_shape=[jax.ShapeDtypeStruct((t, DA_HEAD_DIM), F32)] * 2,
        compiler_params=_cparams(1),
        name="rope_tables",
    )(pos_col, f2)


def _glu_kernel(h_ref, wa_ref, wb_ref, o_ref, wbf_ref, *, tn):
    @pl.when(pl.program_id(1) == 0)
    def _():
        wbf_ref[:, :tn] = wa_ref[...].astype(BF16)
        wbf_ref[:, tn:] = wb_ref[...].astype(BF16)

    acc = jnp.dot(h_ref[...], wbf_ref[...], preferred_element_type=F32)
    o_ref[...] = (acc[:, :tn] * _sigmoid(acc[:, tn:])).astype(o_ref.dtype)


def _glu_proj(h, w_in, tm=1024, tn=512):
    t, d = h.shape
    nb = CONV_CH // tn
    return pl.pallas_call(
        functools.partial(_glu_kernel, tn=tn),
        grid=(nb, t // tm),
        in_specs=[pl.BlockSpec((tm, d), lambda j, i: (i, 0)),
                  pl.BlockSpec((d, tn), lambda j, i: (0, j)),
                  pl.BlockSpec((d, tn), lambda j, i: (0, nb + j))],
        out_specs=pl.BlockSpec((tm, tn), lambda j, i: (i, j)),
        out_shape=jax.ShapeDtypeStruct((t, CONV_CH), BF16),
        scratch_shapes=[pltpu.VMEM((d, 2 * tn), BF16)],
        compiler_params=_cparams(2),
        name="glu_proj",
    )(h, w_in, w_in)


def _qk_kernel(h_ref, w_ref, cos_ref, sin_ref, o_ref, wbf_ref, *, tn):
    @pl.when(pl.program_id(1) == 0)
    def _():
        wbf_ref[...] = w_ref[...].astype(BF16)

    acc = jnp.dot(h_ref[...], wbf_ref[...], preferred_element_type=F32)
    scale = jnp.where(pl.program_id(0) == 0, DA_HEAD_DIM ** -0.5 * LOG2E, 1.0).astype(F32)
    c = cos_ref[...] * scale
    s = sin_ref[...] * scale
    for g in range(tn // DA_HEAD_DIM):
        sl = slice(g * DA_HEAD_DIM, (g + 1) * DA_HEAD_DIM)
        t = acc[:, sl]
        r = pltpu.roll(t, DA_HEAD_DIM // 2, axis=1)
        o_ref[:, sl] = (t * c + r * s).astype(o_ref.dtype)


def _vx_kernel(h_ref, w_ref, o_ref, wbf_ref):
    @pl.when(pl.program_id(1) == 0)
    def _():
        wbf_ref[...] = w_ref[...].astype(BF16)

    acc = jnp.dot(h_ref[...], wbf_ref[...], preferred_element_type=F32)
    scale = jnp.where(pl.program_id(0) == 1, XA_HEAD_DIM ** -0.5, 1.0).astype(F32)
    o_ref[...] = (acc * scale).astype(o_ref.dtype)


def _gate_kernel(h_ref, w_ref, o_ref, wbf_ref):
    @pl.when(pl.program_id(1) == 0)
    def _():
        wbf_ref[...] = w_ref[...].astype(BF16)

    acc = jnp.dot(h_ref[...], wbf_ref[...], preferred_element_type=F32)
    o_ref[...] = _sigmoid(acc).astype(o_ref.dtype)


def _in_proj(body, name, h, w_in, col0, ncols, extra=(), tm=1024, tn=1024):
    t, d = h.shape
    cb = col0 // tn
    extra_specs = [pl.BlockSpec((tm, e.shape[1]), lambda j, i: (i, 0)) for e in extra]
    return pl.pallas_call(
        body,
        grid=(ncols // tn, t // tm),
        in_specs=[pl.BlockSpec((tm, d), lambda j, i: (i, 0)),
                  pl.BlockSpec((d, tn), lambda j, i: (0, cb + j))] + extra_specs,
        out_specs=pl.BlockSpec((tm, tn), lambda j, i: (i, j)),
        out_shape=jax.ShapeDtypeStruct((t, ncols), BF16),
        scratch_shapes=[pltpu.VMEM((d, tn), BF16)],
        compiler_params=_cparams(2),
        name=name,
    )(h, w_in, *extra)


def _conv_kernel(ucur_ref, uprev_ref, w_ref, bdw_ref, g_ref, b_ref, o_ref, buf_ref, sh_ref, wb_ref, *, ts, rc):
    si = pl.program_id(1)

    @pl.when((pl.program_id(0) == 0) & (si == 0))
    def _():
        for k in range(CONV_WIDTH):
            wb_ref[k] = jnp.broadcast_to(w_ref[k:k + 1, :], (SUBLANES, CONV_CH))

    prev = uprev_ref[0].astype(F32)
    buf_ref[0:CONV_HALO, :] = jnp.where(si > 0, prev, 0.0)
    buf_ref[CONV_HALO:CONV_HALO + ts, :] = ucur_ref[0].astype(F32)
    buf_ref[CONV_HALO + ts:, :] = jnp.zeros((SUBLANES, CONV_CH), F32)
    n_sh = ts + CONV_HALO
    for r in range(SUBLANES):
        sh_ref[r] = buf_ref[pl.ds(r, n_sh), :]

    lead = CONV_HALO - (CONV_WIDTH - 1)
    ng = rc // SUBLANES

    def chunk(ci, carry):
        r0 = pl.multiple_of(ci * rc, rc)
        accs = [bdw_ref[...] + jnp.zeros((SUBLANES, CONV_CH), F32) for _ in range(ng)]
        for k in range(CONV_WIDTH):
            off = lead + k
            a, r = off // SUBLANES, off % SUBLANES
            wk = wb_ref[k]
            for gi in range(ng):
                row0 = pl.multiple_of(r0 + SUBLANES * (a + gi), SUBLANES)
                accs[gi] = accs[gi] + sh_ref[r, pl.ds(row0, SUBLANES), :] * wk
        ys = []
        for acc in accs:
            mu = jnp.mean(acc, axis=-1, keepdims=True)
            cen = acc - mu
            var = jnp.mean(cen * cen, axis=-1, keepdims=True)
            y = cen * lax.rsqrt(var + NORM_EPS) * g_ref[...] + b_ref[...]
            ys.append(y * _sigmoid(y))
        o_ref[0, pl.ds(r0, rc), :] = jnp.concatenate(ys, axis=0).astype(o_ref.dtype)
        return carry

    lax.fori_loop(0, ts // rc, chunk, 0)


def _conv_branch(u3, w_dw, b_dw, g_ln, b_ln, ts=256, rc=32):
    b, s, c = u3.shape
    hb = ts // CONV_HALO
    return pl.pallas_call(
        functools.partial(_conv_kernel, ts=ts, rc=rc),
        grid=(b, s // ts),
        in_specs=[pl.BlockSpec((1, ts, c), lambda bi, si: (bi, si, 0)),
                  pl.BlockSpec((1, CONV_HALO, c), lambda bi, si: (bi, jnp.maximum(si * hb - 1, 0), 0)),
                  pl.BlockSpec((CONV_WIDTH, c), lambda bi, si: (0, 0)),
                  pl.BlockSpec((1, c), lambda bi, si: (0, 0)),
                  pl.BlockSpec((1, c), lambda bi, si: (0, 0)),
                  pl.BlockSpec((1, c), lambda bi, si: (0, 0))],
        out_specs=pl.BlockSpec((1, ts, c), lambda bi, si: (bi, si, 0)),
        out_shape=jax.ShapeDtypeStruct((b, s, c), BF16),
        scratch_shapes=[pltpu.VMEM((CONV_HALO + ts + SUBLANES, c), F32),
                        pltpu.VMEM((SUBLANES, CONV_HALO + ts, c), F32),
                        pltpu.VMEM((CONV_WIDTH, SUBLANES, c), F32)],
        compiler_params=_cparams(2),
        name="conv_branch",
    )(u3, u3, w_dw, b_dw.reshape(1, c), g_ln.reshape(1, c), b_ln.reshape(1, c))


def _da_kernel(lq1_ref, lk1_ref, lq2_ref, lk2_ref, g_ref, q_ref, k_ref, v_ref, o_ref, *, tq, nq):
    qi = pl.program_id(2)
    hd = DA_HEAD_DIM
    nt = (((1,), (1,)), ((), ()))
    lam = (jnp.exp(jnp.sum(lq1_ref[...] * lk1_ref[...], axis=-1, keepdims=True))
           - jnp.exp(jnp.sum(lq2_ref[...] * lk2_ref[...], axis=-1, keepdims=True))
           + LAMBDA_INIT)

    def tile(n):
        off = n * tq
        q = q_ref[0]
        causal = (lax.broadcasted_iota(jnp.int32, (tq, tq), 0)
                  >= lax.broadcasted_iota(jnp.int32, (tq, tq), 1))
        p_off, p_dg, inv_l = [], [], []
        for mp in range(2):
            cs = slice(mp * hd, (mp + 1) * hd)
            sd = lax.dot_general(q[:, cs], k_ref[0, off:off + tq, cs], nt, preferred_element_type=F32)
            sd = jnp.where(causal, sd, -jnp.inf)
            m = jnp.max(sd, axis=-1, keepdims=True)
            if n > 0:
                so = lax.dot_general(q[:, cs], k_ref[0, 0:off, cs], nt, preferred_element_type=F32)
                m = jnp.maximum(m, jnp.max(so, axis=-1, keepdims=True))
                po = jnp.exp2(so - m)
                l = jnp.sum(po, axis=-1, keepdims=True)
                p_off.append(po.astype(BF16))
            else:
                l = 0.0
            pd = jnp.exp2(sd - m)
            l = l + jnp.sum(pd, axis=-1, keepdims=True)
            p_dg.append(pd.astype(BF16))
            inv_l.append(1.0 / l)
        acc = jnp.dot(jnp.concatenate(p_dg, axis=0), v_ref[0, off:off + tq, :], preferred_element_type=F32)
        if n > 0:
            acc = acc + jnp.dot(jnp.concatenate(p_off, axis=0), v_ref[0, 0:off, :],
                                preferred_element_type=F32)
        dlt = acc[:tq] * inv_l[0] - lam * (acc[tq:] * inv_l[1])
        y = _rms_scale(dlt, g_ref[...]) * (1.0 - LAMBDA_INIT)
        o_ref[0] = y.astype(o_ref.dtype)

    for n in range(nq):
        pl.when(qi == n)(functools.partial(tile, n))


def _diff_attention(qk3, vx3, lq1, lk1, lq2, lk2, g_subln, tq=256):
    b, s, _ = qk3.shape
    hw = 2 * DA_HEAD_DIM
    nq = s // tq
    vec = lambda: pl.BlockSpec((1, DA_HEAD_DIM), lambda bi, h, qi: (0, 0))
    return pl.pallas_call(
        functools.partial(_da_kernel, tq=tq, nq=nq),
        grid=(b, DA_HEADS, nq),
        in_specs=[vec(), vec(), vec(), vec(),
                  pl.BlockSpec((1, hw), lambda bi, h, qi: (0, 0)),
                  pl.BlockSpec((1, tq, hw), lambda bi, h, qi: (bi, qi, h)),
                  pl.BlockSpec((1, s, hw), lambda bi, h, qi: (bi, 0, DA_HEADS + h)),
                  pl.BlockSpec((1, s, hw), lambda bi, h, qi: (bi, 0, h))],
        out_specs=pl.BlockSpec((1, tq, hw), lambda bi, h, qi: (bi, qi, h)),
        out_shape=jax.ShapeDtypeStruct((b, s, DA_WIDTH), BF16),
        compiler_params=_cparams(3),
        name="diff_attention",
    )(lq1.reshape(1, -1), lk1.reshape(1, -1), lq2.reshape(1, -1), lk2.reshape(1, -1),
      g_subln.reshape(1, hw), qk3, qk3, vx3)


def _proj_kernel(a_ref, w_ref, o_ref, wbf_ref):
    @pl.when(pl.program_id(1) == 0)
    def _():
        wbf_ref[...] = w_ref[...].astype(BF16)

    o_ref[...] = jnp.dot(a_ref[...], wbf_ref[...], preferred_element_type=F32).astype(o_ref.dtype)


def _proj(a, w, tm=1024, tn=1024):
    m, k = a.shape
    n = w.shape[1]
    return pl.pallas_call(
        _proj_kernel,
        grid=(n // tn, m // tm),
        in_specs=[pl.BlockSpec((tm, k), lambda j, i: (i, 0)),
                  pl.BlockSpec((k, tn), lambda j, i: (0, j))],
        out_specs=pl.BlockSpec((tm, tn), lambda j, i: (i, j)),
        out_shape=jax.ShapeDtypeStruct((m, n), BF16),
        scratch_shapes=[pltpu.VMEM((k, tn), BF16)],
        compiler_params=_cparams(2),
        name="proj",
    )(a, w)


def _xa_kernel(q_ref, k_ref, v_ref, o_ref):
    nt = (((1,), (1,)), ((), ()))
    s = lax.dot_general(q_ref[0], k_ref[0], nt, preferred_element_type=F32)
    m = jnp.max(s, axis=-1, keepdims=True)
    p = jnp.exp(s - m)
    l = jnp.sum(p, axis=-1, keepdims=True)
    o = jnp.dot(p.astype(BF16), v_ref[0], preferred_element_type=F32)
    o_ref[0] = (o * (1.0 / l)).astype(o_ref.dtype)


def _cross_attention(vx3, kv3, ts=512):
    b, s, _ = vx3.shape
    mem_len = kv3.shape[1]
    hd = XA_HEAD_DIM
    return pl.pallas_call(
        _xa_kernel,
        grid=(b, s // ts, XA_HEADS),
        in_specs=[pl.BlockSpec((1, ts, hd), lambda bi, si, h: (bi, si, XA_HEADS + h)),
                  pl.BlockSpec((1, mem_len, hd), lambda bi, si, h: (bi, 0, h)),
                  pl.BlockSpec((1, mem_len, hd), lambda bi, si, h: (bi, 0, XA_HEADS + h))],
        out_specs=pl.BlockSpec((1, ts, hd), lambda bi, si, h: (bi, si, h)),
        out_shape=jax.ShapeDtypeStruct((b, s, XA_WIDTH), BF16),
        compiler_params=_cparams(3),
        name="cross_attention",
    )(vx3, kv3, kv3)


def _merge_kernel(oc_ref, od_ref, ox_ref, g0_ref, g1_ref, g2_ref, wc_ref, wd_ref, wx_ref, o_ref, wbf_ref):
    @pl.when(pl.program_id(1) == 0)
    def _():
        wbf_ref[0] = wc_ref[...].astype(BF16)
        wbf_ref[1] = wd_ref[...].astype(BF16)
        wbf_ref[2] = wx_ref[...].astype(BF16)

    y = g0_ref[...].astype(F32) * jnp.dot(oc_ref[...], wbf_ref[0], preferred_element_type=F32)
    y = y + g1_ref[...].astype(F32) * jnp.dot(od_ref[...], wbf_ref[1], preferred_element_type=F32)
    y = y + g2_ref[...].astype(F32) * jnp.dot(ox_ref[...], wbf_ref[2], preferred_element_type=F32)
    o_ref[...] = y.astype(o_ref.dtype)


def _merge(o_conv, o_da, o_xa, gates, w_conv_out, w_da_out, w_xa_out, tm=1024, tn=512):
    t, kk = o_conv.shape
    gstride = D_MODEL // tn
    act = lambda: pl.BlockSpec((tm, kk), lambda j, i: (i, 0))
    wsp = lambda: pl.BlockSpec((kk, tn), lambda j, i: (0, j))
    gate = lambda br: pl.BlockSpec((tm, tn), lambda j, i: (i, br * gstride + j))
    return pl.pallas_call(
        _merge_kernel,
        grid=(D_MODEL // tn, t // tm),
        in_specs=[act(), act(), act(), gate(0), gate(1), gate(2), wsp(), wsp(), wsp()],
        out_specs=pl.BlockSpec((tm, tn), lambda j, i: (i, j)),
        out_shape=jax.ShapeDtypeStruct((t, D_MODEL), BF16),
        scratch_shapes=[pltpu.VMEM((3, kk, tn), BF16)],
        compiler_params=_cparams(2),
        name="merge",
    )(o_conv, o_da, o_xa, gates, gates, gates, w_conv_out, w_da_out, w_xa_out)


def _mix_kernel(a_ref, w_ref, x_ref, o_ref, wbf_ref):
    @pl.when(pl.program_id(1) == 0)
    def _():
        wbf_ref[...] = w_ref[...].astype(BF16)

    o_ref[...] = x_ref[...] + jnp.dot(a_ref[...], wbf_ref[...], preferred_element_type=F32)


def _mix(merged, w_mix, x2d, tm=1024, tn=1024):
    t, k = merged.shape
    n = w_mix.shape[1]
    return pl.pallas_call(
        _mix_kernel,
        grid=(n // tn, t // tm),
        in_specs=[pl.BlockSpec((tm, k), lambda j, i: (i, 0)),
                  pl.BlockSpec((k, tn), lambda j, i: (0, j)),
                  pl.BlockSpec((tm, tn), lambda j, i: (i, j))],
        out_specs=pl.BlockSpec((tm, tn), lambda j, i: (i, j)),
        out_shape=jax.ShapeDtypeStruct((t, n), F32),
        scratch_shapes=[pltpu.VMEM((k, tn), BF16)],
        compiler_params=_cparams(2),
        name="mix_out",
    )(merged, w_mix, x2d)


def _mlp_kernel(x_ref, gm_ref, wu_ref, wd_ref, gf_ref, o_ref, h_ref, *, nf):
    f = pl.program_id(1)

    @pl.when(f == 0)
    def _():
        h_ref[...] = _rms_scale(x_ref[...], gm_ref[...]).astype(BF16)
        o_ref[...] = jnp.zeros_like(o_ref)

    hm = jnp.dot(h_ref[...], wu_ref[...].astype(BF16), preferred_element_type=F32)
    hm = jnp.square(jnp.maximum(hm, 0.0)).astype(BF16)
    o_ref[...] += jnp.dot(hm, wd_ref[...].astype(BF16), preferred_element_type=F32)

    @pl.when(f == nf - 1)
    def _():
        o_ref[...] = _rms_scale(x_ref[...] + o_ref[...], gf_ref[...])


def _mlp(x1, g_mlp, w_up, w_down, g_final, tm=1024, tf=512):
    t, d = x1.shape
    ff = w_up.shape[1]
    nf = ff // tf
    return pl.pallas_call(
        functools.partial(_mlp_kernel, nf=nf),
        grid=(t // tm, nf),
        in_specs=[pl.BlockSpec((tm, d), lambda i, f: (i, 0), pipeline_mode=pl.Buffered(1)),
                  pl.BlockSpec((1, d), lambda i, f: (0, 0)),
                  pl.BlockSpec((d, tf), lambda i, f: (0, f)),
                  pl.BlockSpec((tf, d), lambda i, f: (f, 0)),
                  pl.BlockSpec((1, d), lambda i, f: (0, 0))],
        out_specs=pl.BlockSpec((tm, d), lambda i, f: (i, 0)),
        out_shape=jax.ShapeDtypeStruct((t, d), F32),
        scratch_shapes=[pltpu.VMEM((tm, d), BF16)],
        compiler_params=_cparams(2),
        name="mlp",
    )(x1, g_mlp.reshape(1, d), w_up, w_down, g_final.reshape(1, d))


def kernel(x, mem, positions, g_mix, w_in, w_dw, b_dw, g_conv_ln, b_conv_ln, w_conv_out, lambda_q1, lambda_k1, lambda_q2, lambda_k2, g_subln, w_da_out, g_mem, w_mem_kv, w_xa_out, w_mix_out, g_mlp, w_up, w_down, g_final):
    bsz, seq, d = x.shape
    t = bsz * seq
    x2d = x.reshape(t, d)
    w_in0 = w_in[0]

    h = _rmsnorm(x2d, g_mix[0], BF16)
    cos2, sin2 = _rope_tables(positions.astype(F32).reshape(t, 1))

    u = _glu_proj(h, w_in0)
    qk = _in_proj(functools.partial(_qk_kernel, tn=1024), "qk_proj", h, w_in0, QK_COL0, 2048,
                  extra=(cos2, sin2))
    vx = _in_proj(_vx_kernel, "vx_proj", h, w_in0, VX_COL0, 2048)
    gates = _in_proj(_gate_kernel, "gate_proj", h, w_in0, GATE_COL0, 3 * D_MODEL)
    qk3 = qk.reshape(bsz, seq, 2048)
    vx3 = vx.reshape(bsz, seq, 2048)

    o_conv = _conv_branch(u.reshape(bsz, seq, CONV_CH), w_dw[0], b_dw[0], g_conv_ln[0], b_conv_ln[0])
    o_da = _diff_attention(qk3, vx3, lambda_q1[0], lambda_k1[0], lambda_q2[0], lambda_k2[0], g_subln[0])

    mem_n = _rmsnorm(mem.reshape(-1, d), g_mem[0], BF16)
    kv = _proj(mem_n, w_mem_kv[0])
    o_xa = _cross_attention(vx3, kv.reshape(bsz, -1, 2 * XA_WIDTH))

    merged = _merge(o_conv.reshape(t, CONV_CH), o_da.reshape(t, DA_WIDTH), o_xa.reshape(t, XA_WIDTH),
                    gates, w_conv_out[0], w_da_out[0], w_xa_out[0])
    x1 = _mix(merged, w_mix_out[0], x2d)
    out = _mlp(x1, g_mlp[0], w_up[0], w_down[0], g_final)
    return out.reshape(bsz, seq, d)
```

```python
import functools
import math

import jax
import jax.numpy as jnp
from jax import lax
from jax.experimental import pallas as pl
from jax.experimental.pallas import tpu as pltpu

F32 = jnp.float32
BF16 = jnp.bfloat16

D_MODEL = 2048
CONV_CH = 1024
CONV_WIDTH = 31
DA_HEADS = 4
DA_HEAD_DIM = 128
DA_WIDTH = 1024
XA_HEADS = 4
XA_HEAD_DIM = 256
XA_WIDTH = 1024
D_FF = 4 * D_MODEL
ROPE_THETA = 10000.0
NORM_EPS = 1e-6
LAMBDA_INIT = 0.8 - 0.6 * math.exp(-0.3 * 0)
LOG2E = math.log2(math.e)

QK_COL0 = 2 * CONV_CH
VX_COL0 = QK_COL0 + 2048
GATE_COL0 = VX_COL0 + 2048

V7X_VMEM_LIMIT = 60 * 1024 * 1024
CONV_HALO = 32
SUBLANES = 8
LANES = 128


def _cparams(n_axes):
    return pltpu.CompilerParams(
        dimension_semantics=("arbitrary",) * n_axes,
        vmem_limit_bytes=V7X_VMEM_LIMIT,
    )


def _sigmoid(x):
    return 0.5 * jnp.tanh(0.5 * x) + 0.5


def _rms_scale(x, g):
    ms = jnp.mean(x * x, axis=-1, keepdims=True)
    return x * lax.rsqrt(ms + NORM_EPS) * g


def _rmsnorm_kernel(x_ref, g_ref, o_ref):
    o_ref[...] = _rms_scale(x_ref[...], g_ref[...]).astype(o_ref.dtype)


def _rmsnorm(x2d, g, out_dtype, tm=512):
    m, d = x2d.shape
    return pl.pallas_call(
        _rmsnorm_kernel,
        grid=(m // tm,),
        in_specs=[pl.BlockSpec((tm, d), lambda i: (i, 0)),
                  pl.BlockSpec((1, d), lambda i: (0, 0))],
        out_specs=pl.BlockSpec((tm, d), lambda i: (i, 0)),
        out_shape=jax.ShapeDtypeStruct((m, d), out_dtype),
        compiler_params=_cparams(1),
        name="rmsnorm",
    )(x2d, g.reshape(1, d))


def _glu_kernel(h_ref, wa_ref, wb_ref, o_ref, wbf_ref, *, tn):
    @pl.when(pl.program_id(1) == 0)
    def _():
        wbf_ref[:, :tn] = wa_ref[...].astype(BF16)
        wbf_ref[:, tn:] = wb_ref[...].astype(BF16)

    acc = jnp.dot(h_ref[...], wbf_ref[...], preferred_element_type=F32)
    o_ref[...] = (acc[:, :tn] * _sigmoid(acc[:, tn:])).astype(o_ref.dtype)


def _glu_proj(h, w_in, tm=2048, tn=512):
    t, d = h.shape
    nb = CONV_CH // tn
    return pl.pallas_call(
        functools.partial(_glu_kernel, tn=tn),
        grid=(nb, t // tm),
        in_specs=[pl.BlockSpec((tm, d), lambda j, i: (i, 0)),
                  pl.BlockSpec((d, tn), lambda j, i: (0, j)),
                  pl.BlockSpec((d, tn), lambda j, i: (0, nb + j))],
        out_specs=pl.BlockSpec((tm, tn), lambda j, i: (i, j)),
        out_shape=jax.ShapeDtypeStruct((t, CONV_CH), BF16),
        scratch_shapes=[pltpu.VMEM((d, 2 * tn), BF16)],
        compiler_params=_cparams(2),
        name="glu_proj",
    )(h, w_in, w_in)


def _rope_kernel(pos_ref, f_ref, cos_ref, sin_ref):
    ang = pos_ref[...] * f_ref[...]
    lane = lax.broadcasted_iota(jnp.int32, ang.shape, 1)
    sign = jnp.where(lane < DA_HEAD_DIM // 2, -1.0, 1.0)
    cos_ref[...] = jnp.cos(ang)
    sin_ref[...] = jnp.sin(ang) * sign


def _rope_tables(pos_col, tm=1024):
    t = pos_col.shape[0]
    inv_freq = 1.0 / (ROPE_THETA ** (jnp.arange(0, DA_HEAD_DIM, 2, dtype=F32) / DA_HEAD_DIM))
    f2 = jnp.concatenate([inv_freq, inv_freq]).reshape(1, DA_HEAD_DIM)
    return pl.pallas_call(
        _rope_kernel,
        grid=(t // tm,),
        in_specs=[pl.BlockSpec((tm, 1), lambda i: (i, 0)),
                  pl.BlockSpec((1, DA_HEAD_DIM), lambda i: (0, 0))],
        out_specs=[pl.BlockSpec((tm, DA_HEAD_DIM), lambda i: (i, 0))] * 2,
        out_shape=[jax.ShapeDtypeStruct((t, DA_HEAD_DIM), F32)] * 2,
        compiler_params=_cparams(1),
        name="rope_tables",
    )(pos_col, f2)


def _qk_kernel(h_ref, w_ref, cos_ref, sin_ref, o_ref, wbf_ref, *, tn):
    @pl.when(pl.program_id(1) == 0)
    def _():
        wbf_ref[...] = w_ref[...].astype(BF16)

    acc = jnp.dot(h_ref[...], wbf_ref[...], preferred_element_type=F32)
    scale = jnp.where(pl.program_id(0) == 0, DA_HEAD_DIM ** -0.5 * LOG2E, 1.0).astype(F32)
    c = cos_ref[...] * scale
    s = sin_ref[...] * scale
    for g in range(tn // DA_HEAD_DIM):
        sl = slice(g * DA_HEAD_DIM, (g + 1) * DA_HEAD_DIM)
        t = acc[:, sl]
        r = pltpu.roll(t, DA_HEAD_DIM // 2, axis=1)
        o_ref[:, sl] = (t * c + r * s).astype(o_ref.dtype)


def _vx_kernel(h_ref, w_ref, o_ref, wbf_ref):
    @pl.when(pl.program_id(1) == 0)
    def _():
        wbf_ref[...] = w_ref[...].astype(BF16)

    acc = jnp.dot(h_ref[...], wbf_ref[...], preferred_element_type=F32)
    scale = jnp.where(pl.program_id(0) == 1, XA_HEAD_DIM ** -0.5 * LOG2E, 1.0).astype(F32)
    o_ref[...] = (acc * scale).astype(o_ref.dtype)


def _gate_kernel(h_ref, w_ref, o_ref, wbf_ref):
    @pl.when(pl.program_id(1) == 0)
    def _():
        wbf_ref[...] = w_ref[...].astype(BF16)

    acc = jnp.dot(h_ref[...], wbf_ref[...], preferred_element_type=F32)
    o_ref[...] = _sigmoid(acc).astype(o_ref.dtype)


def _in_proj(body, name, h, w_in, col0, ncols, extra=(), extra_specs=(), tm=1024, tn=1024):
    t, d = h.shape
    cb = col0 // tn
    extra_specs = list(extra_specs)
    return pl.pallas_call(
        body,
        grid=(ncols // tn, t // tm),
        in_specs=[pl.BlockSpec((tm, d), lambda j, i: (i, 0)),
                  pl.BlockSpec((d, tn), lambda j, i: (0, cb + j))] + extra_specs,
        out_specs=pl.BlockSpec((tm, tn), lambda j, i: (i, j)),
        out_shape=jax.ShapeDtypeStruct((t, ncols), BF16),
        scratch_shapes=[pltpu.VMEM((d, tn), BF16)],
        compiler_params=_cparams(2),
        name=name,
    )(h, w_in, *extra)


def _conv_kernel(ucur_ref, uprev_ref, w_ref, bdw_ref, g_ref, b_ref, o_ref, buf_ref, sh_ref, y_ref, *, ts, rg, rc):
    si = pl.program_id(1)
    prev = uprev_ref[0].astype(F32)
    buf_ref[0:CONV_HALO, :] = jnp.where(si > 0, prev, 0.0)
    buf_ref[CONV_HALO:CONV_HALO + ts, :] = ucur_ref[0].astype(F32)
    buf_ref[CONV_HALO + ts:, :] = jnp.zeros((SUBLANES, CONV_CH), F32)
    n_sh = ts + CONV_HALO
    for r in range(SUBLANES):
        sh_ref[r] = buf_ref[pl.ds(r, n_sh), :]

    lead = CONV_HALO - (CONV_WIDTH - 1)
    rows_per_iter = rg * SUBLANES

    for c in range(CONV_CH // LANES):
        lanes = slice(c * LANES, (c + 1) * LANES)
        taps = [jnp.broadcast_to(w_ref[k:k + 1, lanes], (SUBLANES, LANES)) for k in range(CONV_WIDTH)]
        bias = jnp.broadcast_to(bdw_ref[:, lanes], (SUBLANES, LANES))

        def rows_body(it, carry, lanes=lanes, taps=taps, bias=bias):
            r0 = pl.multiple_of(it * rows_per_iter, rows_per_iter)
            for gi in range(rg):
                acc = bias
                for k in range(CONV_WIDTH):
                    a, r = divmod(lead + k, SUBLANES)
                    row0 = pl.multiple_of(r0 + SUBLANES * (a + gi), SUBLANES)
                    acc = acc + sh_ref[r, pl.ds(row0, SUBLANES), lanes] * taps[k]
                y_ref[pl.ds(pl.multiple_of(r0 + SUBLANES * gi, SUBLANES), SUBLANES), lanes] = acc
            return carry

        lax.fori_loop(0, ts // rows_per_iter, rows_body, 0)

    for ci in range(ts // rc):
        rows = slice(ci * rc, (ci + 1) * rc)
        acc = y_ref[rows, :]
        mu = jnp.mean(acc, axis=-1, keepdims=True)
        cen = acc - mu
        var = jnp.mean(cen * cen, axis=-1, keepdims=True)
        y = cen * lax.rsqrt(var + NORM_EPS) * g_ref[...] + b_ref[...]
        o_ref[0, rows, :] = (y * _sigmoid(y)).astype(o_ref.dtype)


def _conv_branch(u3, w_dw, b_dw, g_ln, b_ln, ts=256, rg=8, rc=128):
    b, s, c = u3.shape
    hb = ts // CONV_HALO
    return pl.pallas_call(
        functools.partial(_conv_kernel, ts=ts, rg=rg, rc=rc),
        grid=(b, s // ts),
        in_specs=[pl.BlockSpec((1, ts, c), lambda bi, si: (bi, si, 0)),
                  pl.BlockSpec((1, CONV_HALO, c), lambda bi, si: (bi, jnp.maximum(si * hb - 1, 0), 0)),
                  pl.BlockSpec((CONV_WIDTH, c), lambda bi, si: (0, 0)),
                  pl.BlockSpec((1, c), lambda bi, si: (0, 0)),
                  pl.BlockSpec((1, c), lambda bi, si: (0, 0)),
                  pl.BlockSpec((1, c), lambda bi, si: (0, 0))],
        out_specs=pl.BlockSpec((1, ts, c), lambda bi, si: (bi, si, 0)),
        out_shape=jax.ShapeDtypeStruct((b, s, c), BF16),
        scratch_shapes=[pltpu.VMEM((CONV_HALO + ts + SUBLANES, c), F32),
                        pltpu.VMEM((SUBLANES, CONV_HALO + ts, c), F32),
                        pltpu.VMEM((ts, c), F32)],
        compiler_params=_cparams(2),
        name="conv_branch",
    )(u3, u3, w_dw, b_dw.reshape(1, c), g_ln.reshape(1, c), b_ln.reshape(1, c))


def _da_kernel(lq1_ref, lk1_ref, lq2_ref, lk2_ref, g_ref, q_ref, k_ref, v_ref, o_ref, *, tq, nq, tpg):
    qi = pl.program_id(2)
    hd = DA_HEAD_DIM
    nt = (((1,), (1,)), ((), ()))
    lam = (jnp.exp(jnp.sum(lq1_ref[...] * lk1_ref[...], axis=-1, keepdims=True))
           - jnp.exp(jnp.sum(lq2_ref[...] * lk2_ref[...], axis=-1, keepdims=True))
           + LAMBDA_INIT)

    def tile(n, part):
        off = n * tq
        rows = slice(part * tq, (part + 1) * tq)
        q = q_ref[0, rows, :]
        causal = (lax.broadcasted_iota(jnp.int32, (tq, tq), 0)
                  >= lax.broadcasted_iota(jnp.int32, (tq, tq), 1))
        p_off, p_dg, inv_l = [], [], []
        for mp in range(2):
            cs = slice(mp * hd, (mp + 1) * hd)
            sd = lax.dot_general(q[:, cs], k_ref[0, off:off + tq, cs], nt, preferred_element_type=F32)
            sd = jnp.where(causal, sd, -jnp.inf)
            m = jnp.max(sd, axis=-1, keepdims=True)
            if n > 0:
                so = lax.dot_general(q[:, cs], k_ref[0, 0:off, cs], nt, preferred_element_type=F32)
                m = jnp.maximum(m, jnp.max(so, axis=-1, keepdims=True))
                po = jnp.exp2(so - m)
                l = jnp.sum(po, axis=-1, keepdims=True)
                p_off.append(po.astype(BF16))
            else:
                l = 0.0
            pd = jnp.exp2(sd - m)
            l = l + jnp.sum(pd, axis=-1, keepdims=True)
            p_dg.append(pd.astype(BF16))
            inv_l.append(1.0 / l)
        acc = jnp.dot(jnp.concatenate(p_dg, axis=0), v_ref[0, off:off + tq, :], preferred_element_type=F32)
        if n > 0:
            acc = acc + jnp.dot(jnp.concatenate(p_off, axis=0), v_ref[0, 0:off, :],
                                preferred_element_type=F32)
        dlt = acc[:tq] * inv_l[0] - lam * (acc[tq:] * inv_l[1])
        y = _rms_scale(dlt, g_ref[...]) * (1.0 - LAMBDA_INIT)
        o_ref[0, rows, :] = y.astype(o_ref.dtype)

    def group(gq):
        for part in range(tpg):
            tile(gq * tpg + part, part)

    for gq in range(nq // tpg):
        pl.when(qi == gq)(functools.partial(group, gq))


def _diff_attention(qk3, vx3, lq1, lk1, lq2, lk2, g_subln, tq=256, tpg=1):
    b, s, _ = qk3.shape
    hw = 2 * DA_HEAD_DIM
    nq = s // tq
    vec = lambda: pl.BlockSpec((1, DA_HEAD_DIM), lambda bi, h, qi: (0, 0))
    return pl.pallas_call(
        functools.partial(_da_kernel, tq=tq, nq=nq, tpg=tpg),
        grid=(b, DA_HEADS, nq // tpg),
        in_specs=[vec(), vec(), vec(), vec(),
                  pl.BlockSpec((1, hw), lambda bi, h, qi: (0, 0)),
                  pl.BlockSpec((1, tpg * tq, hw), lambda bi, h, qi: (bi, qi, h)),
                  pl.BlockSpec((1, s, hw), lambda bi, h, qi: (bi, 0, DA_HEADS + h)),
                  pl.BlockSpec((1, s, hw), lambda bi, h, qi: (bi, 0, h))],
        out_specs=pl.BlockSpec((1, tpg * tq, hw), lambda bi, h, qi: (bi, qi, h)),
        out_shape=jax.ShapeDtypeStruct((b, s, DA_WIDTH), BF16),
        compiler_params=_cparams(3),
        name="diff_attention",
    )(lq1.reshape(1, -1), lk1.reshape(1, -1), lq2.reshape(1, -1), lk2.reshape(1, -1),
      g_subln.reshape(1, hw), qk3, qk3, vx3)


def _proj_kernel(a_ref, w_ref, o_ref, wbf_ref):
    @pl.when(pl.program_id(1) == 0)
    def _():
        wbf_ref[...] = w_ref[...].astype(BF16)

    o_ref[...] = jnp.dot(a_ref[...], wbf_ref[...], preferred_element_type=F32).astype(o_ref.dtype)


def _proj(a, w, tm=1024, tn=1024):
    m, k = a.shape
    n = w.shape[1]
    return pl.pallas_call(
        _proj_kernel,
        grid=(n // tn, m // tm),
        in_specs=[pl.BlockSpec((tm, k), lambda j, i: (i, 0)),
                  pl.BlockSpec((k, tn), lambda j, i: (0, j))],
        out_specs=pl.BlockSpec((tm, tn), lambda j, i: (i, j)),
        out_shape=jax.ShapeDtypeStruct((m, n), BF16),
        scratch_shapes=[pltpu.VMEM((k, tn), BF16)],
        compiler_params=_cparams(2),
        name="proj",
    )(a, w)


def _xa_kernel(q_ref, k_ref, v_ref, o_ref):
    nt = (((1,), (1,)), ((), ()))
    s = lax.dot_general(q_ref[0], k_ref[0], nt, preferred_element_type=F32)
    m = jnp.max(s, axis=-1, keepdims=True)
    p = jnp.exp2(s - m)
    l = jnp.sum(p, axis=-1, keepdims=True)
    o = jnp.dot(p.astype(BF16), v_ref[0], preferred_element_type=F32)
    o_ref[0] = (o * (1.0 / l)).astype(o_ref.dtype)


def _cross_attention(vx3, kv3):
    b, s, _ = vx3.shape
    mem_len = kv3.shape[1]
    hd = XA_HEAD_DIM
    return pl.pallas_call(
        _xa_kernel,
        grid=(b, XA_HEADS),
        in_specs=[pl.BlockSpec((1, s, hd), lambda bi, h: (bi, 0, XA_HEADS + h)),
                  pl.BlockSpec((1, mem_len, hd), lambda bi, h: (bi, 0, h)),
                  pl.BlockSpec((1, mem_len, hd), lambda bi, h: (bi, 0, XA_HEADS + h))],
        out_specs=pl.BlockSpec((1, s, hd), lambda bi, h: (bi, 0, h)),
        out_shape=jax.ShapeDtypeStruct((b, s, XA_WIDTH), BF16),
        compiler_params=_cparams(2),
        name="cross_attention",
    )(vx3, kv3, kv3)


def _merge_kernel(oc_ref, od_ref, ox_ref, g0_ref, g1_ref, g2_ref, wc_ref, wd_ref, wx_ref, o_ref, wbf_ref):
    @pl.when(pl.program_id(1) == 0)
    def _():
        wbf_ref[0] = wc_ref[...].astype(BF16)
        wbf_ref[1] = wd_ref[...].astype(BF16)
        wbf_ref[2] = wx_ref[...].astype(BF16)

    y = g0_ref[...].astype(F32) * jnp.dot(oc_ref[...], wbf_ref[0], preferred_element_type=F32)
    y = y + g1_ref[...].astype(F32) * jnp.dot(od_ref[...], wbf_ref[1], preferred_element_type=F32)
    y = y + g2_ref[...].astype(F32) * jnp.dot(ox_ref[...], wbf_ref[2], preferred_element_type=F32)
    o_ref[...] = y.astype(o_ref.dtype)


def _merge(o_conv, o_da, o_xa, gates, w_conv_out, w_da_out, w_xa_out, tm=1024, tn=512):
    t, kk = o_conv.shape
    gstride = D_MODEL // tn
    act = lambda: pl.BlockSpec((tm, kk), lambda j, i: (i, 0))
    wsp = lambda: pl.BlockSpec((kk, tn), lambda j, i: (0, j))
    gate = lambda br: pl.BlockSpec((tm, tn), lambda j, i: (i, br * gstride + j))
    return pl.pallas_call(
        _merge_kernel,
        grid=(D_MODEL // tn, t // tm),
        in_specs=[act(), act(), act(), gate(0), gate(1), gate(2), wsp(), wsp(), wsp()],
        out_specs=pl.BlockSpec((tm, tn), lambda j, i: (i, j)),
        out_shape=jax.ShapeDtypeStruct((t, D_MODEL), BF16),
        scratch_shapes=[pltpu.VMEM((3, kk, tn), BF16)],
        compiler_params=_cparams(2),
        name="merge",
    )(o_conv, o_da, o_xa, gates, gates, gates, w_conv_out, w_da_out, w_xa_out)


def _mix_kernel(a_ref, w_ref, x_ref, o_ref, wbf_ref):
    @pl.when(pl.program_id(1) == 0)
    def _():
        wbf_ref[...] = w_ref[...].astype(BF16)

    o_ref[...] = x_ref[...] + jnp.dot(a_ref[...], wbf_ref[...], preferred_element_type=F32)


def _mix(merged, w_mix, x2d, tm=1024, tn=1024):
    t, k = merged.shape
    n = w_mix.shape[1]
    return pl.pallas_call(
        _mix_kernel,
        grid=(n // tn, t // tm),
        in_specs=[pl.BlockSpec((tm, k), lambda j, i: (i, 0)),
                  pl.BlockSpec((k, tn), lambda j, i: (0, j)),
                  pl.BlockSpec((tm, tn), lambda j, i: (i, j))],
        out_specs=pl.BlockSpec((tm, tn), lambda j, i: (i, j)),
        out_shape=jax.ShapeDtypeStruct((t, n), F32),
        scratch_shapes=[pltpu.VMEM((k, tn), BF16)],
        compiler_params=_cparams(2),
        name="mix_out",
    )(merged, w_mix, x2d)


def _mlp_kernel(x_ref, gm_ref, wu_ref, wd_ref, gf_ref, o_ref, h_ref, *, nf):
    f = pl.program_id(1)

    @pl.when(f == 0)
    def _():
        h_ref[...] = _rms_scale(x_ref[...], gm_ref[...]).astype(BF16)
        o_ref[...] = jnp.zeros_like(o_ref)

    hm = jnp.dot(h_ref[...], wu_ref[...].astype(BF16), preferred_element_type=F32)
    hm = jnp.square(jnp.maximum(hm, 0.0)).astype(BF16)
    o_ref[...] += jnp.dot(hm, wd_ref[...].astype(BF16), preferred_element_type=F32)

    @pl.when(f == nf - 1)
    def _():
        o_ref[...] = _rms_scale(x_ref[...] + o_ref[...], gf_ref[...])


def _mlp(x1, g_mlp, w_up, w_down, g_final, tm=1024, tf=512):
    t, d = x1.shape
    ff = w_up.shape[1]
    nf = ff // tf
    return pl.pallas_call(
        functools.partial(_mlp_kernel, nf=nf),
        grid=(t // tm, nf),
        in_specs=[pl.BlockSpec((tm, d), lambda i, f: (i, 0), pipeline_mode=pl.Buffered(1)),
                  pl.BlockSpec((1, d), lambda i, f: (0, 0)),
                  pl.BlockSpec((d, tf), lambda i, f: (0, f)),
                  pl.BlockSpec((tf, d), lambda i, f: (f, 0)),
                  pl.BlockSpec((1, d), lambda i, f: (0, 0))],
        out_specs=pl.BlockSpec((tm, d), lambda i, f: (i, 0)),
        out_shape=jax.ShapeDtypeStruct((t, d), F32),
        scratch_shapes=[pltpu.VMEM((tm, d), BF16)],
        compiler_params=_cparams(2),
        name="mlp",
    )(x1, g_mlp.reshape(1, d), w_up, w_down, g_final.reshape(1, d))


def kernel(x, mem, positions, g_mix, w_in, w_dw, b_dw, g_conv_ln, b_conv_ln, w_conv_out, lambda_q1, lambda_k1, lambda_q2, lambda_k2, g_subln, w_da_out, g_mem, w_mem_kv, w_xa_out, w_mix_out, g_mlp, w_up, w_down, g_final):
    bsz, seq, d = x.shape
    t = bsz * seq
    x2d = x.reshape(t, d)
    w_in0 = w_in[0]

    h = _rmsnorm(x2d, g_mix[0], BF16)
    cos2, sin2 = _rope_tables(positions.astype(F32).reshape(t, 1))

    u = _glu_proj(h, w_in0)
    table_spec = pl.BlockSpec((2048, DA_HEAD_DIM), lambda j, i: (i, 0))
    qk = _in_proj(functools.partial(_qk_kernel, tn=1024), "qk_proj", h, w_in0, QK_COL0, 2048,
                  extra=(cos2, sin2), extra_specs=(table_spec, table_spec), tm=2048)
    vx = _in_proj(_vx_kernel, "vx_proj", h, w_in0, VX_COL0, 2048, tm=2048)
    gates = _in_proj(_gate_kernel, "gate_proj", h, w_in0, GATE_COL0, 3 * D_MODEL, tm=2048)
    qk3 = qk.reshape(bsz, seq, 2048)
    vx3 = vx.reshape(bsz, seq, 2048)

    o_conv = _conv_branch(u.reshape(bsz, seq, CONV_CH), w_dw[0], b_dw[0], g_conv_ln[0], b_conv_ln[0])
    o_da = _diff_attention(qk3, vx3, lambda_q1[0], lambda_k1[0], lambda_q2[0], lambda_k2[0], g_subln[0])

    mem_n = _rmsnorm(mem.reshape(-1, d), g_mem[0], BF16)
    kv = _proj(mem_n, w_mem_kv[0])
    o_xa = _cross_attention(vx3, kv.reshape(bsz, -1, 2 * XA_WIDTH))

    merged = _merge(o_conv.reshape(t, CONV_CH), o_da.reshape(t, DA_WIDTH), o_xa.reshape(t, XA_WIDTH),
                    gates, w_conv_out[0], w_da_out[0], w_xa_out[0])
    x1 = _mix(merged, w_mix_out[0], x2d)
    out = _mlp(x1, g_mlp[0], w_up[0], w_down[0], g_final)
    return out.reshape(bsz, seq, d)
```

```python
import functools
import math

import jax
import jax.numpy as jnp
from jax import lax
from jax.experimental import pallas as pl
from jax.experimental.pallas import tpu as pltpu

F32 = jnp.float32
BF16 = jnp.bfloat16

D_MODEL = 2048
CONV_CH = 1024
CONV_WIDTH = 31
DA_HEADS = 4
DA_HEAD_DIM = 128
DA_WIDTH = 1024
XA_HEADS = 4
XA_HEAD_DIM = 256
XA_WIDTH = 1024
D_FF = 4 * D_MODEL
ROPE_THETA = 10000.0
NORM_EPS = 1e-6
LAMBDA_INIT = 0.8 - 0.6 * math.exp(-0.3 * 0)
LOG2E = math.log2(math.e)

QK_COL0 = 2 * CONV_CH
VX_COL0 = QK_COL0 + 2048
GATE_COL0 = VX_COL0 + 2048

V7X_VMEM_LIMIT = 60 * 1024 * 1024
CONV_HALO = 32
SUBLANES = 8
LANES = 128


def _cparams(n_axes):
    return pltpu.CompilerParams(
        dimension_semantics=("arbitrary",) * n_axes,
        vmem_limit_bytes=V7X_VMEM_LIMIT,
    )


def _sigmoid(x):
    return 0.5 * jnp.tanh(0.5 * x) + 0.5


def _rms_scale(x, g):
    ms = jnp.mean(x * x, axis=-1, keepdims=True)
    return x * lax.rsqrt(ms + NORM_EPS) * g


def _rmsnorm_kernel(x_ref, g_ref, o_ref):
    o_ref[...] = _rms_scale(x_ref[...], g_ref[...]).astype(o_ref.dtype)


def _rmsnorm(x2d, g, out_dtype, tm=512):
    m, d = x2d.shape
    return pl.pallas_call(
        _rmsnorm_kernel,
        grid=(m // tm,),
        in_specs=[pl.BlockSpec((tm, d), lambda i: (i, 0)),
                  pl.BlockSpec((1, d), lambda i: (0, 0))],
        out_specs=pl.BlockSpec((tm, d), lambda i: (i, 0)),
        out_shape=jax.ShapeDtypeStruct((m, d), out_dtype),
        compiler_params=_cparams(1),
        name="rmsnorm",
    )(x2d, g.reshape(1, d))


def _glu_kernel(h_ref, wa_ref, wb_ref, o_ref, wbf_ref, *, tn):
    @pl.when(pl.program_id(1) == 0)
    def _():
        wbf_ref[:, :tn] = wa_ref[...].astype(BF16)
        wbf_ref[:, tn:] = wb_ref[...].astype(BF16)

    acc = jnp.dot(h_ref[...], wbf_ref[...], preferred_element_type=F32)
    o_ref[...] = (acc[:, :tn] * _sigmoid(acc[:, tn:])).astype(o_ref.dtype)


def _glu_proj(h, w_in, tm=2048, tn=512):
    t, d = h.shape
    nb = CONV_CH // tn
    return pl.pallas_call(
        functools.partial(_glu_kernel, tn=tn),
        grid=(nb, t // tm),
        in_specs=[pl.BlockSpec((tm, d), lambda j, i: (i, 0)),
                  pl.BlockSpec((d, tn), lambda j, i: (0, j)),
                  pl.BlockSpec((d, tn), lambda j, i: (0, nb + j))],
        out_specs=pl.BlockSpec((tm, tn), lambda j, i: (i, j)),
        out_shape=jax.ShapeDtypeStruct((t, CONV_CH), BF16),
        scratch_shapes=[pltpu.VMEM((d, 2 * tn), BF16)],
        compiler_params=_cparams(2),
        name="glu_proj",
    )(h, w_in, w_in)


def _rope_kernel(pos_ref, f_ref, cos_ref, sin_ref):
    ang = pos_ref[...] * f_ref[...]
    lane = lax.broadcasted_iota(jnp.int32, ang.shape, 1)
    sign = jnp.where(lane < DA_HEAD_DIM // 2, -1.0, 1.0)
    cos_ref[...] = jnp.cos(ang)
    sin_ref[...] = jnp.sin(ang) * sign


def _rope_tables(pos_col, tm=1024):
    t = pos_col.shape[0]
    inv_freq = 1.0 / (ROPE_THETA ** (jnp.arange(0, DA_HEAD_DIM, 2, dtype=F32) / DA_HEAD_DIM))
    f2 = jnp.concatenate([inv_freq, inv_freq]).reshape(1, DA_HEAD_DIM)
    return pl.pallas_call(
        _rope_kernel,
        grid=(t // tm,),
        in_specs=[pl.BlockSpec((tm, 1), lambda i: (i, 0)),
                  pl.BlockSpec((1, DA_HEAD_DIM), lambda i: (0, 0))],
        out_specs=[pl.BlockSpec((tm, DA_HEAD_DIM), lambda i: (i, 0))] * 2,
        out_shape=[jax.ShapeDtypeStruct((t, DA_HEAD_DIM), F32)] * 2,
        compiler_params=_cparams(1),
        name="rope_tables",
    )(pos_col, f2)


def _qk_kernel(h_ref, w_ref, cos_ref, sin_ref, o_ref, wbf_ref, *, tn):
    @pl.when(pl.program_id(1) == 0)
    def _():
        wbf_ref[...] = w_ref[...].astype(BF16)

    acc = jnp.dot(h_ref[...], wbf_ref[...], preferred_element_type=F32)
    scale = jnp.where(pl.program_id(0) == 0, DA_HEAD_DIM ** -0.5 * LOG2E, 1.0).astype(F32)
    c = cos_ref[...] * scale
    s = sin_ref[...] * scale
    for g in range(tn // DA_HEAD_DIM):
        sl = slice(g * DA_HEAD_DIM, (g + 1) * DA_HEAD_DIM)
        t = acc[:, sl]
        r = pltpu.roll(t, DA_HEAD_DIM // 2, axis=1)
        o_ref[:, sl] = (t * c + r * s).astype(o_ref.dtype)


def _vx_kernel(h_ref, w_ref, o_ref, wbf_ref):
    @pl.when(pl.program_id(1) == 0)
    def _():
        wbf_ref[...] = w_ref[...].astype(BF16)

    acc = jnp.dot(h_ref[...], wbf_ref[...], preferred_element_type=F32)
    scale = jnp.where(pl.program_id(0) == 1, XA_HEAD_DIM ** -0.5 * LOG2E, 1.0).astype(F32)
    o_ref[...] = (acc * scale).astype(o_ref.dtype)


def _gate_kernel(h_ref, w_ref, o_ref, wbf_ref):
    @pl.when(pl.program_id(1) == 0)
    def _():
        wbf_ref[...] = w_ref[...].astype(BF16)

    acc = jnp.dot(h_ref[...], wbf_ref[...], preferred_element_type=F32)
    o_ref[...] = _sigmoid(acc).astype(o_ref.dtype)


def _in_proj(body, name, h, w_in, col0, ncols, extra=(), extra_specs=(), tm=1024, tn=1024):
    t, d = h.shape
    cb = col0 // tn
    extra_specs = list(extra_specs)
    return pl.pallas_call(
        body,
        grid=(ncols // tn, t // tm),
        in_specs=[pl.BlockSpec((tm, d), lambda j, i: (i, 0)),
                  pl.BlockSpec((d, tn), lambda j, i: (0, cb + j))] + extra_specs,
        out_specs=pl.BlockSpec((tm, tn), lambda j, i: (i, j)),
        out_shape=jax.ShapeDtypeStruct((t, ncols), BF16),
        scratch_shapes=[pltpu.VMEM((d, tn), BF16)],
        compiler_params=_cparams(2),
        name=name,
    )(h, w_in, *extra)


def _conv_kernel(ucur_ref, uprev_ref, w_ref, bdw_ref, g_ref, b_ref, o_ref, buf_ref, sh_ref, y_ref, *, ts, rg, rc):
    si = pl.program_id(1)
    prev = uprev_ref[0].astype(F32)
    buf_ref[0:CONV_HALO, :] = jnp.where(si > 0, prev, 0.0)
    buf_ref[CONV_HALO:CONV_HALO + ts, :] = ucur_ref[0].astype(F32)
    buf_ref[CONV_HALO + ts:, :] = jnp.zeros((SUBLANES, CONV_CH), F32)
    n_sh = ts + CONV_HALO
    for r in range(1, SUBLANES):
        sh_ref[r - 1] = buf_ref[pl.ds(r, n_sh), :]

    lead = CONV_HALO - (CONV_WIDTH - 1)
    rows_per_iter = rg * SUBLANES

    for c in range(CONV_CH // LANES):
        lanes = slice(c * LANES, (c + 1) * LANES)
        taps = [jnp.broadcast_to(w_ref[k:k + 1, lanes], (SUBLANES, LANES)) for k in range(CONV_WIDTH)]
        bias = jnp.broadcast_to(bdw_ref[:, lanes], (SUBLANES, LANES))

        def rows_body(it, carry, lanes=lanes, taps=taps, bias=bias):
            r0 = pl.multiple_of(it * rows_per_iter, rows_per_iter)
            for gi in range(rg):
                acc = bias
                for k in range(CONV_WIDTH):
                    a, r = divmod(lead + k, SUBLANES)
                    row0 = pl.multiple_of(r0 + SUBLANES * (a + gi), SUBLANES)
                    if r == 0:
                        rows = buf_ref[pl.ds(row0, SUBLANES), lanes]
                    else:
                        rows = sh_ref[r - 1, pl.ds(row0, SUBLANES), lanes]
                    acc = acc + rows * taps[k]
                y_ref[pl.ds(pl.multiple_of(r0 + SUBLANES * gi, SUBLANES), SUBLANES), lanes] = acc
            return carry

        lax.fori_loop(0, ts // rows_per_iter, rows_body, 0)

    for ci in range(ts // rc):
        rows = slice(ci * rc, (ci + 1) * rc)
        acc = y_ref[rows, :]
        mu = jnp.mean(acc, axis=-1, keepdims=True)
        cen = acc - mu
        var = jnp.mean(cen * cen, axis=-1, keepdims=True)
        y = cen * lax.rsqrt(var + NORM_EPS) * g_ref[...] + b_ref[...]
        o_ref[0, rows, :] = (y * _sigmoid(y)).astype(o_ref.dtype)


def _conv_branch(u3, w_dw, b_dw, g_ln, b_ln, ts=512, rg=16, rc=128):
    b, s, c = u3.shape
    hb = ts // CONV_HALO
    return pl.pallas_call(
        functools.partial(_conv_kernel, ts=ts, rg=rg, rc=rc),
        grid=(b, s // ts),
        in_specs=[pl.BlockSpec((1, ts, c), lambda bi, si: (bi, si, 0)),
                  pl.BlockSpec((1, CONV_HALO, c), lambda bi, si: (bi, jnp.maximum(si * hb - 1, 0), 0)),
                  pl.BlockSpec((CONV_WIDTH, c), lambda bi, si: (0, 0)),
                  pl.BlockSpec((1, c), lambda bi, si: (0, 0)),
                  pl.BlockSpec((1, c), lambda bi, si: (0, 0)),
                  pl.BlockSpec((1, c), lambda bi, si: (0, 0))],
        out_specs=pl.BlockSpec((1, ts, c), lambda bi, si: (bi, si, 0)),
        out_shape=jax.ShapeDtypeStruct((b, s, c), BF16),
        scratch_shapes=[pltpu.VMEM((CONV_HALO + ts + SUBLANES, c), F32),
                        pltpu.VMEM((SUBLANES - 1, CONV_HALO + ts, c), F32),
                        pltpu.VMEM((ts, c), F32)],
        compiler_params=_cparams(2),
        name="conv_branch",
    )(u3, u3, w_dw, b_dw.reshape(1, c), g_ln.reshape(1, c), b_ln.reshape(1, c))


def _da_kernel(lq1_ref, lk1_ref, lq2_ref, lk2_ref, g_ref, q_ref, k_ref, v_ref, o_ref, *, tq, nq, tpg):
    qi = pl.program_id(2)
    hd = DA_HEAD_DIM
    nt = (((1,), (1,)), ((), ()))
    lam = (jnp.exp(jnp.sum(lq1_ref[...] * lk1_ref[...], axis=-1, keepdims=True))
           - jnp.exp(jnp.sum(lq2_ref[...] * lk2_ref[...], axis=-1, keepdims=True))
           + LAMBDA_INIT)

    def tile(n, part):
        off = n * tq
        rows = slice(part * tq, (part + 1) * tq)
        q = q_ref[0, rows, :]
        causal = (lax.broadcasted_iota(jnp.int32, (tq, tq), 0)
                  >= lax.broadcasted_iota(jnp.int32, (tq, tq), 1))
        p_off, p_dg, inv_l = [], [], []
        for mp in range(2):
            cs = slice(mp * hd, (mp + 1) * hd)
            sd = lax.dot_general(q[:, cs], k_ref[0, off:off + tq, cs], nt, preferred_element_type=F32)
            sd = jnp.where(causal, sd, -jnp.inf)
            m = jnp.max(sd, axis=-1, keepdims=True)
            if n > 0:
                so = lax.dot_general(q[:, cs], k_ref[0, 0:off, cs], nt, preferred_element_type=F32)
                m = jnp.maximum(m, jnp.max(so, axis=-1, keepdims=True))
                po = jnp.exp2(so - m)
                l = jnp.sum(po, axis=-1, keepdims=True)
                p_off.append(po.astype(BF16))
            else:
                l = 0.0
            pd = jnp.exp2(sd - m)
            l = l + jnp.sum(pd, axis=-1, keepdims=True)
            p_dg.append(pd.astype(BF16))
            inv_l.append(1.0 / l)
        acc = jnp.dot(jnp.concatenate(p_dg, axis=0), v_ref[0, off:off + tq, :], preferred_element_type=F32)
        if n > 0:
            acc = acc + jnp.dot(jnp.concatenate(p_off, axis=0), v_ref[0, 0:off, :],
                                preferred_element_type=F32)
        dlt = acc[:tq] * inv_l[0] - lam * (acc[tq:] * inv_l[1])
        y = _rms_scale(dlt, g_ref[...]) * (1.0 - LAMBDA_INIT)
        o_ref[0, rows, :] = y.astype(o_ref.dtype)

    def group(gq):
        for part in range(tpg):
            tile(gq * tpg + part, part)

    for gq in range(nq // tpg):
        pl.when(qi == gq)(functools.partial(group, gq))


def _diff_attention(qk3, vx3, lq1, lk1, lq2, lk2, g_subln, tq=256, tpg=8):
    b, s, _ = qk3.shape
    hw = 2 * DA_HEAD_DIM
    nq = s // tq
    vec = lambda: pl.BlockSpec((1, DA_HEAD_DIM), lambda bi, h, qi: (0, 0))
    return pl.pallas_call(
        functools.partial(_da_kernel, tq=tq, nq=nq, tpg=tpg),
        grid=(b, DA_HEADS, nq // tpg),
        in_specs=[vec(), vec(), vec(), vec(),
                  pl.BlockSpec((1, hw), lambda bi, h, qi: (0, 0)),
                  pl.BlockSpec((1, tpg * tq, hw), lambda bi, h, qi: (bi, qi, h)),
                  pl.BlockSpec((1, s, hw), lambda bi, h, qi: (bi, 0, DA_HEADS + h)),
                  pl.BlockSpec((1, s, hw), lambda bi, h, qi: (bi, 0, h))],
        out_specs=pl.BlockSpec((1, tpg * tq, hw), lambda bi, h, qi: (bi, qi, h)),
        out_shape=jax.ShapeDtypeStruct((b, s, DA_WIDTH), BF16),
        compiler_params=_cparams(3),
        name="diff_attention",
    )(lq1.reshape(1, -1), lk1.reshape(1, -1), lq2.reshape(1, -1), lk2.reshape(1, -1),
      g_subln.reshape(1, hw), qk3, qk3, vx3)


def _proj_kernel(a_ref, w_ref, o_ref, wbf_ref):
    @pl.when(pl.program_id(1) == 0)
    def _():
        wbf_ref[...] = w_ref[...].astype(BF16)

    o_ref[...] = jnp.dot(a_ref[...], wbf_ref[...], preferred_element_type=F32).astype(o_ref.dtype)


def _proj(a, w, tm=1024, tn=1024):
    m, k = a.shape
    n = w.shape[1]
    return pl.pallas_call(
        _proj_kernel,
        grid=(n // tn, m // tm),
        in_specs=[pl.BlockSpec((tm, k), lambda j, i: (i, 0)),
                  pl.BlockSpec((k, tn), lambda j, i: (0, j))],
        out_specs=pl.BlockSpec((tm, tn), lambda j, i: (i, j)),
        out_shape=jax.ShapeDtypeStruct((m, n), BF16),
        scratch_shapes=[pltpu.VMEM((k, tn), BF16)],
        compiler_params=_cparams(2),
        name="proj",
    )(a, w)


def _xa_kernel(q_ref, k_ref, v_ref, o_ref):
    nt = (((1,), (1,)), ((), ()))
    s = lax.dot_general(q_ref[0], k_ref[0], nt, preferred_element_type=F32)
    m = jnp.max(s, axis=-1, keepdims=True)
    p = jnp.exp2(s - m)
    l = jnp.sum(p, axis=-1, keepdims=True)
    o = jnp.dot(p.astype(BF16), v_ref[0], preferred_element_type=F32)
    o_ref[0] = (o * (1.0 / l)).astype(o_ref.dtype)


def _cross_attention(vx3, kv3):
    b, s, _ = vx3.shape
    mem_len = kv3.shape[1]
    hd = XA_HEAD_DIM
    return pl.pallas_call(
        _xa_kernel,
        grid=(b, XA_HEADS),
        in_specs=[pl.BlockSpec((1, s, hd), lambda bi, h: (bi, 0, XA_HEADS + h)),
                  pl.BlockSpec((1, mem_len, hd), lambda bi, h: (bi, 0, h)),
                  pl.BlockSpec((1, mem_len, hd), lambda bi, h: (bi, 0, XA_HEADS + h))],
        out_specs=pl.BlockSpec((1, s, hd), lambda bi, h: (bi, 0, h)),
        out_shape=jax.ShapeDtypeStruct((b, s, XA_WIDTH), BF16),
        compiler_params=_cparams(2),
        name="cross_attention",
    )(vx3, kv3, kv3)


def _merge_kernel(oc_ref, od_ref, ox_ref, g0_ref, g1_ref, g2_ref, wc_ref, wd_ref, wx_ref, o_ref, wbf_ref):
    @pl.when(pl.program_id(1) == 0)
    def _():
        wbf_ref[0] = wc_ref[...].astype(BF16)
        wbf_ref[1] = wd_ref[...].astype(BF16)
        wbf_ref[2] = wx_ref[...].astype(BF16)

    y = g0_ref[...].astype(F32) * jnp.dot(oc_ref[...], wbf_ref[0], preferred_element_type=F32)
    y = y + g1_ref[...].astype(F32) * jnp.dot(od_ref[...], wbf_ref[1], preferred_element_type=F32)
    y = y + g2_ref[...].astype(F32) * jnp.dot(ox_ref[...], wbf_ref[2], preferred_element_type=F32)
    o_ref[...] = y.astype(o_ref.dtype)


def _merge(o_conv, o_da, o_xa, gates, w_conv_out, w_da_out, w_xa_out, tm=512, tn=1024):
    t, kk = o_conv.shape
    gstride = D_MODEL // tn
    act = lambda: pl.BlockSpec((tm, kk), lambda j, i: (i, 0))
    wsp = lambda: pl.BlockSpec((kk, tn), lambda j, i: (0, j))
    gate = lambda br: pl.BlockSpec((tm, tn), lambda j, i: (i, br * gstride + j))
    return pl.pallas_call(
        _merge_kernel,
        grid=(D_MODEL // tn, t // tm),
        in_specs=[act(), act(), act(), gate(0), gate(1), gate(2), wsp(), wsp(), wsp()],
        out_specs=pl.BlockSpec((tm, tn), lambda j, i: (i, j)),
        out_shape=jax.ShapeDtypeStruct((t, D_MODEL), BF16),
        scratch_shapes=[pltpu.VMEM((3, kk, tn), BF16)],
        compiler_params=_cparams(2),
        name="merge",
    )(o_conv, o_da, o_xa, gates, gates, gates, w_conv_out, w_da_out, w_xa_out)


def _mix_kernel(a_ref, w_ref, x_ref, o_ref, wbf_ref):
    @pl.when(pl.program_id(1) == 0)
    def _():
        wbf_ref[...] = w_ref[...].astype(BF16)

    o_ref[...] = x_ref[...] + jnp.dot(a_ref[...], wbf_ref[...], preferred_element_type=F32)


def _mix(merged, w_mix, x2d, tm=1024, tn=1024):
    t, k = merged.shape
    n = w_mix.shape[1]
    return pl.pallas_call(
        _mix_kernel,
        grid=(n // tn, t // tm),
        in_specs=[pl.BlockSpec((tm, k), lambda j, i: (i, 0)),
                  pl.BlockSpec((k, tn), lambda j, i: (0, j)),
                  pl.BlockSpec((tm, tn), lambda j, i: (i, j))],
        out_specs=pl.BlockSpec((tm, tn), lambda j, i: (i, j)),
        out_shape=jax.ShapeDtypeStruct((t, n), F32),
        scratch_shapes=[pltpu.VMEM((k, tn), BF16)],
        compiler_params=_cparams(2),
        name="mix_out",
    )(merged, w_mix, x2d)


def _mlp_kernel(x_ref, gm_ref, wu_ref, wd_ref, gf_ref, o_ref, h_ref, *, nf):
    f = pl.program_id(1)

    @pl.when(f == 0)
    def _():
        h_ref[...] = _rms_scale(x_ref[...], gm_ref[...]).astype(BF16)
        o_ref[...] = jnp.zeros_like(o_ref)

    hm = jnp.dot(h_ref[...], wu_ref[...].astype(BF16), preferred_element_type=F32)
    hm = jnp.square(jnp.maximum(hm, 0.0)).astype(BF16)
    o_ref[...] += jnp.dot(hm, wd_ref[...].astype(BF16), preferred_element_type=F32)

    @pl.when(f == nf - 1)
    def _():
        o_ref[...] = _rms_scale(x_ref[...] + o_ref[...], gf_ref[...])


def _mlp(x1, g_mlp, w_up, w_down, g_final, tm=1024, tf=512):
    t, d = x1.shape
    ff = w_up.shape[1]
    nf = ff // tf
    return pl.pallas_call(
        functools.partial(_mlp_kernel, nf=nf),
        grid=(t // tm, nf),
        in_specs=[pl.BlockSpec((tm, d), lambda i, f: (i, 0), pipeline_mode=pl.Buffered(1)),
                  pl.BlockSpec((1, d), lambda i, f: (0, 0)),
                  pl.BlockSpec((d, tf), lambda i, f: (0, f)),
                  pl.BlockSpec((tf, d), lambda i, f: (f, 0)),
                  pl.BlockSpec((1, d), lambda i, f: (0, 0))],
        out_specs=pl.BlockSpec((tm, d), lambda i, f: (i, 0)),
        out_shape=jax.ShapeDtypeStruct((t, d), F32),
        scratch_shapes=[pltpu.VMEM((tm, d), BF16)],
        compiler_params=_cparams(2),
        name="mlp",
    )(x1, g_mlp.reshape(1, d), w_up, w_down, g_final.reshape(1, d))


def kernel(x, mem, positions, g_mix, w_in, w_dw, b_dw, g_conv_ln, b_conv_ln, w_conv_out, lambda_q1, lambda_k1, lambda_q2, lambda_k2, g_subln, w_da_out, g_mem, w_mem_kv, w_xa_out, w_mix_out, g_mlp, w_up, w_down, g_final):
    bsz, seq, d = x.shape
    t = bsz * seq
    x2d = x.reshape(t, d)
    w_in0 = w_in[0]

    h = _rmsnorm(x2d, g_mix[0], BF16)
    cos2, sin2 = _rope_tables(positions.astype(F32).reshape(t, 1))

    u = _glu_proj(h, w_in0)
    table_spec = pl.BlockSpec((2048, DA_HEAD_DIM), lambda j, i: (i, 0))
    qk = _in_proj(functools.partial(_qk_kernel, tn=1024), "qk_proj", h, w_in0, QK_COL0, 2048,
                  extra=(cos2, sin2), extra_specs=(table_spec, table_spec), tm=2048)
    vx = _in_proj(_vx_kernel, "vx_proj", h, w_in0, VX_COL0, 2048, tm=2048)
    gates = _in_proj(_gate_kernel, "gate_proj", h, w_in0, GATE_COL0, 3 * D_MODEL, tm=2048)
    qk3 = qk.reshape(bsz, seq, 2048)
    vx3 = vx.reshape(bsz, seq, 2048)

    o_conv = _conv_branch(u.reshape(bsz, seq, CONV_CH), w_dw[0], b_dw[0], g_conv_ln[0], b_conv_ln[0])
    o_da = _diff_attention(qk3, vx3, lambda_q1[0], lambda_k1[0], lambda_q2[0], lambda_k2[0], g_subln[0])

    mem_n = _rmsnorm(mem.reshape(-1, d), g_mem[0], BF16)
    kv = _proj(mem_n, w_mem_kv[0])
    o_xa = _cross_attention(vx3, kv.reshape(bsz, -1, 2 * XA_WIDTH))

    merged = _merge(o_conv.reshape(t, CONV_CH), o_da.reshape(t, DA_WIDTH), o_xa.reshape(t, XA_WIDTH),
                    gates, w_conv_out[0], w_da_out[0], w_xa_out[0])
    x1 = _mix(merged, w_mix_out[0], x2d)
    out = _mlp(x1, g_mlp[0], w_up[0], w_down[0], g_final)
    return out.reshape(bsz, seq, d)
```

```python
import functools
import math

import jax
import jax.numpy as jnp
from jax import lax
from jax.experimental import pallas as pl
from jax.experimental.pallas import tpu as pltpu

F32 = jnp.float32
BF16 = jnp.bfloat16

D_MODEL = 2048
CONV_CH = 1024
CONV_WIDTH = 31
DA_HEADS = 4
DA_HEAD_DIM = 128
DA_WIDTH = 1024
XA_HEADS = 4
XA_HEAD_DIM = 256
XA_WIDTH = 1024
D_FF = 4 * D_MODEL
ROPE_THETA = 10000.0
NORM_EPS = 1e-6
LAMBDA_INIT = 0.8 - 0.6 * math.exp(-0.3 * 0)
LOG2E = math.log2(math.e)

QK_COL0 = 2 * CONV_CH
VX_COL0 = QK_COL0 + 2048
GATE_COL0 = VX_COL0 + 2048

V7X_VMEM_LIMIT = 60 * 1024 * 1024
CONV_HALO = 32
SUBLANES = 8
LANES = 128


def _cparams(n_axes):
    return pltpu.CompilerParams(
        dimension_semantics=("arbitrary",) * n_axes,
        vmem_limit_bytes=V7X_VMEM_LIMIT,
    )


def _sigmoid(x):
    return 0.5 * jnp.tanh(0.5 * x) + 0.5


def _rms_scale(x, g):
    ms = jnp.mean(x * x, axis=-1, keepdims=True)
    return x * lax.rsqrt(ms + NORM_EPS) * g


def _rmsnorm_kernel(x_ref, g_ref, o_ref):
    o_ref[...] = _rms_scale(x_ref[...], g_ref[...]).astype(o_ref.dtype)


def _rmsnorm(x2d, g, out_dtype, tm=512):
    m, d = x2d.shape
    return pl.pallas_call(
        _rmsnorm_kernel,
        grid=(m // tm,),
        in_specs=[pl.BlockSpec((tm, d), lambda i: (i, 0)),
                  pl.BlockSpec((1, d), lambda i: (0, 0))],
        out_specs=pl.BlockSpec((tm, d), lambda i: (i, 0)),
        out_shape=jax.ShapeDtypeStruct((m, d), out_dtype),
        compiler_params=_cparams(1),
        name="rmsnorm",
    )(x2d, g.reshape(1, d))


def _glu_kernel(h_ref, wa_ref, wb_ref, o_ref, wbf_ref, *, tn):
    @pl.when(pl.program_id(1) == 0)
    def _():
        wbf_ref[:, :tn] = wa_ref[...].astype(BF16)
        wbf_ref[:, tn:] = wb_ref[...].astype(BF16)

    acc = jnp.dot(h_ref[...], wbf_ref[...], preferred_element_type=F32)
    o_ref[...] = (acc[:, :tn] * _sigmoid(acc[:, tn:])).astype(o_ref.dtype)


def _glu_proj(h, w_in, tm=2048, tn=512):
    t, d = h.shape
    nb = CONV_CH // tn
    return pl.pallas_call(
        functools.partial(_glu_kernel, tn=tn),
        grid=(nb, t // tm),
        in_specs=[pl.BlockSpec((tm, d), lambda j, i: (i, 0)),
                  pl.BlockSpec((d, tn), lambda j, i: (0, j)),
                  pl.BlockSpec((d, tn), lambda j, i: (0, nb + j))],
        out_specs=pl.BlockSpec((tm, tn), lambda j, i: (i, j)),
        out_shape=jax.ShapeDtypeStruct((t, CONV_CH), BF16),
        scratch_shapes=[pltpu.VMEM((d, 2 * tn), BF16)],
        compiler_params=_cparams(2),
        name="glu_proj",
    )(h, w_in, w_in)


def _rope_kernel(pos_ref, f_ref, cos_ref, sin_ref):
    ang = pos_ref[...] * f_ref[...]
    lane = lax.broadcasted_iota(jnp.int32, ang.shape, 1)
    sign = jnp.where(lane < DA_HEAD_DIM // 2, -1.0, 1.0)
    cos_ref[...] = jnp.cos(ang)
    sin_ref[...] = jnp.sin(ang) * sign


def _rope_tables(pos_col, tm=1024):
    t = pos_col.shape[0]
    inv_freq = 1.0 / (ROPE_THETA ** (jnp.arange(0, DA_HEAD_DIM, 2, dtype=F32) / DA_HEAD_DIM))
    f2 = jnp.concatenate([inv_freq, inv_freq]).reshape(1, DA_HEAD_DIM)
    return pl.pallas_call(
        _rope_kernel,
        grid=(t // tm,),
        in_specs=[pl.BlockSpec((tm, 1), lambda i: (i, 0)),
                  pl.BlockSpec((1, DA_HEAD_DIM), lambda i: (0, 0))],
        out_specs=[pl.BlockSpec((tm, DA_HEAD_DIM), lambda i: (i, 0))] * 2,
        out_shape=[jax.ShapeDtypeStruct((t, DA_HEAD_DIM), F32)] * 2,
        compiler_params=_cparams(1),
        name="rope_tables",
    )(pos_col, f2)


def _qk_kernel(h_ref, w_ref, cos_ref, sin_ref, o_ref, wbf_ref, *, tn):
    @pl.when(pl.program_id(1) == 0)
    def _():
        wbf_ref[...] = w_ref[...].astype(BF16)

    acc = jnp.dot(h_ref[...], wbf_ref[...], preferred_element_type=F32)
    scale = jnp.where(pl.program_id(0) == 0, DA_HEAD_DIM ** -0.5 * LOG2E, 1.0).astype(F32)
    c = cos_ref[...] * scale
    s = sin_ref[...] * scale
    for g in range(tn // DA_HEAD_DIM):
        sl = slice(g * DA_HEAD_DIM, (g + 1) * DA_HEAD_DIM)
        t = acc[:, sl]
        r = pltpu.roll(t, DA_HEAD_DIM // 2, axis=1)
        o_ref[:, sl] = (t * c + r * s).astype(o_ref.dtype)


def _vx_kernel(h_ref, w_ref, o_ref, wbf_ref):
    @pl.when(pl.program_id(1) == 0)
    def _():
        wbf_ref[...] = w_ref[...].astype(BF16)

    acc = jnp.dot(h_ref[...], wbf_ref[...], preferred_element_type=F32)
    scale = jnp.where(pl.program_id(0) == 1, XA_HEAD_DIM ** -0.5 * LOG2E, 1.0).astype(F32)
    o_ref[...] = (acc * scale).astype(o_ref.dtype)


def _gate_kernel(h_ref, w_ref, o_ref, wbf_ref):
    @pl.when(pl.program_id(1) == 0)
    def _():
        wbf_ref[...] = w_ref[...].astype(BF16)

    acc = jnp.dot(h_ref[...], wbf_ref[...], preferred_element_type=F32)
    o_ref[...] = _sigmoid(acc).astype(o_ref.dtype)


def _in_proj(body, name, h, w_in, col0, ncols, extra=(), extra_specs=(), tm=1024, tn=1024):
    t, d = h.shape
    cb = col0 // tn
    extra_specs = list(extra_specs)
    return pl.pallas_call(
        body,
        grid=(ncols // tn, t // tm),
        in_specs=[pl.BlockSpec((tm, d), lambda j, i: (i, 0)),
                  pl.BlockSpec((d, tn), lambda j, i: (0, cb + j))] + extra_specs,
        out_specs=pl.BlockSpec((tm, tn), lambda j, i: (i, j)),
        out_shape=jax.ShapeDtypeStruct((t, ncols), BF16),
        scratch_shapes=[pltpu.VMEM((d, tn), BF16)],
        compiler_params=_cparams(2),
        name=name,
    )(h, w_in, *extra)


def _conv_kernel(ucur_ref, uprev_ref, w_ref, bdw_ref, g_ref, b_ref, o_ref, buf_ref, sh_ref, y_ref, *, ts, rg, rc):
    si = pl.program_id(1)
    prev = uprev_ref[0].astype(F32)
    buf_ref[0:CONV_HALO, :] = jnp.where(si > 0, prev, 0.0)
    buf_ref[CONV_HALO:CONV_HALO + ts, :] = ucur_ref[0].astype(F32)
    buf_ref[CONV_HALO + ts:, :] = jnp.zeros((SUBLANES, CONV_CH), F32)
    n_sh = ts + CONV_HALO
    for r in range(1, SUBLANES):
        sh_ref[r - 1] = buf_ref[pl.ds(r, n_sh), :]

    lead = CONV_HALO - (CONV_WIDTH - 1)
    rows_per_iter = rg * SUBLANES

    for c in range(CONV_CH // LANES):
        lanes = slice(c * LANES, (c + 1) * LANES)
        taps = [jnp.broadcast_to(w_ref[k:k + 1, lanes], (SUBLANES, LANES)) for k in range(CONV_WIDTH)]
        bias = jnp.broadcast_to(bdw_ref[:, lanes], (SUBLANES, LANES))

        def rows_body(it, carry, lanes=lanes, taps=taps, bias=bias):
            r0 = pl.multiple_of(it * rows_per_iter, rows_per_iter)
            for gi in range(rg):
                acc = bias
                for k in range(CONV_WIDTH):
                    a, r = divmod(lead + k, SUBLANES)
                    row0 = pl.multiple_of(r0 + SUBLANES * (a + gi), SUBLANES)
                    if r == 0:
                        rows = buf_ref[pl.ds(row0, SUBLANES), lanes]
                    else:
                        rows = sh_ref[r - 1, pl.ds(row0, SUBLANES), lanes]
                    acc = acc + rows * taps[k]
                y_ref[pl.ds(pl.multiple_of(r0 + SUBLANES * gi, SUBLANES), SUBLANES), lanes] = acc
            return carry

        lax.fori_loop(0, ts // rows_per_iter, rows_body, 0)

    for ci in range(ts // rc):
        rows = slice(ci * rc, (ci + 1) * rc)
        acc = y_ref[rows, :]
        mu = jnp.mean(acc, axis=-1, keepdims=True)
        cen = acc - mu
        var = jnp.mean(cen * cen, axis=-1, keepdims=True)
        y = cen * lax.rsqrt(var + NORM_EPS) * g_ref[...] + b_ref[...]
        o_ref[0, rows, :] = (y * _sigmoid(y)).astype(o_ref.dtype)


def _conv_branch(u3, w_dw, b_dw, g_ln, b_ln, ts=512, rg=16, rc=128):
    b, s, c = u3.shape
    hb = ts // CONV_HALO
    return pl.pallas_call(
        functools.partial(_conv_kernel, ts=ts, rg=rg, rc=rc),
        grid=(b, s // ts),
        in_specs=[pl.BlockSpec((1, ts, c), lambda bi, si: (bi, si, 0)),
                  pl.BlockSpec((1, CONV_HALO, c), lambda bi, si: (bi, jnp.maximum(si * hb - 1, 0), 0)),
                  pl.BlockSpec((CONV_WIDTH, c), lambda bi, si: (0, 0)),
                  pl.BlockSpec((1, c), lambda bi, si: (0, 0)),
                  pl.BlockSpec((1, c), lambda bi, si: (0, 0)),
                  pl.BlockSpec((1, c), lambda bi, si: (0, 0))],
        out_specs=pl.BlockSpec((1, ts, c), lambda bi, si: (bi, si, 0)),
        out_shape=jax.ShapeDtypeStruct((b, s, c), BF16),
        scratch_shapes=[pltpu.VMEM((CONV_HALO + ts + SUBLANES, c), F32),
                        pltpu.VMEM((SUBLANES - 1, CONV_HALO + ts, c), F32),
                        pltpu.VMEM((ts, c), F32)],
        compiler_params=_cparams(2),
        name="conv_branch",
    )(u3, u3, w_dw, b_dw.reshape(1, c), g_ln.reshape(1, c), b_ln.reshape(1, c))


def _da_kernel(lq1_ref, lk1_ref, lq2_ref, lk2_ref, g_ref, q_ref, k_ref, v_ref, o_ref, *, tq, nq, tpg):
    qi = pl.program_id(2)
    hd = DA_HEAD_DIM
    nt = (((1,), (1,)), ((), ()))
    lam = (jnp.exp(jnp.sum(lq1_ref[...] * lk1_ref[...], axis=-1, keepdims=True))
           - jnp.exp(jnp.sum(lq2_ref[...] * lk2_ref[...], axis=-1, keepdims=True))
           + LAMBDA_INIT)

    def tile(n, part):
        off = n * tq
        rows = slice(part * tq, (part + 1) * tq)
        q = q_ref[0, rows, :]
        causal = (lax.broadcasted_iota(jnp.int32, (tq, tq), 0)
                  >= lax.broadcasted_iota(jnp.int32, (tq, tq), 1))
        p_off, p_dg, inv_l = [], [], []
        for mp in range(2):
            cs = slice(mp * hd, (mp + 1) * hd)
            sd = lax.dot_general(q[:, cs], k_ref[0, off:off + tq, cs], nt, preferred_element_type=F32)
            sd = jnp.where(causal, sd, -jnp.inf)
            m = jnp.max(sd, axis=-1, keepdims=True)
            if n > 0:
                so = lax.dot_general(q[:, cs], k_ref[0, 0:off, cs], nt, preferred_element_type=F32)
                m = jnp.maximum(m, jnp.max(so, axis=-1, keepdims=True))
                po = jnp.exp2(so - m)
                l = jnp.sum(po, axis=-1, keepdims=True)
                p_off.append(po.astype(BF16))
            else:
                l = 0.0
            pd = jnp.exp2(sd - m)
            l = l + jnp.sum(pd, axis=-1, keepdims=True)
            p_dg.append(pd.astype(BF16))
            inv_l.append(1.0 / l)
        acc = jnp.dot(jnp.concatenate(p_dg, axis=0), v_ref[0, off:off + tq, :], preferred_element_type=F32)
        if n > 0:
            acc = acc + jnp.dot(jnp.concatenate(p_off, axis=0), v_ref[0, 0:off, :],
                                preferred_element_type=F32)
        dlt = acc[:tq] * inv_l[0] - lam * (acc[tq:] * inv_l[1])
        y = _rms_scale(dlt, g_ref[...]) * (1.0 - LAMBDA_INIT)
        o_ref[0, rows, :] = y.astype(o_ref.dtype)

    def group(gq):
        for part in range(tpg):
            tile(gq * tpg + part, part)

    for gq in range(nq // tpg):
        pl.when(qi == gq)(functools.partial(group, gq))


def _diff_attention(qk3, vx3, lq1, lk1, lq2, lk2, g_subln, tq=256, tpg=8):
    b, s, _ = qk3.shape
    hw = 2 * DA_HEAD_DIM
    nq = s // tq
    vec = lambda: pl.BlockSpec((1, DA_HEAD_DIM), lambda bi, h, qi: (0, 0))
    return pl.pallas_call(
        functools.partial(_da_kernel, tq=tq, nq=nq, tpg=tpg),
        grid=(b, DA_HEADS, nq // tpg),
        in_specs=[vec(), vec(), vec(), vec(),
                  pl.BlockSpec((1, hw), lambda bi, h, qi: (0, 0)),
                  pl.BlockSpec((1, tpg * tq, hw), lambda bi, h, qi: (bi, qi, h)),
                  pl.BlockSpec((1, s, hw), lambda bi, h, qi: (bi, 0, DA_HEADS + h)),
                  pl.BlockSpec((1, s, hw), lambda bi, h, qi: (bi, 0, h))],
        out_specs=pl.BlockSpec((1, tpg * tq, hw), lambda bi, h, qi: (bi, qi, h)),
        out_shape=jax.ShapeDtypeStruct((b, s, DA_WIDTH), BF16),
        compiler_params=_cparams(3),
        name="diff_attention",
    )(lq1.reshape(1, -1), lk1.reshape(1, -1), lq2.reshape(1, -1), lk2.reshape(1, -1),
      g_subln.reshape(1, hw), qk3, qk3, vx3)


def _proj_kernel(a_ref, w_ref, o_ref, wbf_ref):
    @pl.when(pl.program_id(1) == 0)
    def _():
        wbf_ref[...] = w_ref[...].astype(BF16)

    o_ref[...] = jnp.dot(a_ref[...], wbf_ref[...], preferred_element_type=F32).astype(o_ref.dtype)


def _proj(a, w, tm=1024, tn=1024):
    m, k = a.shape
    n = w.shape[1]
    return pl.pallas_call(
        _proj_kernel,
        grid=(n // tn, m // tm),
        in_specs=[pl.BlockSpec((tm, k), lambda j, i: (i, 0)),
                  pl.BlockSpec((k, tn), lambda j, i: (0, j))],
        out_specs=pl.BlockSpec((tm, tn), lambda j, i: (i, j)),
        out_shape=jax.ShapeDtypeStruct((m, n), BF16),
        scratch_shapes=[pltpu.VMEM((k, tn), BF16)],
        compiler_params=_cparams(2),
        name="proj",
    )(a, w)


def _xa_kernel(q_ref, k_ref, v_ref, o_ref):
    nt = (((1,), (1,)), ((), ()))
    s = lax.dot_general(q_ref[0], k_ref[0], nt, preferred_element_type=F32)
    m = jnp.max(s, axis=-1, keepdims=True)
    p = jnp.exp2(s - m)
    l = jnp.sum(p, axis=-1, keepdims=True)
    o = jnp.dot(p.astype(BF16), v_ref[0], preferred_element_type=F32)
    o_ref[0] = (o * (1.0 / l)).astype(o_ref.dtype)


def _cross_attention(vx3, kv3):
    b, s, _ = vx3.shape
    mem_len = kv3.shape[1]
    hd = XA_HEAD_DIM
    return pl.pallas_call(
        _xa_kernel,
        grid=(b, XA_HEADS),
        in_specs=[pl.BlockSpec((1, s, hd), lambda bi, h: (bi, 0, XA_HEADS + h)),
                  pl.BlockSpec((1, mem_len, hd), lambda bi, h: (bi, 0, h)),
                  pl.BlockSpec((1, mem_len, hd), lambda bi, h: (bi, 0, XA_HEADS + h))],
        out_specs=pl.BlockSpec((1, s, hd), lambda bi, h: (bi, 0, h)),
        out_shape=jax.ShapeDtypeStruct((b, s, XA_WIDTH), BF16),
        compiler_params=_cparams(2),
        name="cross_attention",
    )(vx3, kv3, kv3)


def _merge_kernel(oc_ref, od_ref, ox_ref, g0_ref, g1_ref, g2_ref, wc_ref, wd_ref, wx_ref, o_ref, wbf_ref):
    @pl.when(pl.program_id(1) == 0)
    def _():
        wbf_ref[0] = wc_ref[...].astype(BF16)
        wbf_ref[1] = wd_ref[...].astype(BF16)
        wbf_ref[2] = wx_ref[...].astype(BF16)

    y = g0_ref[...].astype(F32) * jnp.dot(oc_ref[...], wbf_ref[0], preferred_element_type=F32)
    y = y + g1_ref[...].astype(F32) * jnp.dot(od_ref[...], wbf_ref[1], preferred_element_type=F32)
    y = y + g2_ref[...].astype(F32) * jnp.dot(ox_ref[...], wbf_ref[2], preferred_element_type=F32)
    o_ref[...] = y.astype(o_ref.dtype)


def _merge(o_conv, o_da, o_xa, gates, w_conv_out, w_da_out, w_xa_out, tm=512, tn=1024):
    t, kk = o_conv.shape
    gstride = D_MODEL // tn
    act = lambda: pl.BlockSpec((tm, kk), lambda j, i: (i, 0))
    wsp = lambda: pl.BlockSpec((kk, tn), lambda j, i: (0, j))
    gate = lambda br: pl.BlockSpec((tm, tn), lambda j, i: (i, br * gstride + j))
    return pl.pallas_call(
        _merge_kernel,
        grid=(D_MODEL // tn, t // tm),
        in_specs=[act(), act(), act(), gate(0), gate(1), gate(2), wsp(), wsp(), wsp()],
        out_specs=pl.BlockSpec((tm, tn), lambda j, i: (i, j)),
        out_shape=jax.ShapeDtypeStruct((t, D_MODEL), BF16),
        scratch_shapes=[pltpu.VMEM((3, kk, tn), BF16)],
        compiler_params=_cparams(2),
        name="merge",
    )(o_conv, o_da, o_xa, gates, gates, gates, w_conv_out, w_da_out, w_xa_out)


def _mix_kernel(a_ref, w_ref, x_ref, g_ref, x1_ref, h2_ref, wbf_ref):
    @pl.when(pl.program_id(0) == 0)
    def _():
        wbf_ref[...] = w_ref[...].astype(BF16)

    x1 = x_ref[...] + jnp.dot(a_ref[...], wbf_ref[...], preferred_element_type=F32)
    x1_ref[...] = x1
    h2_ref[...] = _rms_scale(x1, g_ref[...]).astype(h2_ref.dtype)


def _mix(merged, w_mix, x2d, g_mlp, tm=512):
    t, k = merged.shape
    n = w_mix.shape[1]
    return pl.pallas_call(
        _mix_kernel,
        grid=(t // tm,),
        in_specs=[pl.BlockSpec((tm, k), lambda i: (i, 0)),
                  pl.BlockSpec((k, n), lambda i: (0, 0), pipeline_mode=pl.Buffered(1)),
                  pl.BlockSpec((tm, n), lambda i: (i, 0)),
                  pl.BlockSpec((1, n), lambda i: (0, 0))],
        out_specs=[pl.BlockSpec((tm, n), lambda i: (i, 0)),
                   pl.BlockSpec((tm, n), lambda i: (i, 0))],
        out_shape=[jax.ShapeDtypeStruct((t, n), F32),
                   jax.ShapeDtypeStruct((t, n), BF16)],
        scratch_shapes=[pltpu.VMEM((k, n), BF16)],
        compiler_params=_cparams(1),
        name="mix_out",
    )(merged, w_mix, x2d, g_mlp.reshape(1, n))


def _mlp_kernel(h_ref, x_hbm, wu_ref, wd_ref, gf_ref, o_ref, xbuf_ref, xsem, *, nf, tm):
    i = pl.program_id(0)
    f = pl.program_id(1)

    def residual_copy():
        rows = pl.ds(pl.multiple_of(i * tm, tm), tm)
        return pltpu.make_async_copy(x_hbm.at[rows, :], xbuf_ref, xsem)

    @pl.when(f == 0)
    def _():
        o_ref[...] = jnp.zeros_like(o_ref)

    @pl.when(f == nf - 2)
    def _():
        residual_copy().start()

    hm = jnp.dot(h_ref[...], wu_ref[...].astype(BF16), preferred_element_type=F32)
    hm = jnp.square(jnp.maximum(hm, 0.0)).astype(BF16)
    o_ref[...] += jnp.dot(hm, wd_ref[...].astype(BF16), preferred_element_type=F32)

    @pl.when(f == nf - 1)
    def _():
        residual_copy().wait()
        o_ref[...] = _rms_scale(xbuf_ref[...] + o_ref[...], gf_ref[...])


def _mlp(h2, x1, w_up, w_down, g_final, tm=1024, tf=512):
    t, d = x1.shape
    ff = w_up.shape[1]
    nf = ff // tf
    assert nf >= 2
    return pl.pallas_call(
        functools.partial(_mlp_kernel, nf=nf, tm=tm),
        grid=(t // tm, nf),
        in_specs=[pl.BlockSpec((tm, d), lambda i, f: (i, 0)),
                  pl.BlockSpec(memory_space=pl.ANY),
                  pl.BlockSpec((d, tf), lambda i, f: (0, f)),
                  pl.BlockSpec((tf, d), lambda i, f: (f, 0)),
                  pl.BlockSpec((1, d), lambda i, f: (0, 0))],
        out_specs=pl.BlockSpec((tm, d), lambda i, f: (i, 0)),
        out_shape=jax.ShapeDtypeStruct((t, d), F32),
        scratch_shapes=[pltpu.VMEM((tm, d), F32), pltpu.SemaphoreType.DMA(())],
        compiler_params=_cparams(2),
        name="mlp",
    )(h2, x1, w_up, w_down, g_final.reshape(1, d))


def kernel(x, mem, positions, g_mix, w_in, w_dw, b_dw, g_conv_ln, b_conv_ln, w_conv_out, lambda_q1, lambda_k1, lambda_q2, lambda_k2, g_subln, w_da_out, g_mem, w_mem_kv, w_xa_out, w_mix_out, g_mlp, w_up, w_down, g_final):
    bsz, seq, d = x.shape
    t = bsz * seq
    x2d = x.reshape(t, d)
    w_in0 = w_in[0]

    h = _rmsnorm(x2d, g_mix[0], BF16)
    cos2, sin2 = _rope_tables(positions.astype(F32).reshape(t, 1))

    u = _glu_proj(h, w_in0)
    table_spec = pl.BlockSpec((2048, DA_HEAD_DIM), lambda j, i: (i, 0))
    qk = _in_proj(functools.partial(_qk_kernel, tn=1024), "qk_proj", h, w_in0, QK_COL0, 2048,
                  extra=(cos2, sin2), extra_specs=(table_spec, table_spec), tm=2048)
    vx = _in_proj(_vx_kernel, "vx_proj", h, w_in0, VX_COL0, 2048, tm=2048)
    gates = _in_proj(_gate_kernel, "gate_proj", h, w_in0, GATE_COL0, 3 * D_MODEL, tm=2048)
    qk3 = qk.reshape(bsz, seq, 2048)
    vx3 = vx.reshape(bsz, seq, 2048)

    o_conv = _conv_branch(u.reshape(bsz, seq, CONV_CH), w_dw[0], b_dw[0], g_conv_ln[0], b_conv_ln[0])
    o_da = _diff_attention(qk3, vx3, lambda_q1[0], lambda_k1[0], lambda_q2[0], lambda_k2[0], g_subln[0])

    mem_n = _rmsnorm(mem.reshape(-1, d), g_mem[0], BF16)
    kv = _proj(mem_n, w_mem_kv[0])
    o_xa = _cross_attention(vx3, kv.reshape(bsz, -1, 2 * XA_WIDTH))

    merged = _merge(o_conv.reshape(t, CONV_CH), o_da.reshape(t, DA_WIDTH), o_xa.reshape(t, XA_WIDTH),
                    gates, w_conv_out[0], w_da_out[0], w_xa_out[0])
    x1, h2 = _mix(merged, w_mix_out[0], x2d, g_mlp[0])
    out = _mlp(h2, x1, w_up[0], w_down[0], g_final)
    return out.reshape(bsz, seq, d)
```

```python
import functools
import math

import jax
import jax.numpy as jnp
from jax import lax
from jax.experimental import pallas as pl
from jax.experimental.pallas import tpu as pltpu

F32 = jnp.float32
BF16 = jnp.bfloat16

D_MODEL = 2048
CONV_CH = 1024
CONV_WIDTH = 31
DA_HEADS = 4
DA_HEAD_DIM = 128
DA_WIDTH = 1024
XA_HEADS = 4
XA_HEAD_DIM = 256
XA_WIDTH = 1024
D_FF = 4 * D_MODEL
ROPE_THETA = 10000.0
NORM_EPS = 1e-6
LAMBDA_INIT = 0.8 - 0.6 * math.exp(-0.3 * 0)
LOG2E = math.log2(math.e)

QK_COL0 = 2 * CONV_CH
VX_COL0 = QK_COL0 + 2048
GATE_COL0 = VX_COL0 + 2048

V7X_VMEM_LIMIT = 60 * 1024 * 1024
CONV_HALO = 32
SUBLANES = 8
LANES = 128


def _cparams(n_axes):
    return pltpu.CompilerParams(
        dimension_semantics=("arbitrary",) * n_axes,
        vmem_limit_bytes=V7X_VMEM_LIMIT,
    )


def _sigmoid(x):
    return 0.5 * jnp.tanh(0.5 * x) + 0.5


def _rms_scale(x, g):
    ms = jnp.mean(x * x, axis=-1, keepdims=True)
    return x * lax.rsqrt(ms + NORM_EPS) * g


def _rmsnorm_kernel(x_ref, g_ref, o_ref):
    o_ref[...] = _rms_scale(x_ref[...], g_ref[...]).astype(o_ref.dtype)


def _rmsnorm(x2d, g, out_dtype, tm=512):
    m, d = x2d.shape
    return pl.pallas_call(
        _rmsnorm_kernel,
        grid=(m // tm,),
        in_specs=[pl.BlockSpec((tm, d), lambda i: (i, 0)),
                  pl.BlockSpec((1, d), lambda i: (0, 0))],
        out_specs=pl.BlockSpec((tm, d), lambda i: (i, 0)),
        out_shape=jax.ShapeDtypeStruct((m, d), out_dtype),
        compiler_params=_cparams(1),
        name="rmsnorm",
    )(x2d, g.reshape(1, d))


def _glu_kernel(h_ref, wa_ref, wb_ref, o_ref, wbf_ref, *, tn):
    @pl.when(pl.program_id(1) == 0)
    def _():
        wbf_ref[:, :tn] = wa_ref[...].astype(BF16)
        wbf_ref[:, tn:] = wb_ref[...].astype(BF16)

    acc = jnp.dot(h_ref[...], wbf_ref[...], preferred_element_type=F32)
    o_ref[...] = (acc[:, :tn] * _sigmoid(acc[:, tn:])).astype(o_ref.dtype)


def _glu_proj(h, w_in, tm=2048, tn=512):
    t, d = h.shape
    nb = CONV_CH // tn
    return pl.pallas_call(
        functools.partial(_glu_kernel, tn=tn),
        grid=(nb, t // tm),
        in_specs=[pl.BlockSpec((tm, d), lambda j, i: (i, 0)),
                  pl.BlockSpec((d, tn), lambda j, i: (0, j)),
                  pl.BlockSpec((d, tn), lambda j, i: (0, nb + j))],
        out_specs=pl.BlockSpec((tm, tn), lambda j, i: (i, j)),
        out_shape=jax.ShapeDtypeStruct((t, CONV_CH), BF16),
        scratch_shapes=[pltpu.VMEM((d, 2 * tn), BF16)],
        compiler_params=_cparams(2),
        name="glu_proj",
    )(h, w_in, w_in)


def _rope_kernel(pos_lo_ref, pos_hi_ref, f_ref, cos_ref, sin_ref, *, half_rows):
    half = DA_HEAD_DIM // 2
    lane = lax.broadcasted_iota(jnp.int32, (half_rows, DA_HEAD_DIM), 1)
    left = lane < half
    ang = jnp.where(left, pos_lo_ref[...], pos_hi_ref[...]) * f_ref[...]
    c = jnp.cos(ang)
    s = jnp.sin(ang)
    c_sw = pltpu.roll(c, half, axis=1)
    s_sw = pltpu.roll(s, half, axis=1)
    cos_ref[0:half_rows, :] = jnp.where(left, c, c_sw)
    cos_ref[half_rows:, :] = jnp.where(left, c_sw, c)
    sin_ref[0:half_rows, :] = jnp.where(left, -s, s_sw)
    sin_ref[half_rows:, :] = jnp.where(left, -s_sw, s)


def _rope_tables(pos_col, tm=1024):
    t = pos_col.shape[0]
    inv_freq = 1.0 / (ROPE_THETA ** (jnp.arange(0, DA_HEAD_DIM, 2, dtype=F32) / DA_HEAD_DIM))
    f2 = jnp.concatenate([inv_freq, inv_freq]).reshape(1, DA_HEAD_DIM)
    half_rows = tm // 2
    return pl.pallas_call(
        functools.partial(_rope_kernel, half_rows=half_rows),
        grid=(t // tm,),
        in_specs=[pl.BlockSpec((half_rows, 1), lambda i: (2 * i, 0)),
                  pl.BlockSpec((half_rows, 1), lambda i: (2 * i + 1, 0)),
                  pl.BlockSpec((1, DA_HEAD_DIM), lambda i: (0, 0))],
        out_specs=[pl.BlockSpec((tm, DA_HEAD_DIM), lambda i: (i, 0))] * 2,
        out_shape=[jax.ShapeDtypeStruct((t, DA_HEAD_DIM), F32)] * 2,
        compiler_params=_cparams(1),
        name="rope_tables",
    )(pos_col, pos_col, f2)


def _qk_kernel(h_ref, w_ref, cos_ref, sin_ref, o_ref, wbf_ref, *, tn):
    @pl.when(pl.program_id(1) == 0)
    def _():
        wbf_ref[...] = w_ref[...].astype(BF16)

    acc = jnp.dot(h_ref[...], wbf_ref[...], preferred_element_type=F32)
    scale = jnp.where(pl.program_id(0) == 0, DA_HEAD_DIM ** -0.5 * LOG2E, 1.0).astype(F32)
    c = cos_ref[...] * scale
    s = sin_ref[...] * scale
    for g in range(tn // DA_HEAD_DIM):
        sl = slice(g * DA_HEAD_DIM, (g + 1) * DA_HEAD_DIM)
        t = acc[:, sl]
        r = pltpu.roll(t, DA_HEAD_DIM // 2, axis=1)
        o_ref[:, sl] = (t * c + r * s).astype(o_ref.dtype)


def _vx_kernel(h_ref, w_ref, o_ref, wbf_ref):
    @pl.when(pl.program_id(1) == 0)
    def _():
        wbf_ref[...] = w_ref[...].astype(BF16)

    acc = jnp.dot(h_ref[...], wbf_ref[...], preferred_element_type=F32)
    scale = jnp.where(pl.program_id(0) == 1, XA_HEAD_DIM ** -0.5 * LOG2E, 1.0).astype(F32)
    o_ref[...] = (acc * scale).astype(o_ref.dtype)


def _gate_kernel(h_ref, w_ref, o_ref, wbf_ref):
    @pl.when(pl.program_id(1) == 0)
    def _():
        wbf_ref[...] = w_ref[...].astype(BF16)

    acc = jnp.dot(h_ref[...], wbf_ref[...], preferred_element_type=F32)
    o_ref[...] = _sigmoid(acc).astype(o_ref.dtype)


def _in_proj(body, name, h, w_in, col0, ncols, extra=(), extra_specs=(), tm=1024, tn=1024):
    t, d = h.shape
    cb = col0 // tn
    extra_specs = list(extra_specs)
    return pl.pallas_call(
        body,
        grid=(ncols // tn, t // tm),
        in_specs=[pl.BlockSpec((tm, d), lambda j, i: (i, 0)),
                  pl.BlockSpec((d, tn), lambda j, i: (0, cb + j))] + extra_specs,
        out_specs=pl.BlockSpec((tm, tn), lambda j, i: (i, j)),
        out_shape=jax.ShapeDtypeStruct((t, ncols), BF16),
        scratch_shapes=[pltpu.VMEM((d, tn), BF16)],
        compiler_params=_cparams(2),
        name=name,
    )(h, w_in, *extra)


def _conv_kernel(ucur_ref, uprev_ref, w_ref, bdw_ref, g_ref, b_ref, o_ref, buf_ref, sh_ref, y_ref, *, ts, rg, rc):
    si = pl.program_id(1)
    prev = uprev_ref[0].astype(F32)
    buf_ref[0:CONV_HALO, :] = jnp.where(si > 0, prev, 0.0)
    buf_ref[CONV_HALO:CONV_HALO + ts, :] = ucur_ref[0].astype(F32)
    buf_ref[CONV_HALO + ts:, :] = jnp.zeros((SUBLANES, CONV_CH), F32)
    n_sh = ts + CONV_HALO
    for r in range(1, SUBLANES):
        sh_ref[r - 1] = buf_ref[pl.ds(r, n_sh), :]

    lead = CONV_HALO - (CONV_WIDTH - 1)
    rows_per_iter = rg * SUBLANES

    for c in range(CONV_CH // LANES):
        lanes = slice(c * LANES, (c + 1) * LANES)
        taps = [jnp.broadcast_to(w_ref[k:k + 1, lanes], (SUBLANES, LANES)) for k in range(CONV_WIDTH)]
        bias = jnp.broadcast_to(bdw_ref[:, lanes], (SUBLANES, LANES))

        def rows_body(it, carry, lanes=lanes, taps=taps, bias=bias):
            r0 = pl.multiple_of(it * rows_per_iter, rows_per_iter)
            for gi in range(rg):
                acc = bias
                for k in range(CONV_WIDTH):
                    a, r = divmod(lead + k, SUBLANES)
                    row0 = pl.multiple_of(r0 + SUBLANES * (a + gi), SUBLANES)
                    if r == 0:
                        rows = buf_ref[pl.ds(row0, SUBLANES), lanes]
                    else:
                        rows = sh_ref[r - 1, pl.ds(row0, SUBLANES), lanes]
                    acc = acc + rows * taps[k]
                y_ref[pl.ds(pl.multiple_of(r0 + SUBLANES * gi, SUBLANES), SUBLANES), lanes] = acc
            return carry

        lax.fori_loop(0, ts // rows_per_iter, rows_body, 0)

    for ci in range(ts // rc):
        rows = slice(ci * rc, (ci + 1) * rc)
        acc = y_ref[rows, :]
        mu = jnp.mean(acc, axis=-1, keepdims=True)
        cen = acc - mu
        var = jnp.mean(cen * cen, axis=-1, keepdims=True)
        y = cen * lax.rsqrt(var + NORM_EPS) * g_ref[...] + b_ref[...]
        o_ref[0, rows, :] = (y * _sigmoid(y)).astype(o_ref.dtype)


def _conv_branch(u3, w_dw, b_dw, g_ln, b_ln, ts=512, rg=16, rc=128):
    b, s, c = u3.shape
    hb = ts // CONV_HALO
    return pl.pallas_call(
        functools.partial(_conv_kernel, ts=ts, rg=rg, rc=rc),
        grid=(b, s // ts),
        in_specs=[pl.BlockSpec((1, ts, c), lambda bi, si: (bi, si, 0)),
                  pl.BlockSpec((1, CONV_HALO, c), lambda bi, si: (bi, jnp.maximum(si * hb - 1, 0), 0)),
                  pl.BlockSpec((CONV_WIDTH, c), lambda bi, si: (0, 0)),
                  pl.BlockSpec((1, c), lambda bi, si: (0, 0)),
                  pl.BlockSpec((1, c), lambda bi, si: (0, 0)),
                  pl.BlockSpec((1, c), lambda bi, si: (0, 0))],
        out_specs=pl.BlockSpec((1, ts, c), lambda bi, si: (bi, si, 0)),
        out_shape=jax.ShapeDtypeStruct((b, s, c), BF16),
        scratch_shapes=[pltpu.VMEM((CONV_HALO + ts + SUBLANES, c), F32),
                        pltpu.VMEM((SUBLANES - 1, CONV_HALO + ts, c), F32),
                        pltpu.VMEM((ts, c), F32)],
        compiler_params=_cparams(2),
        name="conv_branch",
    )(u3, u3, w_dw, b_dw.reshape(1, c), g_ln.reshape(1, c), b_ln.reshape(1, c))


def _da_tiles(lq1_ref, lk1_ref, lq2_ref, lk2_ref, g_ref, q_ref, k_ref, v_ref, o_ref, *, tq, tiles):
    hd = DA_HEAD_DIM
    nt = (((1,), (1,)), ((), ()))
    lam = (jnp.exp(jnp.sum(lq1_ref[...] * lk1_ref[...], axis=-1, keepdims=True))
           - jnp.exp(jnp.sum(lq2_ref[...] * lk2_ref[...], axis=-1, keepdims=True))
           + LAMBDA_INIT)

    def tile(n, part):
        off = n * tq
        rows = slice(part * tq, (part + 1) * tq)
        q = q_ref[0, rows, :]
        causal = (lax.broadcasted_iota(jnp.int32, (tq, tq), 0)
                  >= lax.broadcasted_iota(jnp.int32, (tq, tq), 1))
        p_off, p_dg, inv_l = [], [], []
        for mp in range(2):
            cs = slice(mp * hd, (mp + 1) * hd)
            sd = lax.dot_general(q[:, cs], k_ref[0, off:off + tq, cs], nt, preferred_element_type=F32)
            sd = jnp.where(causal, sd, -jnp.inf)
            m = jnp.max(sd, axis=-1, keepdims=True)
            if n > 0:
                so = lax.dot_general(q[:, cs], k_ref[0, 0:off, cs], nt, preferred_element_type=F32)
                m = jnp.maximum(m, jnp.max(so, axis=-1, keepdims=True))
                po = jnp.exp2(so - m)
                l = jnp.sum(po, axis=-1, keepdims=True)
                p_off.append(po.astype(BF16))
            else:
                l = 0.0
            pd = jnp.exp2(sd - m)
            l = l + jnp.sum(pd, axis=-1, keepdims=True)
            p_dg.append(pd.astype(BF16))
            inv_l.append(1.0 / l)
        acc = jnp.dot(jnp.concatenate(p_dg, axis=0), v_ref[0, off:off + tq, :], preferred_element_type=F32)
        if n > 0:
            acc = acc + jnp.dot(jnp.concatenate(p_off, axis=0), v_ref[0, 0:off, :],
                                preferred_element_type=F32)
        dlt = acc[:tq] * inv_l[0] - lam * (acc[tq:] * inv_l[1])
        y = _rms_scale(dlt, g_ref[...]) * (1.0 - LAMBDA_INIT)
        o_ref[0, rows, :] = y.astype(o_ref.dtype)

    for n, part in tiles:
        tile(n, part)


def _da_kernel(*refs, tq, nq, tpg):
    if tpg == nq:
        _da_tiles(*refs, tq=tq, tiles=[(n, n) for n in range(nq)])
        return
    qi = pl.program_id(2)
    for gq in range(nq // tpg):
        tiles = [(gq * tpg + part, part) for part in range(tpg)]
        pl.when(qi == gq)(functools.partial(_da_tiles, *refs, tq=tq, tiles=tiles))


def _diff_attention(qk3, vx3, lq1, lk1, lq2, lk2, g_subln, tq=256, tpg=8):
    b, s, _ = qk3.shape
    hw = 2 * DA_HEAD_DIM
    nq = s // tq
    vec = lambda: pl.BlockSpec((1, DA_HEAD_DIM), lambda bi, h, qi: (0, 0))
    return pl.pallas_call(
        functools.partial(_da_kernel, tq=tq, nq=nq, tpg=tpg),
        grid=(b, DA_HEADS, nq // tpg),
        in_specs=[vec(), vec(), vec(), vec(),
                  pl.BlockSpec((1, hw), lambda bi, h, qi: (0, 0)),
                  pl.BlockSpec((1, tpg * tq, hw), lambda bi, h, qi: (bi, qi, h)),
                  pl.BlockSpec((1, s, hw), lambda bi, h, qi: (bi, 0, DA_HEADS + h)),
                  pl.BlockSpec((1, s, hw), lambda bi, h, qi: (bi, 0, h))],
        out_specs=pl.BlockSpec((1, tpg * tq, hw), lambda bi, h, qi: (bi, qi, h)),
        out_shape=jax.ShapeDtypeStruct((b, s, DA_WIDTH), BF16),
        compiler_params=_cparams(3),
        name="diff_attention",
    )(lq1.reshape(1, -1), lk1.reshape(1, -1), lq2.reshape(1, -1), lk2.reshape(1, -1),
      g_subln.reshape(1, hw), qk3, qk3, vx3)


def _proj_kernel(a_ref, w_ref, o_ref, wbf_ref):
    @pl.when(pl.program_id(1) == 0)
    def _():
        wbf_ref[...] = w_ref[...].astype(BF16)

    o_ref[...] = jnp.dot(a_ref[...], wbf_ref[...], preferred_element_type=F32).astype(o_ref.dtype)


def _proj(a, w, tm=1024, tn=1024):
    m, k = a.shape
    n = w.shape[1]
    return pl.pallas_call(
        _proj_kernel,
        grid=(n // tn, m // tm),
        in_specs=[pl.BlockSpec((tm, k), lambda j, i: (i, 0)),
                  pl.BlockSpec((k, tn), lambda j, i: (0, j))],
        out_specs=pl.BlockSpec((tm, tn), lambda j, i: (i, j)),
        out_shape=jax.ShapeDtypeStruct((m, n), BF16),
        scratch_shapes=[pltpu.VMEM((k, tn), BF16)],
        compiler_params=_cparams(2),
        name="proj",
    )(a, w)


def _xa_kernel(q_ref, k_ref, v_ref, o_ref):
    nt = (((1,), (1,)), ((), ()))
    s = lax.dot_general(q_ref[0], k_ref[0], nt, preferred_element_type=F32)
    m = jnp.max(s, axis=-1, keepdims=True)
    p = jnp.exp2(s - m)
    l = jnp.sum(p, axis=-1, keepdims=True)
    o = jnp.dot(p.astype(BF16), v_ref[0], preferred_element_type=F32)
    o_ref[0] = (o * (1.0 / l)).astype(o_ref.dtype)


def _cross_attention(vx3, kv3):
    b, s, _ = vx3.shape
    mem_len = kv3.shape[1]
    hd = XA_HEAD_DIM
    return pl.pallas_call(
        _xa_kernel,
        grid=(b, XA_HEADS),
        in_specs=[pl.BlockSpec((1, s, hd), lambda bi, h: (bi, 0, XA_HEADS + h)),
                  pl.BlockSpec((1, mem_len, hd), lambda bi, h: (bi, 0, h)),
                  pl.BlockSpec((1, mem_len, hd), lambda bi, h: (bi, 0, XA_HEADS + h))],
        out_specs=pl.BlockSpec((1, s, hd), lambda bi, h: (bi, 0, h)),
        out_shape=jax.ShapeDtypeStruct((b, s, XA_WIDTH), BF16),
        compiler_params=_cparams(2),
        name="cross_attention",
    )(vx3, kv3, kv3)


def _merge_kernel(oc_ref, od_ref, ox_ref, g0_ref, g1_ref, g2_ref, wc_ref, wd_ref, wx_ref, o_ref, wbf_ref):
    @pl.when(pl.program_id(1) == 0)
    def _():
        wbf_ref[0] = wc_ref[...].astype(BF16)
        wbf_ref[1] = wd_ref[...].astype(BF16)
        wbf_ref[2] = wx_ref[...].astype(BF16)

    y = g0_ref[...].astype(F32) * jnp.dot(oc_ref[...], wbf_ref[0], preferred_element_type=F32)
    y = y + g1_ref[...].astype(F32) * jnp.dot(od_ref[...], wbf_ref[1], preferred_element_type=F32)
    y = y + g2_ref[...].astype(F32) * jnp.dot(ox_ref[...], wbf_ref[2], preferred_element_type=F32)
    o_ref[...] = y.astype(o_ref.dtype)


def _merge(o_conv, o_da, o_xa, gates, w_conv_out, w_da_out, w_xa_out, tm=1024, tn=1024):
    t, kk = o_conv.shape
    gstride = D_MODEL // tn
    act = lambda: pl.BlockSpec((tm, kk), lambda j, i: (i, 0))
    wsp = lambda: pl.BlockSpec((kk, tn), lambda j, i: (0, j), pipeline_mode=pl.Buffered(1))
    gate = lambda br: pl.BlockSpec((tm, tn), lambda j, i: (i, br * gstride + j))
    return pl.pallas_call(
        _merge_kernel,
        grid=(D_MODEL // tn, t // tm),
        in_specs=[act(), act(), act(), gate(0), gate(1), gate(2), wsp(), wsp(), wsp()],
        out_specs=pl.BlockSpec((tm, tn), lambda j, i: (i, j)),
        out_shape=jax.ShapeDtypeStruct((t, D_MODEL), BF16),
        scratch_shapes=[pltpu.VMEM((3, kk, tn), BF16)],
        compiler_params=_cparams(2),
        name="merge",
    )(o_conv, o_da, o_xa, gates, gates, gates, w_conv_out, w_da_out, w_xa_out)


def _mix_kernel(a_ref, w_ref, x_ref, g_ref, x1_ref, h2_ref, wbf_ref):
    @pl.when(pl.program_id(0) == 0)
    def _():
        wbf_ref[...] = w_ref[...].astype(BF16)

    x1 = x_ref[...] + jnp.dot(a_ref[...], wbf_ref[...], preferred_element_type=F32)
    x1_ref[...] = x1
    h2_ref[...] = _rms_scale(x1, g_ref[...]).astype(h2_ref.dtype)


def _mix(merged, w_mix, x2d, g_mlp, tm=512):
    t, k = merged.shape
    n = w_mix.shape[1]
    return pl.pallas_call(
        _mix_kernel,
        grid=(t // tm,),
        in_specs=[pl.BlockSpec((tm, k), lambda i: (i, 0)),
                  pl.BlockSpec((k, n), lambda i: (0, 0), pipeline_mode=pl.Buffered(1)),
                  pl.BlockSpec((tm, n), lambda i: (i, 0)),
                  pl.BlockSpec((1, n), lambda i: (0, 0))],
        out_specs=[pl.BlockSpec((tm, n), lambda i: (i, 0)),
                   pl.BlockSpec((tm, n), lambda i: (i, 0))],
        out_shape=[jax.ShapeDtypeStruct((t, n), F32),
                   jax.ShapeDtypeStruct((t, n), BF16)],
        scratch_shapes=[pltpu.VMEM((k, n), BF16)],
        compiler_params=_cparams(1),
        name="mix_out",
    )(merged, w_mix, x2d, g_mlp.reshape(1, n))


def _mlp_kernel(h_ref, x_hbm, wu_ref, wd_ref, gf_ref, o_ref, xbuf_ref, xsem, *, nf, tm):
    i = pl.program_id(0)
    f = pl.program_id(1)

    def residual_copy():
        rows = pl.ds(pl.multiple_of(i * tm, tm), tm)
        return pltpu.make_async_copy(x_hbm.at[rows, :], xbuf_ref, xsem)

    @pl.when(f == 0)
    def _():
        o_ref[...] = jnp.zeros_like(o_ref)

    @pl.when(f == nf - 2)
    def _():
        residual_copy().start()

    hm = jnp.dot(h_ref[...], wu_ref[...].astype(BF16), preferred_element_type=F32)
    hm = jnp.square(jnp.maximum(hm, 0.0)).astype(BF16)
    o_ref[...] += jnp.dot(hm, wd_ref[...].astype(BF16), preferred_element_type=F32)

    @pl.when(f == nf - 1)
    def _():
        residual_copy().wait()
        o_ref[...] = _rms_scale(xbuf_ref[...] + o_ref[...], gf_ref[...])


def _mlp(h2, x1, w_up, w_down, g_final, tm=1024, tf=512):
    t, d = x1.shape
    ff = w_up.shape[1]
    nf = ff // tf
    assert nf >= 2
    return pl.pallas_call(
        functools.partial(_mlp_kernel, nf=nf, tm=tm),
        grid=(t // tm, nf),
        in_specs=[pl.BlockSpec((tm, d), lambda i, f: (i, 0)),
                  pl.BlockSpec(memory_space=pl.ANY),
                  pl.BlockSpec((d, tf), lambda i, f: (0, f)),
                  pl.BlockSpec((tf, d), lambda i, f: (f, 0)),
                  pl.BlockSpec((1, d), lambda i, f: (0, 0))],
        out_specs=pl.BlockSpec((tm, d), lambda i, f: (i, 0)),
        out_shape=jax.ShapeDtypeStruct((t, d), F32),
        scratch_shapes=[pltpu.VMEM((tm, d), F32), pltpu.SemaphoreType.DMA(())],
        compiler_params=_cparams(2),
        name="mlp",
    )(h2, x1, w_up, w_down, g_final.reshape(1, d))


def kernel(x, mem, positions, g_mix, w_in, w_dw, b_dw, g_conv_ln, b_conv_ln, w_conv_out, lambda_q1, lambda_k1, lambda_q2, lambda_k2, g_subln, w_da_out, g_mem, w_mem_kv, w_xa_out, w_mix_out, g_mlp, w_up, w_down, g_final):
    bsz, seq, d = x.shape
    t = bsz * seq
    x2d = x.reshape(t, d)
    w_in0 = w_in[0]

    h = _rmsnorm(x2d, g_mix[0], BF16)
    cos2, sin2 = _rope_tables(positions.astype(F32).reshape(t, 1))

    u = _glu_proj(h, w_in0)
    table_spec = pl.BlockSpec((2048, DA_HEAD_DIM), lambda j, i: (i, 0))
    qk = _in_proj(functools.partial(_qk_kernel, tn=1024), "qk_proj", h, w_in0, QK_COL0, 2048,
                  extra=(cos2, sin2), extra_specs=(table_spec, table_spec), tm=2048)
    vx = _in_proj(_vx_kernel, "vx_proj", h, w_in0, VX_COL0, 2048, tm=2048)
    gates = _in_proj(_gate_kernel, "gate_proj", h, w_in0, GATE_COL0, 3 * D_MODEL, tm=2048)
    qk3 = qk.reshape(bsz, seq, 2048)
    vx3 = vx.reshape(bsz, seq, 2048)

    o_conv = _conv_branch(u.reshape(bsz, seq, CONV_CH), w_dw[0], b_dw[0], g_conv_ln[0], b_conv_ln[0])
    o_da = _diff_attention(qk3, vx3, lambda_q1[0], lambda_k1[0], lambda_q2[0], lambda_k2[0], g_subln[0])

    mem_n = _rmsnorm(mem.reshape(-1, d), g_mem[0], BF16)
    kv = _proj(mem_n, w_mem_kv[0])
    o_xa = _cross_attention(vx3, kv.reshape(bsz, -1, 2 * XA_WIDTH))

    merged = _merge(o_conv.reshape(t, CONV_CH), o_da.reshape(t, DA_WIDTH), o_xa.reshape(t, XA_WIDTH),
                    gates, w_conv_out[0], w_da_out[0], w_xa_out[0])
    x1, h2 = _mix(merged, w_mix_out[0], x2d, g_mlp[0])
    out = _mlp(h2, x1, w_up[0], w_down[0], g_final)
    return out.reshape(bsz, seq, d)
```

```python
import functools
import math

import jax
import jax.numpy as jnp
from jax import lax
from jax.experimental import pallas as pl
from jax.experimental.pallas import tpu as pltpu

F32 = jnp.float32
BF16 = jnp.bfloat16

D_MODEL = 2048
CONV_CH = 1024
CONV_WIDTH = 31
DA_HEADS = 4
DA_HEAD_DIM = 128
DA_WIDTH = 1024
XA_HEADS = 4
XA_HEAD_DIM = 256
XA_WIDTH = 1024
D_FF = 4 * D_MODEL
ROPE_THETA = 10000.0
NORM_EPS = 1e-6
LAMBDA_INIT = 0.8 - 0.6 * math.exp(-0.3 * 0)
LOG2E = math.log2(math.e)

IN_TILE = 1024
IN_HALF = IN_TILE // 2
N_GLU_TILES = CONV_CH // IN_HALF
P_COLS = 2 * 1024 + 1024 + XA_WIDTH + 3 * D_MODEL
Q_OFF, K_OFF, V_OFF, XQ_OFF, GATE_OFF = 0, 1024, 2048, 3072, 4096

V7X_VMEM_LIMIT = 60 * 1024 * 1024
CONV_HALO = 32
SUBLANES = 8
LANES = 128


def _cparams(n_axes):
    return pltpu.CompilerParams(
        dimension_semantics=("arbitrary",) * n_axes,
        vmem_limit_bytes=V7X_VMEM_LIMIT,
    )


def _sigmoid(x):
    return 0.5 * jnp.tanh(0.5 * x) + 0.5


def _rms_scale(x, g):
    ms = jnp.mean(x * x, axis=-1, keepdims=True)
    return x * lax.rsqrt(ms + NORM_EPS) * g


def _rmsnorm_kernel(x_ref, g_ref, o_ref):
    o_ref[...] = _rms_scale(x_ref[...], g_ref[...]).astype(o_ref.dtype)


def _rmsnorm(x2d, g, out_dtype, tm=512):
    m, d = x2d.shape
    return pl.pallas_call(
        _rmsnorm_kernel,
        grid=(m // tm,),
        in_specs=[pl.BlockSpec((tm, d), lambda i: (i, 0)),
                  pl.BlockSpec((1, d), lambda i: (0, 0))],
        out_specs=pl.BlockSpec((tm, d), lambda i: (i, 0)),
        out_shape=jax.ShapeDtypeStruct((m, d), out_dtype),
        compiler_params=_cparams(1),
        name="rmsnorm",
    )(x2d, g.reshape(1, d))


def _rope_kernel(pos_lo_ref, pos_hi_ref, f_ref, cos_ref, sin_ref, *, half_rows):
    half = DA_HEAD_DIM // 2
    lane = lax.broadcasted_iota(jnp.int32, (half_rows, DA_HEAD_DIM), 1)
    left = lane < half
    ang = jnp.where(left, pos_lo_ref[...], pos_hi_ref[...]) * f_ref[...]
    c = jnp.cos(ang)
    s = jnp.sin(ang)
    c_sw = pltpu.roll(c, half, axis=1)
    s_sw = pltpu.roll(s, half, axis=1)
    cos_ref[0:half_rows, :] = jnp.where(left, c, c_sw)
    cos_ref[half_rows:, :] = jnp.where(left, c_sw, c)
    sin_ref[0:half_rows, :] = jnp.where(left, -s, s_sw)
    sin_ref[half_rows:, :] = jnp.where(left, -s_sw, s)


def _rope_tables(pos_col, tm=1024):
    t = pos_col.shape[0]
    inv_freq = 1.0 / (ROPE_THETA ** (jnp.arange(0, DA_HEAD_DIM, 2, dtype=F32) / DA_HEAD_DIM))
    f2 = jnp.concatenate([inv_freq, inv_freq]).reshape(1, DA_HEAD_DIM)
    half_rows = tm // 2
    return pl.pallas_call(
        functools.partial(_rope_kernel, half_rows=half_rows),
        grid=(t // tm,),
        in_specs=[pl.BlockSpec((half_rows, 1), lambda i: (2 * i, 0)),
                  pl.BlockSpec((half_rows, 1), lambda i: (2 * i + 1, 0)),
                  pl.BlockSpec((1, DA_HEAD_DIM), lambda i: (0, 0))],
        out_specs=[pl.BlockSpec((tm, DA_HEAD_DIM), lambda i: (i, 0))] * 2,
        out_shape=[jax.ShapeDtypeStruct((t, DA_HEAD_DIM), F32)] * 2,
        compiler_params=_cparams(1),
        name="rope_tables",
    )(pos_col, pos_col, f2)


def _in_proj_kernel(h_ref, wa_ref, wb_ref, cos_ref, sin_ref, u_ref, p_ref, wbf_ref):
    j = pl.program_id(0)
    n = j - N_GLU_TILES

    @pl.when(pl.program_id(1) == 0)
    def _():
        wbf_ref[:, :IN_HALF] = wa_ref[...].astype(BF16)
        wbf_ref[:, IN_HALF:] = wb_ref[...].astype(BF16)

    def proj():
        return jnp.dot(h_ref[...], wbf_ref[...], preferred_element_type=F32)

    @pl.when(j < N_GLU_TILES)
    def _():
        acc = proj()
        u_ref[...] = (acc[:, :IN_HALF] * _sigmoid(acc[:, IN_HALF:])).astype(u_ref.dtype)

    @pl.when((n >= 0) & (n < 2))
    def _():
        acc = proj()
        scale = jnp.where(n == 0, DA_HEAD_DIM ** -0.5 * LOG2E, 1.0).astype(F32)
        c = cos_ref[...] * scale
        s = sin_ref[...] * scale
        for g in range(IN_TILE // DA_HEAD_DIM):
            sl = slice(g * DA_HEAD_DIM, (g + 1) * DA_HEAD_DIM)
            t = acc[:, sl]
            r = pltpu.roll(t, DA_HEAD_DIM // 2, axis=1)
            p_ref[:, sl] = (t * c + r * s).astype(p_ref.dtype)

    @pl.when((n >= 2) & (n < 4))
    def _():
        scale = jnp.where(n == 3, XA_HEAD_DIM ** -0.5 * LOG2E, 1.0).astype(F32)
        p_ref[...] = (proj() * scale).astype(p_ref.dtype)

    @pl.when(n >= 4)
    def _():
        p_ref[...] = _sigmoid(proj()).astype(p_ref.dtype)


def _in_proj(h, w_in, cos2, sin2, tm=1024):
    t, d = h.shape
    n_i = t // tm
    glu = lambda j: j < N_GLU_TILES
    wa_map = lambda j, i: (0, jnp.where(glu(j), j, 2 * j))
    wb_map = lambda j, i: (0, jnp.where(glu(j), j + N_GLU_TILES, 2 * j + 1))
    u_map = lambda j, i: (jnp.where(glu(j), i, n_i - 1), jnp.where(glu(j), j, N_GLU_TILES - 1))
    p_map = lambda j, i: (jnp.where(glu(j), 0, i), jnp.where(glu(j), 0, j - N_GLU_TILES))
    return pl.pallas_call(
        _in_proj_kernel,
        grid=(N_GLU_TILES + P_COLS // IN_TILE, n_i),
        in_specs=[pl.BlockSpec((tm, d), lambda j, i: (i, 0)),
                  pl.BlockSpec((d, IN_HALF), wa_map),
                  pl.BlockSpec((d, IN_HALF), wb_map),
                  pl.BlockSpec((tm, DA_HEAD_DIM), lambda j, i: (i, 0)),
                  pl.BlockSpec((tm, DA_HEAD_DIM), lambda j, i: (i, 0))],
        out_specs=[pl.BlockSpec((tm, IN_HALF), u_map),
                   pl.BlockSpec((tm, IN_TILE), p_map)],
        out_shape=[jax.ShapeDtypeStruct((t, CONV_CH), BF16),
                   jax.ShapeDtypeStruct((t, P_COLS), BF16)],
        scratch_shapes=[pltpu.VMEM((d, IN_TILE), BF16)],
        compiler_params=_cparams(2),
        name="in_proj",
    )(h, w_in, w_in, cos2, sin2)


def _conv_kernel(ucur_ref, uprev_ref, w_ref, bdw_ref, g_ref, b_ref, o_ref, buf_ref, sh_ref, y_ref, *, ts, rg, rc):
    si = pl.program_id(1)
    prev = uprev_ref[0].astype(F32)
    buf_ref[0:CONV_HALO, :] = jnp.where(si > 0, prev, 0.0)
    buf_ref[CONV_HALO:CONV_HALO + ts, :] = ucur_ref[0].astype(F32)
    buf_ref[CONV_HALO + ts:, :] = jnp.zeros((SUBLANES, CONV_CH), F32)
    n_sh = ts + CONV_HALO
    for r in range(1, SUBLANES):
        sh_ref[r - 1] = buf_ref[pl.ds(r, n_sh), :]

    lead = CONV_HALO - (CONV_WIDTH - 1)
    rows_per_iter = rg * SUBLANES

    for c in range(CONV_CH // LANES):
        lanes = slice(c * LANES, (c + 1) * LANES)
        taps = [jnp.broadcast_to(w_ref[k:k + 1, lanes], (SUBLANES, LANES)) for k in range(CONV_WIDTH)]
        bias = jnp.broadcast_to(bdw_ref[:, lanes], (SUBLANES, LANES))

        def rows_body(it, carry, lanes=lanes, taps=taps, bias=bias):
            r0 = pl.multiple_of(it * rows_per_iter, rows_per_iter)
            for gi in range(rg):
                acc = bias
                for k in range(CONV_WIDTH):
                    a, r = divmod(lead + k, SUBLANES)
                    row0 = pl.multiple_of(r0 + SUBLANES * (a + gi), SUBLANES)
                    if r == 0:
                        rows = buf_ref[pl.ds(row0, SUBLANES), lanes]
                    else:
                        rows = sh_ref[r - 1, pl.ds(row0, SUBLANES), lanes]
                    acc = acc + rows * taps[k]
                y_ref[pl.ds(pl.multiple_of(r0 + SUBLANES * gi, SUBLANES), SUBLANES), lanes] = acc
            return carry

        lax.fori_loop(0, ts // rows_per_iter, rows_body, 0)

    for ci in range(ts // rc):
        rows = slice(ci * rc, (ci + 1) * rc)
        acc = y_ref[rows, :]
        mu = jnp.mean(acc, axis=-1, keepdims=True)
        cen = acc - mu
        var = jnp.mean(cen * cen, axis=-1, keepdims=True)
        y = cen * lax.rsqrt(var + NORM_EPS) * g_ref[...] + b_ref[...]
        o_ref[0, rows, :] = (y * _sigmoid(y)).astype(o_ref.dtype)


def _conv_branch(u3, w_dw, b_dw, g_ln, b_ln, ts=512, rg=16, rc=128):
    b, s, c = u3.shape
    hb = ts // CONV_HALO
    return pl.pallas_call(
        functools.partial(_conv_kernel, ts=ts, rg=rg, rc=rc),
        grid=(b, s // ts),
        in_specs=[pl.BlockSpec((1, ts, c), lambda bi, si: (bi, si, 0)),
                  pl.BlockSpec((1, CONV_HALO, c), lambda bi, si: (bi, jnp.maximum(si * hb - 1, 0), 0)),
                  pl.BlockSpec((CONV_WIDTH, c), lambda bi, si: (0, 0)),
                  pl.BlockSpec((1, c), lambda bi, si: (0, 0)),
                  pl.BlockSpec((1, c), lambda bi, si: (0, 0)),
                  pl.BlockSpec((1, c), lambda bi, si: (0, 0))],
        out_specs=pl.BlockSpec((1, ts, c), lambda bi, si: (bi, si, 0)),
        out_shape=jax.ShapeDtypeStruct((b, s, c), BF16),
        scratch_shapes=[pltpu.VMEM((CONV_HALO + ts + SUBLANES, c), F32),
                        pltpu.VMEM((SUBLANES - 1, CONV_HALO + ts, c), F32),
                        pltpu.VMEM((ts, c), F32)],
        compiler_params=_cparams(2),
        name="conv_branch",
    )(u3, u3, w_dw, b_dw.reshape(1, c), g_ln.reshape(1, c), b_ln.reshape(1, c))


def _da_tiles(lq1_ref, lk1_ref, lq2_ref, lk2_ref, g_ref, q_ref, k_ref, v_ref, o_ref, *, tq, tiles):
    hd = DA_HEAD_DIM
    nt = (((1,), (1,)), ((), ()))
    lam = (jnp.exp(jnp.sum(lq1_ref[...] * lk1_ref[...], axis=-1, keepdims=True))
           - jnp.exp(jnp.sum(lq2_ref[...] * lk2_ref[...], axis=-1, keepdims=True))
           + LAMBDA_INIT)

    def tile(n, part):
        off = n * tq
        rows = slice(part * tq, (part + 1) * tq)
        q = q_ref[0, rows, :]
        causal = (lax.broadcasted_iota(jnp.int32, (tq, tq), 0)
                  >= lax.broadcasted_iota(jnp.int32, (tq, tq), 1))
        p_off, p_dg, inv_l = [], [], []
        for mp in range(2):
            cs = slice(mp * hd, (mp + 1) * hd)
            sd = lax.dot_general(q[:, cs], k_ref[0, off:off + tq, cs], nt, preferred_element_type=F32)
            sd = jnp.where(causal, sd, -jnp.inf)
            m = jnp.max(sd, axis=-1, keepdims=True)
            if n > 0:
                so = lax.dot_general(q[:, cs], k_ref[0, 0:off, cs], nt, preferred_element_type=F32)
                m = jnp.maximum(m, jnp.max(so, axis=-1, keepdims=True))
                po = jnp.exp2(so - m)
                l = jnp.sum(po, axis=-1, keepdims=True)
                p_off.append(po.astype(BF16))
            else:
                l = 0.0
            pd = jnp.exp2(sd - m)
            l = l + jnp.sum(pd, axis=-1, keepdims=True)
            p_dg.append(pd.astype(BF16))
            inv_l.append(1.0 / l)
        acc = jnp.dot(jnp.concatenate(p_dg, axis=0), v_ref[0, off:off + tq, :], preferred_element_type=F32)
        if n > 0:
            acc = acc + jnp.dot(jnp.concatenate(p_off, axis=0), v_ref[0, 0:off, :],
                                preferred_element_type=F32)
        dlt = acc[:tq] * inv_l[0] - lam * (acc[tq:] * inv_l[1])
        y = _rms_scale(dlt, g_ref[...]) * (1.0 - LAMBDA_INIT)
        o_ref[0, rows, :] = y.astype(o_ref.dtype)

    for n, part in tiles:
        tile(n, part)


def _da_kernel(*refs, tq, nq, tpg):
    if tpg == nq:
        _da_tiles(*refs, tq=tq, tiles=[(n, n) for n in range(nq)])
        return
    qi = pl.program_id(2)
    for gq in range(nq // tpg):
        tiles = [(gq * tpg + part, part) for part in range(tpg)]
        pl.when(qi == gq)(functools.partial(_da_tiles, *refs, tq=tq, tiles=tiles))


def _diff_attention(p3, lq1, lk1, lq2, lk2, g_subln, tq=256, tpg=8):
    b, s, _ = p3.shape
    hw = 2 * DA_HEAD_DIM
    nq = s // tq
    vec = lambda: pl.BlockSpec((1, DA_HEAD_DIM), lambda bi, h, qi: (0, 0))
    return pl.pallas_call(
        functools.partial(_da_kernel, tq=tq, nq=nq, tpg=tpg),
        grid=(b, DA_HEADS, nq // tpg),
        in_specs=[vec(), vec(), vec(), vec(),
                  pl.BlockSpec((1, hw), lambda bi, h, qi: (0, 0)),
                  pl.BlockSpec((1, tpg * tq, hw), lambda bi, h, qi: (bi, qi, Q_OFF // hw + h)),
                  pl.BlockSpec((1, s, hw), lambda bi, h, qi: (bi, 0, K_OFF // hw + h)),
                  pl.BlockSpec((1, s, hw), lambda bi, h, qi: (bi, 0, V_OFF // hw + h))],
        out_specs=pl.BlockSpec((1, tpg * tq, hw), lambda bi, h, qi: (bi, qi, h)),
        out_shape=jax.ShapeDtypeStruct((b, s, DA_WIDTH), BF16),
        compiler_params=_cparams(3),
        name="diff_attention",
    )(lq1.reshape(1, -1), lk1.reshape(1, -1), lq2.reshape(1, -1), lk2.reshape(1, -1),
      g_subln.reshape(1, hw), p3, p3, p3)


def _proj_kernel(a_ref, w_ref, o_ref, wbf_ref):
    @pl.when(pl.program_id(1) == 0)
    def _():
        wbf_ref[...] = w_ref[...].astype(BF16)

    o_ref[...] = jnp.dot(a_ref[...], wbf_ref[...], preferred_element_type=F32).astype(o_ref.dtype)


def _proj(a, w, tm=1024, tn=1024):
    m, k = a.shape
    n = w.shape[1]
    return pl.pallas_call(
        _proj_kernel,
        grid=(n // tn, m // tm),
        in_specs=[pl.BlockSpec((tm, k), lambda j, i: (i, 0)),
                  pl.BlockSpec((k, tn), lambda j, i: (0, j))],
        out_specs=pl.BlockSpec((tm, tn), lambda j, i: (i, j)),
        out_shape=jax.ShapeDtypeStruct((m, n), BF16),
        scratch_shapes=[pltpu.VMEM((k, tn), BF16)],
        compiler_params=_cparams(2),
        name="proj",
    )(a, w)


def _xa_kernel(q_ref, k_ref, v_ref, o_ref):
    nt = (((1,), (1,)), ((), ()))
    s = lax.dot_general(q_ref[0], k_ref[0], nt, preferred_element_type=F32)
    m = jnp.max(s, axis=-1, keepdims=True)
    p = jnp.exp2(s - m)
    l = jnp.sum(p, axis=-1, keepdims=True)
    o = jnp.dot(p.astype(BF16), v_ref[0], preferred_element_type=F32)
    o_ref[0] = (o * (1.0 / l)).astype(o_ref.dtype)


def _cross_attention(p3, kv3):
    b, s, _ = p3.shape
    mem_len = kv3.shape[1]
    hd = XA_HEAD_DIM
    return pl.pallas_call(
        _xa_kernel,
        grid=(b, XA_HEADS),
        in_specs=[pl.BlockSpec((1, s, hd), lambda bi, h: (bi, 0, XQ_OFF // hd + h)),
                  pl.BlockSpec((1, mem_len, hd), lambda bi, h: (bi, 0, h)),
                  pl.BlockSpec((1, mem_len, hd), lambda bi, h: (bi, 0, XA_HEADS + h))],
        out_specs=pl.BlockSpec((1, s, hd), lambda bi, h: (bi, 0, h)),
        out_shape=jax.ShapeDtypeStruct((b, s, XA_WIDTH), BF16),
        compiler_params=_cparams(2),
        name="cross_attention",
    )(p3, kv3, kv3)


def _merge_kernel(oc_ref, od_ref, ox_ref, g0_ref, g1_ref, g2_ref, wc_ref, wd_ref, wx_ref, o_ref, wbf_ref):
    @pl.when(pl.program_id(1) == 0)
    def _():
        wbf_ref[0] = wc_ref[...].astype(BF16)
        wbf_ref[1] = wd_ref[...].astype(BF16)
        wbf_ref[2] = wx_ref[...].astype(BF16)

    y = g0_ref[...].astype(F32) * jnp.dot(oc_ref[...], wbf_ref[0], preferred_element_type=F32)
    y = y + g1_ref[...].astype(F32) * jnp.dot(od_ref[...], wbf_ref[1], preferred_element_type=F32)
    y = y + g2_ref[...].astype(F32) * jnp.dot(ox_ref[...], wbf_ref[2], preferred_element_type=F32)
    o_ref[...] = y.astype(o_ref.dtype)


def _merge(o_conv, o_da, o_xa, p2, w_conv_out, w_da_out, w_xa_out, tm=1024, tn=1024):
    t, kk = o_conv.shape
    gcol = GATE_OFF // tn
    gstride = D_MODEL // tn
    act = lambda: pl.BlockSpec((tm, kk), lambda j, i: (i, 0))
    wsp = lambda: pl.BlockSpec((kk, tn), lambda j, i: (0, j), pipeline_mode=pl.Buffered(1))
    gate = lambda br: pl.BlockSpec((tm, tn), lambda j, i: (i, gcol + br * gstride + j))
    return pl.pallas_call(
        _merge_kernel,
        grid=(D_MODEL // tn, t // tm),
        in_specs=[act(), act(), act(), gate(0), gate(1), gate(2), wsp(), wsp(), wsp()],
        out_specs=pl.BlockSpec((tm, tn), lambda j, i: (i, j)),
        out_shape=jax.ShapeDtypeStruct((t, D_MODEL), BF16),
        scratch_shapes=[pltpu.VMEM((3, kk, tn), BF16)],
        compiler_params=_cparams(2),
        name="merge",
    )(o_conv, o_da, o_xa, p2, p2, p2, w_conv_out, w_da_out, w_xa_out)


def _mix_kernel(a_ref, w_ref, x_ref, g_ref, x1_ref, h2_ref, wbf_ref):
    @pl.when(pl.program_id(0) == 0)
    def _():
        wbf_ref[...] = w_ref[...].astype(BF16)

    x1 = x_ref[...] + jnp.dot(a_ref[...], wbf_ref[...], preferred_element_type=F32)
    x1_ref[...] = x1
    h2_ref[...] = _rms_scale(x1, g_ref[...]).astype(h2_ref.dtype)


def _mix(merged, w_mix, x2d, g_mlp, tm=512):
    t, k = merged.shape
    n = w_mix.shape[1]
    return pl.pallas_call(
        _mix_kernel,
        grid=(t // tm,),
        in_specs=[pl.BlockSpec((tm, k), lambda i: (i, 0)),
                  pl.BlockSpec((k, n), lambda i: (0, 0), pipeline_mode=pl.Buffered(1)),
                  pl.BlockSpec((tm, n), lambda i: (i, 0)),
                  pl.BlockSpec((1, n), lambda i: (0, 0))],
        out_specs=[pl.BlockSpec((tm, n), lambda i: (i, 0)),
                   pl.BlockSpec((tm, n), lambda i: (i, 0))],
        out_shape=[jax.ShapeDtypeStruct((t, n), F32),
                   jax.ShapeDtypeStruct((t, n), BF16)],
        scratch_shapes=[pltpu.VMEM((k, n), BF16)],
        compiler_params=_cparams(1),
        name="mix_out",
    )(merged, w_mix, x2d, g_mlp.reshape(1, n))


def _mlp_kernel(h_ref, x_hbm, wu_ref, wd_ref, gf_ref, o_ref, xbuf_ref, xsem, *, nf, tm):
    i = pl.program_id(0)
    f = pl.program_id(1)

    def residual_copy():
        rows = pl.ds(pl.multiple_of(i * tm, tm), tm)
        return pltpu.make_async_copy(x_hbm.at[rows, :], xbuf_ref, xsem)

    @pl.when(f == 0)
    def _():
        o_ref[...] = jnp.zeros_like(o_ref)

    @pl.when(f == nf - 2)
    def _():
        residual_copy().start()

    hm = jnp.dot(h_ref[...], wu_ref[...].astype(BF16), preferred_element_type=F32)
    hm = jnp.square(jnp.maximum(hm, 0.0)).astype(BF16)
    o_ref[...] += jnp.dot(hm, wd_ref[...].astype(BF16), preferred_element_type=F32)

    @pl.when(f == nf - 1)
    def _():
        residual_copy().wait()
        o_ref[...] = _rms_scale(xbuf_ref[...] + o_ref[...], gf_ref[...])


def _mlp(h2, x1, w_up, w_down, g_final, tm=1024, tf=512):
    t, d = x1.shape
    ff = w_up.shape[1]
    nf = ff // tf
    assert nf >= 2
    return pl.pallas_call(
        functools.partial(_mlp_kernel, nf=nf, tm=tm),
        grid=(t // tm, nf),
        in_specs=[pl.BlockSpec((tm, d), lambda i, f: (i, 0)),
                  pl.BlockSpec(memory_space=pl.ANY),
                  pl.BlockSpec((d, tf), lambda i, f: (0, f)),
                  pl.BlockSpec((tf, d), lambda i, f: (f, 0)),
                  pl.BlockSpec((1, d), lambda i, f: (0, 0))],
        out_specs=pl.BlockSpec((tm, d), lambda i, f: (i, 0)),
        out_shape=jax.ShapeDtypeStruct((t, d), F32),
        scratch_shapes=[pltpu.VMEM((tm, d), F32), pltpu.SemaphoreType.DMA(())],
        compiler_params=_cparams(2),
        name="mlp",
    )(h2, x1, w_up, w_down, g_final.reshape(1, d))


def kernel(x, mem, positions, g_mix, w_in, w_dw, b_dw, g_conv_ln, b_conv_ln, w_conv_out, lambda_q1, lambda_k1, lambda_q2, lambda_k2, g_subln, w_da_out, g_mem, w_mem_kv, w_xa_out, w_mix_out, g_mlp, w_up, w_down, g_final):
    bsz, seq, d = x.shape
    t = bsz * seq
    x2d = x.reshape(t, d)

    h = _rmsnorm(x2d, g_mix[0], BF16)
    cos2, sin2 = _rope_tables(positions.astype(F32).reshape(t, 1))
    u, p2 = _in_proj(h, w_in[0], cos2, sin2)
    p3 = p2.reshape(bsz, seq, P_COLS)

    o_conv = _conv_branch(u.reshape(bsz, seq, CONV_CH), w_dw[0], b_dw[0], g_conv_ln[0], b_conv_ln[0])
    o_da = _diff_attention(p3, lambda_q1[0], lambda_k1[0], lambda_q2[0], lambda_k2[0], g_subln[0])

    mem_n = _rmsnorm(mem.reshape(-1, d), g_mem[0], BF16)
    kv = _proj(mem_n, w_mem_kv[0])
    o_xa = _cross_attention(p3, kv.reshape(bsz, -1, 2 * XA_WIDTH))

    merged = _merge(o_conv.reshape(t, CONV_CH), o_da.reshape(t, DA_WIDTH), o_xa.reshape(t, XA_WIDTH),
                    p2, w_conv_out[0], w_da_out[0], w_xa_out[0])
    x1, h2 = _mix(merged, w_mix_out[0], x2d, g_mlp[0])
    out = _mlp(h2, x1, w_up[0], w_down[0], g_final)
    return out.reshape(bsz, seq, d)
```

```python
import functools
import math

import jax
import jax.numpy as jnp
from jax import lax
from jax.experimental import pallas as pl
from jax.experimental.pallas import tpu as pltpu

F32 = jnp.float32
BF16 = jnp.bfloat16

D_MODEL = 2048
CONV_CH = 1024
CONV_WIDTH = 31
DA_HEADS = 4
DA_HEAD_DIM = 128
DA_WIDTH = 1024
XA_HEADS = 4
XA_HEAD_DIM = 256
XA_WIDTH = 1024
D_FF = 4 * D_MODEL
ROPE_THETA = 10000.0
NORM_EPS = 1e-6
LAMBDA_INIT = 0.8 - 0.6 * math.exp(-0.3 * 0)
LOG2E = math.log2(math.e)

IN_TILE = 1024
IN_HALF = IN_TILE // 2
N_GLU_TILES = CONV_CH // IN_HALF
P_COLS = 2 * 1024 + 1024 + XA_WIDTH + 3 * D_MODEL
Q_OFF, K_OFF, V_OFF, XQ_OFF, GATE_OFF = 0, 1024, 2048, 3072, 4096
MLP_CAST_CHUNKS = 64

V7X_VMEM_LIMIT = 62 * 1024 * 1024
CONV_HALO = 32
SUBLANES = 8
LANES = 128


def _cparams(n_axes):
    return pltpu.CompilerParams(
        dimension_semantics=("arbitrary",) * n_axes,
        vmem_limit_bytes=V7X_VMEM_LIMIT,
    )


def _sigmoid(x):
    return 0.5 * jnp.tanh(0.5 * x) + 0.5


def _rms_scale(x, g):
    ms = jnp.mean(x * x, axis=-1, keepdims=True)
    return x * lax.rsqrt(ms + NORM_EPS) * g


def _rmsnorm_kernel(x_ref, g_ref, o_ref):
    o_ref[...] = _rms_scale(x_ref[...], g_ref[...]).astype(o_ref.dtype)


def _rmsnorm(x2d, g, out_dtype, tm=512):
    m, d = x2d.shape
    return pl.pallas_call(
        _rmsnorm_kernel,
        grid=(m // tm,),
        in_specs=[pl.BlockSpec((tm, d), lambda i: (i, 0)),
                  pl.BlockSpec((1, d), lambda i: (0, 0))],
        out_specs=pl.BlockSpec((tm, d), lambda i: (i, 0)),
        out_shape=jax.ShapeDtypeStruct((m, d), out_dtype),
        compiler_params=_cparams(1),
        name="rmsnorm",
    )(x2d, g.reshape(1, d))


def _rope_kernel(pos_lo_ref, pos_hi_ref, f_ref, cos_ref, sin_ref, *, half_rows):
    half = DA_HEAD_DIM // 2
    lane = lax.broadcasted_iota(jnp.int32, (half_rows, DA_HEAD_DIM), 1)
    left = lane < half
    ang = jnp.where(left, pos_lo_ref[...], pos_hi_ref[...]) * f_ref[...]
    c = jnp.cos(ang)
    s = jnp.sin(ang)
    c_sw = pltpu.roll(c, half, axis=1)
    s_sw = pltpu.roll(s, half, axis=1)
    cos_ref[0:half_rows, :] = jnp.where(left, c, c_sw)
    cos_ref[half_rows:, :] = jnp.where(left, c_sw, c)
    sin_ref[0:half_rows, :] = jnp.where(left, -s, s_sw)
    sin_ref[half_rows:, :] = jnp.where(left, -s_sw, s)


def _rope_tables(pos_col, tm=1024):
    t = pos_col.shape[0]
    inv_freq = 1.0 / (ROPE_THETA ** (jnp.arange(0, DA_HEAD_DIM, 2, dtype=F32) / DA_HEAD_DIM))
    f2 = jnp.concatenate([inv_freq, inv_freq]).reshape(1, DA_HEAD_DIM)
    half_rows = tm // 2
    return pl.pallas_call(
        functools.partial(_rope_kernel, half_rows=half_rows),
        grid=(t // tm,),
        in_specs=[pl.BlockSpec((half_rows, 1), lambda i: (2 * i, 0)),
                  pl.BlockSpec((half_rows, 1), lambda i: (2 * i + 1, 0)),
                  pl.BlockSpec((1, DA_HEAD_DIM), lambda i: (0, 0))],
        out_specs=[pl.BlockSpec((tm, DA_HEAD_DIM), lambda i: (i, 0))] * 2,
        out_shape=[jax.ShapeDtypeStruct((t, DA_HEAD_DIM), F32)] * 2,
        compiler_params=_cparams(1),
        name="rope_tables",
    )(pos_col, pos_col, f2)


def _in_proj_kernel(h_ref, wa_ref, wb_ref, cos_ref, sin_ref, wu_src_ref, wd_src_ref,
                    u_ref, p_ref, wu_bf_ref, wd_bf_ref, wbf_ref, *, n_i):
    j = pl.program_id(0)
    n = j - N_GLU_TILES

    @pl.when(j * n_i + pl.program_id(1) < MLP_CAST_CHUNKS)
    def _():
        wu_bf_ref[...] = wu_src_ref[...].astype(BF16)
        wd_bf_ref[...] = wd_src_ref[...].astype(BF16)

    @pl.when(pl.program_id(1) == 0)
    def _():
        wbf_ref[:, :IN_HALF] = wa_ref[...].astype(BF16)
        wbf_ref[:, IN_HALF:] = wb_ref[...].astype(BF16)

    def proj():
        return jnp.dot(h_ref[...], wbf_ref[...], preferred_element_type=F32)

    @pl.when(j < N_GLU_TILES)
    def _():
        acc = proj()
        u_ref[...] = (acc[:, :IN_HALF] * _sigmoid(acc[:, IN_HALF:])).astype(u_ref.dtype)

    @pl.when((n >= 0) & (n < 2))
    def _():
        acc = proj()
        scale = jnp.where(n == 0, DA_HEAD_DIM ** -0.5 * LOG2E, 1.0).astype(F32)
        c = cos_ref[...] * scale
        s = sin_ref[...] * scale
        for g in range(IN_TILE // DA_HEAD_DIM):
            sl = slice(g * DA_HEAD_DIM, (g + 1) * DA_HEAD_DIM)
            t = acc[:, sl]
            r = pltpu.roll(t, DA_HEAD_DIM // 2, axis=1)
            p_ref[:, sl] = (t * c + r * s).astype(p_ref.dtype)

    @pl.when((n >= 2) & (n < 4))
    def _():
        scale = jnp.where(n == 3, XA_HEAD_DIM ** -0.5 * LOG2E, 1.0).astype(F32)
        p_ref[...] = (proj() * scale).astype(p_ref.dtype)

    @pl.when(n >= 4)
    def _():
        p_ref[...] = _sigmoid(proj()).astype(p_ref.dtype)


def _in_proj(h, w_in, cos2, sin2, w_up, w_down, tm=1024):
    t, d = h.shape
    n_i = t // tm
    glu = lambda j: j < N_GLU_TILES
    wa_map = lambda j, i: (0, jnp.where(glu(j), j, 2 * j))
    wb_map = lambda j, i: (0, jnp.where(glu(j), j + N_GLU_TILES, 2 * j + 1))
    u_map = lambda j, i: (jnp.where(glu(j), i, n_i - 1), jnp.where(glu(j), j, N_GLU_TILES - 1))
    p_map = lambda j, i: (jnp.where(glu(j), 0, i), jnp.where(glu(j), 0, j - N_GLU_TILES))
    n_steps = (N_GLU_TILES + P_COLS // IN_TILE) * n_i
    assert n_steps >= MLP_CAST_CHUNKS
    chunk_map = lambda j, i: (jnp.minimum(j * n_i + i, MLP_CAST_CHUNKS - 1), 0)
    up_rows = w_up.shape[0] // MLP_CAST_CHUNKS
    dn_rows = w_down.shape[0] // MLP_CAST_CHUNKS
    return pl.pallas_call(
        functools.partial(_in_proj_kernel, n_i=n_i),
        grid=(N_GLU_TILES + P_COLS // IN_TILE, n_i),
        in_specs=[pl.BlockSpec((tm, d), lambda j, i: (i, 0)),
                  pl.BlockSpec((d, IN_HALF), wa_map),
                  pl.BlockSpec((d, IN_HALF), wb_map),
                  pl.BlockSpec((tm, DA_HEAD_DIM), lambda j, i: (i, 0)),
                  pl.BlockSpec((tm, DA_HEAD_DIM), lambda j, i: (i, 0)),
                  pl.BlockSpec((up_rows, w_up.shape[1]), chunk_map),
                  pl.BlockSpec((dn_rows, w_down.shape[1]), chunk_map)],
        out_specs=[pl.BlockSpec((tm, IN_HALF), u_map),
                   pl.BlockSpec((tm, IN_TILE), p_map),
                   pl.BlockSpec((up_rows, w_up.shape[1]), chunk_map),
                   pl.BlockSpec((dn_rows, w_down.shape[1]), chunk_map)],
        out_shape=[jax.ShapeDtypeStruct((t, CONV_CH), BF16),
                   jax.ShapeDtypeStruct((t, P_COLS), BF16),
                   jax.ShapeDtypeStruct(w_up.shape, BF16),
                   jax.ShapeDtypeStruct(w_down.shape, BF16)],
        scratch_shapes=[pltpu.VMEM((d, IN_TILE), BF16)],
        compiler_params=_cparams(2),
        name="in_proj",
    )(h, w_in, w_in, cos2, sin2, w_up, w_down)


def _conv_kernel(ucur_ref, uprev_ref, w_ref, bdw_ref, g_ref, b_ref, o_ref, buf_ref, sh_ref, y_ref, *, ts, rg, rc):
    si = pl.program_id(1)
    prev = uprev_ref[0].astype(F32)
    buf_ref[0:CONV_HALO, :] = jnp.where(si > 0, prev, 0.0)
    buf_ref[CONV_HALO:CONV_HALO + ts, :] = ucur_ref[0].astype(F32)
    buf_ref[CONV_HALO + ts:, :] = jnp.zeros((SUBLANES, CONV_CH), F32)
    n_sh = ts + CONV_HALO
    for r in range(1, SUBLANES):
        sh_ref[r - 1] = buf_ref[pl.ds(r, n_sh), :]

    lead = CONV_HALO - (CONV_WIDTH - 1)
    rows_per_iter = rg * SUBLANES

    for c in range(CONV_CH // LANES):
        lanes = slice(c * LANES, (c + 1) * LANES)
        taps = [jnp.broadcast_to(w_ref[k:k + 1, lanes], (SUBLANES, LANES)) for k in range(CONV_WIDTH)]
        bias = jnp.broadcast_to(bdw_ref[:, lanes], (SUBLANES, LANES))

        def rows_body(it, carry, lanes=lanes, taps=taps, bias=bias):
            r0 = pl.multiple_of(it * rows_per_iter, rows_per_iter)
            for gi in range(rg):
                acc = bias
                for k in range(CONV_WIDTH):
                    a, r = divmod(lead + k, SUBLANES)
                    row0 = pl.multiple_of(r0 + SUBLANES * (a + gi), SUBLANES)
                    if r == 0:
                        rows = buf_ref[pl.ds(row0, SUBLANES), lanes]
                    else:
                        rows = sh_ref[r - 1, pl.ds(row0, SUBLANES), lanes]
                    acc = acc + rows * taps[k]
                y_ref[pl.ds(pl.multiple_of(r0 + SUBLANES * gi, SUBLANES), SUBLANES), lanes] = acc
            return carry

        lax.fori_loop(0, ts // rows_per_iter, rows_body, 0)

    for ci in range(ts // rc):
        rows = slice(ci * rc, (ci + 1) * rc)
        acc = y_ref[rows, :]
        mu = jnp.mean(acc, axis=-1, keepdims=True)
        cen = acc - mu
        var = jnp.mean(cen * cen, axis=-1, keepdims=True)
        y = cen * lax.rsqrt(var + NORM_EPS) * g_ref[...] + b_ref[...]
        o_ref[0, rows, :] = (y * _sigmoid(y)).astype(o_ref.dtype)


def _conv_branch(u3, w_dw, b_dw, g_ln, b_ln, ts=512, rg=16, rc=128):
    b, s, c = u3.shape
    hb = ts // CONV_HALO
    return pl.pallas_call(
        functools.partial(_conv_kernel, ts=ts, rg=rg, rc=rc),
        grid=(b, s // ts),
        in_specs=[pl.BlockSpec((1, ts, c), lambda bi, si: (bi, si, 0)),
                  pl.BlockSpec((1, CONV_HALO, c), lambda bi, si: (bi, jnp.maximum(si * hb - 1, 0), 0)),
                  pl.BlockSpec((CONV_WIDTH, c), lambda bi, si: (0, 0)),
                  pl.BlockSpec((1, c), lambda bi, si: (0, 0)),
                  pl.BlockSpec((1, c), lambda bi, si: (0, 0)),
                  pl.BlockSpec((1, c), lambda bi, si: (0, 0))],
        out_specs=pl.BlockSpec((1, ts, c), lambda bi, si: (bi, si, 0)),
        out_shape=jax.ShapeDtypeStruct((b, s, c), BF16),
        scratch_shapes=[pltpu.VMEM((CONV_HALO + ts + SUBLANES, c), F32),
                        pltpu.VMEM((SUBLANES - 1, CONV_HALO + ts, c), F32),
                        pltpu.VMEM((ts, c), F32)],
        compiler_params=_cparams(2),
        name="conv_branch",
    )(u3, u3, w_dw, b_dw.reshape(1, c), g_ln.reshape(1, c), b_ln.reshape(1, c))


def _da_tiles(lq1_ref, lk1_ref, lq2_ref, lk2_ref, g_ref, q_ref, k_ref, v_ref, o_ref, *, tq, tiles):
    hd = DA_HEAD_DIM
    nt = (((1,), (1,)), ((), ()))
    lam = (jnp.exp(jnp.sum(lq1_ref[...] * lk1_ref[...], axis=-1, keepdims=True))
           - jnp.exp(jnp.sum(lq2_ref[...] * lk2_ref[...], axis=-1, keepdims=True))
           + LAMBDA_INIT)

    def tile(n, part):
        off = n * tq
        rows = slice(part * tq, (part + 1) * tq)
        q = q_ref[0, rows, :]
        causal = (lax.broadcasted_iota(jnp.int32, (tq, tq), 0)
                  >= lax.broadcasted_iota(jnp.int32, (tq, tq), 1))
        p_off, p_dg, inv_l = [], [], []
        for mp in range(2):
            cs = slice(mp * hd, (mp + 1) * hd)
            sd = lax.dot_general(q[:, cs], k_ref[0, off:off + tq, cs], nt, preferred_element_type=F32)
            sd = jnp.where(causal, sd, -jnp.inf)
            m = jnp.max(sd, axis=-1, keepdims=True)
            if n > 0:
                so = lax.dot_general(q[:, cs], k_ref[0, 0:off, cs], nt, preferred_element_type=F32)
                m = jnp.maximum(m, jnp.max(so, axis=-1, keepdims=True))
                po = jnp.exp2(so - m)
                l = jnp.sum(po, axis=-1, keepdims=True)
                p_off.append(po.astype(BF16))
            else:
                l = 0.0
            pd = jnp.exp2(sd - m)
            l = l + jnp.sum(pd, axis=-1, keepdims=True)
            p_dg.append(pd.astype(BF16))
            inv_l.append(1.0 / l)
        acc = jnp.dot(jnp.concatenate(p_dg, axis=0), v_ref[0, off:off + tq, :], preferred_element_type=F32)
        if n > 0:
            acc = acc + jnp.dot(jnp.concatenate(p_off, axis=0), v_ref[0, 0:off, :],
                                preferred_element_type=F32)
        dlt = acc[:tq] * inv_l[0] - lam * (acc[tq:] * inv_l[1])
        y = _rms_scale(dlt, g_ref[...]) * (1.0 - LAMBDA_INIT)
        o_ref[0, rows, :] = y.astype(o_ref.dtype)

    for n, part in tiles:
        tile(n, part)


def _da_kernel(*refs, tq, nq, tpg):
    if tpg == nq:
        _da_tiles(*refs, tq=tq, tiles=[(n, n) for n in range(nq)])
        return
    qi = pl.program_id(2)
    for gq in range(nq // tpg):
        tiles = [(gq * tpg + part, part) for part in range(tpg)]
        pl.when(qi == gq)(functools.partial(_da_tiles, *refs, tq=tq, tiles=tiles))


def _diff_attention(p3, lq1, lk1, lq2, lk2, g_subln, tq=256, tpg=8):
    b, s, _ = p3.shape
    hw = 2 * DA_HEAD_DIM
    nq = s // tq
    vec = lambda: pl.BlockSpec((1, DA_HEAD_DIM), lambda bi, h, qi: (0, 0))
    return pl.pallas_call(
        functools.partial(_da_kernel, tq=tq, nq=nq, tpg=tpg),
        grid=(b, DA_HEADS, nq // tpg),
        in_specs=[vec(), vec(), vec(), vec(),
                  pl.BlockSpec((1, hw), lambda bi, h, qi: (0, 0)),
                  pl.BlockSpec((1, tpg * tq, hw), lambda bi, h, qi: (bi, qi, Q_OFF // hw + h)),
                  pl.BlockSpec((1, s, hw), lambda bi, h, qi: (bi, 0, K_OFF // hw + h)),
                  pl.BlockSpec((1, s, hw), lambda bi, h, qi: (bi, 0, V_OFF // hw + h))],
        out_specs=pl.BlockSpec((1, tpg * tq, hw), lambda bi, h, qi: (bi, qi, h)),
        out_shape=jax.ShapeDtypeStruct((b, s, DA_WIDTH), BF16),
        compiler_params=_cparams(3),
        name="diff_attention",
    )(lq1.reshape(1, -1), lk1.reshape(1, -1), lq2.reshape(1, -1), lk2.reshape(1, -1),
      g_subln.reshape(1, hw), p3, p3, p3)


def _proj_kernel(a_ref, w_ref, o_ref, wbf_ref):
    @pl.when(pl.program_id(1) == 0)
    def _():
        wbf_ref[...] = w_ref[...].astype(BF16)

    o_ref[...] = jnp.dot(a_ref[...], wbf_ref[...], preferred_element_type=F32).astype(o_ref.dtype)


def _proj(a, w, tm=1024, tn=1024):
    m, k = a.shape
    n = w.shape[1]
    return pl.pallas_call(
        _proj_kernel,
        grid=(n // tn, m // tm),
        in_specs=[pl.BlockSpec((tm, k), lambda j, i: (i, 0)),
                  pl.BlockSpec((k, tn), lambda j, i: (0, j))],
        out_specs=pl.BlockSpec((tm, tn), lambda j, i: (i, j)),
        out_shape=jax.ShapeDtypeStruct((m, n), BF16),
        scratch_shapes=[pltpu.VMEM((k, tn), BF16)],
        compiler_params=_cparams(2),
        name="proj",
    )(a, w)


def _xa_kernel(q_ref, k_ref, v_ref, o_ref):
    nt = (((1,), (1,)), ((), ()))
    s = lax.dot_general(q_ref[0], k_ref[0], nt, preferred_element_type=F32)
    m = jnp.max(s, axis=-1, keepdims=True)
    p = jnp.exp2(s - m)
    l = jnp.sum(p, axis=-1, keepdims=True)
    o = jnp.dot(p.astype(BF16), v_ref[0], preferred_element_type=F32)
    o_ref[0] = (o * (1.0 / l)).astype(o_ref.dtype)


def _cross_attention(p3, kv3):
    b, s, _ = p3.shape
    mem_len = kv3.shape[1]
    hd = XA_HEAD_DIM
    return pl.pallas_call(
        _xa_kernel,
        grid=(b, XA_HEADS),
        in_specs=[pl.BlockSpec((1, s, hd), lambda bi, h: (bi, 0, XQ_OFF // hd + h)),
                  pl.BlockSpec((1, mem_len, hd), lambda bi, h: (bi, 0, h)),
                  pl.BlockSpec((1, mem_len, hd), lambda bi, h: (bi, 0, XA_HEADS + h))],
        out_specs=pl.BlockSpec((1, s, hd), lambda bi, h: (bi, 0, h)),
        out_shape=jax.ShapeDtypeStruct((b, s, XA_WIDTH), BF16),
        compiler_params=_cparams(2),
        name="cross_attention",
    )(p3, kv3, kv3)


def _merge_kernel(oc_ref, od_ref, ox_ref, g0_ref, g1_ref, g2_ref, wc_ref, wd_ref, wx_ref, o_ref, wbf_ref):
    @pl.when(pl.program_id(1) == 0)
    def _():
        wbf_ref[0] = wc_ref[...].astype(BF16)
        wbf_ref[1] = wd_ref[...].astype(BF16)
        wbf_ref[2] = wx_ref[...].astype(BF16)

    y = g0_ref[...].astype(F32) * jnp.dot(oc_ref[...], wbf_ref[0], preferred_element_type=F32)
    y = y + g1_ref[...].astype(F32) * jnp.dot(od_ref[...], wbf_ref[1], preferred_element_type=F32)
    y = y + g2_ref[...].astype(F32) * jnp.dot(ox_ref[...], wbf_ref[2], preferred_element_type=F32)
    o_ref[...] = y.astype(o_ref.dtype)


def _merge(o_conv, o_da, o_xa, p2, w_conv_out, w_da_out, w_xa_out, tm=1024, tn=1024):
    t, kk = o_conv.shape
    gcol = GATE_OFF // tn
    gstride = D_MODEL // tn
    act = lambda: pl.BlockSpec((tm, kk), lambda j, i: (i, 0))
    wsp = lambda: pl.BlockSpec((kk, tn), lambda j, i: (0, j), pipeline_mode=pl.Buffered(1))
    gate = lambda br: pl.BlockSpec((tm, tn), lambda j, i: (i, gcol + br * gstride + j))
    return pl.pallas_call(
        _merge_kernel,
        grid=(D_MODEL // tn, t // tm),
        in_specs=[act(), act(), act(), gate(0), gate(1), gate(2), wsp(), wsp(), wsp()],
        out_specs=pl.BlockSpec((tm, tn), lambda j, i: (i, j)),
        out_shape=jax.ShapeDtypeStruct((t, D_MODEL), BF16),
        scratch_shapes=[pltpu.VMEM((3, kk, tn), BF16)],
        compiler_params=_cparams(2),
        name="merge",
    )(o_conv, o_da, o_xa, p2, p2, p2, w_conv_out, w_da_out, w_xa_out)


def _mix_kernel(a_ref, w_ref, x_ref, g_ref, x1_ref, h2_ref, wbf_ref):
    @pl.when(pl.program_id(0) == 0)
    def _():
        wbf_ref[...] = w_ref[...].astype(BF16)

    x1 = x_ref[...] + jnp.dot(a_ref[...], wbf_ref[...], preferred_element_type=F32)
    x1_ref[...] = x1
    h2_ref[...] = _rms_scale(x1, g_ref[...]).astype(h2_ref.dtype)


def _mix(merged, w_mix, x2d, g_mlp, tm=512):
    t, k = merged.shape
    n = w_mix.shape[1]
    return pl.pallas_call(
        _mix_kernel,
        grid=(t // tm,),
        in_specs=[pl.BlockSpec((tm, k), lambda i: (i, 0)),
                  pl.BlockSpec((k, n), lambda i: (0, 0), pipeline_mode=pl.Buffered(1)),
                  pl.BlockSpec((tm, n), lambda i: (i, 0)),
                  pl.BlockSpec((1, n), lambda i: (0, 0))],
        out_specs=[pl.BlockSpec((tm, n), lambda i: (i, 0)),
                   pl.BlockSpec((tm, n), lambda i: (i, 0))],
        out_shape=[jax.ShapeDtypeStruct((t, n), F32),
                   jax.ShapeDtypeStruct((t, n), BF16)],
        scratch_shapes=[pltpu.VMEM((k, n), BF16)],
        compiler_params=_cparams(1),
        name="mix_out",
    )(merged, w_mix, x2d, g_mlp.reshape(1, n))


def _mlp_kernel(h_ref, x_hbm, wu_ref, wd_ref, gf_ref, o_ref, xbuf_ref, xsem, *, nf, tm):
    i = pl.program_id(0)
    f = pl.program_id(1)

    def residual_copy():
        rows = pl.ds(pl.multiple_of(i * tm, tm), tm)
        return pltpu.make_async_copy(x_hbm.at[rows, :], xbuf_ref, xsem)

    @pl.when(f == 0)
    def _():
        o_ref[...] = jnp.zeros_like(o_ref)

    @pl.when(f == nf - 2)
    def _():
        residual_copy().start()

    hm = jnp.dot(h_ref[...], wu_ref[...], preferred_element_type=F32)
    hm = jnp.square(jnp.maximum(hm, 0.0)).astype(BF16)
    o_ref[...] += jnp.dot(hm, wd_ref[...], preferred_element_type=F32)

    @pl.when(f == nf - 1)
    def _():
        residual_copy().wait()
        o_ref[...] = _rms_scale(xbuf_ref[...] + o_ref[...], gf_ref[...])


def _mlp(h2, x1, w_up, w_down, g_final, tm=1024, tf=1024):
    t, d = x1.shape
    ff = w_up.shape[1]
    nf = ff // tf
    assert nf >= 2
    return pl.pallas_call(
        functools.partial(_mlp_kernel, nf=nf, tm=tm),
        grid=(t // tm, nf),
        in_specs=[pl.BlockSpec((tm, d), lambda i, f: (i, 0)),
                  pl.BlockSpec(memory_space=pl.ANY),
                  pl.BlockSpec((d, tf), lambda i, f: (0, f)),
                  pl.BlockSpec((tf, d), lambda i, f: (f, 0)),
                  pl.BlockSpec((1, d), lambda i, f: (0, 0))],
        out_specs=pl.BlockSpec((tm, d), lambda i, f: (i, 0)),
        out_shape=jax.ShapeDtypeStruct((t, d), F32),
        scratch_shapes=[pltpu.VMEM((tm, d), F32), pltpu.SemaphoreType.DMA(())],
        compiler_params=_cparams(2),
        name="mlp",
    )(h2, x1, w_up, w_down, g_final.reshape(1, d))


def kernel(x, mem, positions, g_mix, w_in, w_dw, b_dw, g_conv_ln, b_conv_ln, w_conv_out, lambda_q1, lambda_k1, lambda_q2, lambda_k2, g_subln, w_da_out, g_mem, w_mem_kv, w_xa_out, w_mix_out, g_mlp, w_up, w_down, g_final):
    bsz, seq, d = x.shape
    t = bsz * seq
    x2d = x.reshape(t, d)

    h = _rmsnorm(x2d, g_mix[0], BF16)
    cos2, sin2 = _rope_tables(positions.astype(F32).reshape(t, 1))
    u, p2, w_up_bf, w_down_bf = _in_proj(h, w_in[0], cos2, sin2, w_up[0], w_down[0])
    p3 = p2.reshape(bsz, seq, P_COLS)

    o_conv = _conv_branch(u.reshape(bsz, seq, CONV_CH), w_dw[0], b_dw[0], g_conv_ln[0], b_conv_ln[0])
    o_da = _diff_attention(p3, lambda_q1[0], lambda_k1[0], lambda_q2[0], lambda_k2[0], g_subln[0])

    mem_n = _rmsnorm(mem.reshape(-1, d), g_mem[0], BF16)
    kv = _proj(mem_n, w_mem_kv[0])
    o_xa = _cross_attention(p3, kv.reshape(bsz, -1, 2 * XA_WIDTH))

    merged = _merge(o_conv.reshape(t, CONV_CH), o_da.reshape(t, DA_WIDTH), o_xa.reshape(t, XA_WIDTH),
                    p2, w_conv_out[0], w_da_out[0], w_xa_out[0])
    x1, h2 = _mix(merged, w_mix_out[0], x2d, g_mlp[0])
    out = _mlp(h2, x1, w_up_bf, w_down_bf, g_final)
    return out.reshape(bsz, seq, d)
```

```python
import functools
import math

import jax
import jax.numpy as jnp
from jax import lax
from jax.experimental import pallas as pl
from jax.experimental.pallas import tpu as pltpu

F32 = jnp.float32
BF16 = jnp.bfloat16

D_MODEL = 2048
CONV_CH = 1024
CONV_WIDTH = 31
DA_HEADS = 4
DA_HEAD_DIM = 128
DA_WIDTH = 1024
XA_HEADS = 4
XA_HEAD_DIM = 256
XA_WIDTH = 1024
D_FF = 4 * D_MODEL
ROPE_THETA = 10000.0
NORM_EPS = 1e-6
LAMBDA_INIT = 0.8 - 0.6 * math.exp(-0.3 * 0)
LOG2E = math.log2(math.e)

IN_TILE = 1024
IN_HALF = IN_TILE // 2
N_GLU_TILES = CONV_CH // IN_HALF
P_COLS = 2 * 1024 + 1024 + XA_WIDTH + 3 * D_MODEL
Q_OFF, K_OFF, V_OFF, XQ_OFF, GATE_OFF = 0, 1024, 2048, 3072, 4096
MLP_CAST_CHUNKS = 64

V7X_VMEM_LIMIT = 62 * 1024 * 1024
CONV_HALO = 32
SUBLANES = 8
LANES = 128


def _cparams(n_axes):
    return pltpu.CompilerParams(
        dimension_semantics=("arbitrary",) * n_axes,
        vmem_limit_bytes=V7X_VMEM_LIMIT,
    )


def _sigmoid(x):
    return 0.5 * jnp.tanh(0.5 * x) + 0.5


def _rms_scale(x, g):
    ms = jnp.mean(x * x, axis=-1, keepdims=True)
    return x * lax.rsqrt(ms + NORM_EPS) * g


def _rmsnorm_kernel(x_ref, g_ref, o_ref):
    o_ref[...] = _rms_scale(x_ref[...], g_ref[...]).astype(o_ref.dtype)


def _rmsnorm(x2d, g, out_dtype, tm=512):
    m, d = x2d.shape
    return pl.pallas_call(
        _rmsnorm_kernel,
        grid=(m // tm,),
        in_specs=[pl.BlockSpec((tm, d), lambda i: (i, 0)),
                  pl.BlockSpec((1, d), lambda i: (0, 0))],
        out_specs=pl.BlockSpec((tm, d), lambda i: (i, 0)),
        out_shape=jax.ShapeDtypeStruct((m, d), out_dtype),
        compiler_params=_cparams(1),
        name="rmsnorm",
    )(x2d, g.reshape(1, d))


def _rope_kernel(pos_lo_ref, pos_hi_ref, f_ref, cos_ref, sin_ref, *, half_rows):
    half = DA_HEAD_DIM // 2
    lane = lax.broadcasted_iota(jnp.int32, (half_rows, DA_HEAD_DIM), 1)
    left = lane < half
    ang = jnp.where(left, pos_lo_ref[...], pos_hi_ref[...]) * f_ref[...]
    c = jnp.cos(ang)
    s = jnp.sin(ang)
    c_sw = pltpu.roll(c, half, axis=1)
    s_sw = pltpu.roll(s, half, axis=1)
    cos_ref[0:half_rows, :] = jnp.where(left, c, c_sw)
    cos_ref[half_rows:, :] = jnp.where(left, c_sw, c)
    sin_ref[0:half_rows, :] = jnp.where(left, -s, s_sw)
    sin_ref[half_rows:, :] = jnp.where(left, -s_sw, s)


def _rope_tables(pos_col, tm=1024):
    t = pos_col.shape[0]
    inv_freq = 1.0 / (ROPE_THETA ** (jnp.arange(0, DA_HEAD_DIM, 2, dtype=F32) / DA_HEAD_DIM))
    f2 = jnp.concatenate([inv_freq, inv_freq]).reshape(1, DA_HEAD_DIM)
    half_rows = tm // 2
    return pl.pallas_call(
        functools.partial(_rope_kernel, half_rows=half_rows),
        grid=(t // tm,),
        in_specs=[pl.BlockSpec((half_rows, 1), lambda i: (2 * i, 0)),
                  pl.BlockSpec((half_rows, 1), lambda i: (2 * i + 1, 0)),
                  pl.BlockSpec((1, DA_HEAD_DIM), lambda i: (0, 0))],
        out_specs=[pl.BlockSpec((tm, DA_HEAD_DIM), lambda i: (i, 0))] * 2,
        out_shape=[jax.ShapeDtypeStruct((t, DA_HEAD_DIM), F32)] * 2,
        compiler_params=_cparams(1),
        name="rope_tables",
    )(pos_col, pos_col, f2)


def _in_proj_kernel(h_ref, wa_ref, wb_ref, cos_ref, sin_ref, wu_src_ref, wd_src_ref,
                    u_ref, p_ref, wu_bf_ref, wd_bf_ref, wbf_ref, *, n_i):
    j = pl.program_id(0)
    n = j - N_GLU_TILES

    @pl.when(j * n_i + pl.program_id(1) < MLP_CAST_CHUNKS)
    def _():
        wu_bf_ref[...] = wu_src_ref[...].astype(BF16)
        wd_bf_ref[...] = wd_src_ref[...].astype(BF16)

    @pl.when(pl.program_id(1) == 0)
    def _():
        wbf_ref[:, :IN_HALF] = wa_ref[...].astype(BF16)
        wbf_ref[:, IN_HALF:] = wb_ref[...].astype(BF16)

    def proj():
        return jnp.dot(h_ref[...], wbf_ref[...], preferred_element_type=F32)

    @pl.when(j < N_GLU_TILES)
    def _():
        acc = proj()
        u_ref[...] = (acc[:, :IN_HALF] * _sigmoid(acc[:, IN_HALF:])).astype(u_ref.dtype)

    @pl.when((n >= 0) & (n < 2))
    def _():
        acc = proj()
        scale = jnp.where(n == 0, DA_HEAD_DIM ** -0.5 * LOG2E, 1.0).astype(F32)
        c = cos_ref[...] * scale
        s = sin_ref[...] * scale
        for g in range(IN_TILE // DA_HEAD_DIM):
            sl = slice(g * DA_HEAD_DIM, (g + 1) * DA_HEAD_DIM)
            t = acc[:, sl]
            r = pltpu.roll(t, DA_HEAD_DIM // 2, axis=1)
            p_ref[:, sl] = (t * c + r * s).astype(p_ref.dtype)

    @pl.when((n >= 2) & (n < 4))
    def _():
        scale = jnp.where(n == 3, XA_HEAD_DIM ** -0.5 * LOG2E, 1.0).astype(F32)
        p_ref[...] = (proj() * scale).astype(p_ref.dtype)

    @pl.when(n >= 4)
    def _():
        p_ref[...] = _sigmoid(proj()).astype(p_ref.dtype)


def _in_proj(h, w_in, cos2, sin2, w_up, w_down, tm=1024):
    t, d = h.shape
    n_i = t // tm
    glu = lambda j: j < N_GLU_TILES
    wa_map = lambda j, i: (0, jnp.where(glu(j), j, 2 * j))
    wb_map = lambda j, i: (0, jnp.where(glu(j), j + N_GLU_TILES, 2 * j + 1))
    u_map = lambda j, i: (jnp.where(glu(j), i, n_i - 1), jnp.where(glu(j), j, N_GLU_TILES - 1))
    p_map = lambda j, i: (jnp.where(glu(j), 0, i), jnp.where(glu(j), 0, j - N_GLU_TILES))
    n_steps = (N_GLU_TILES + P_COLS // IN_TILE) * n_i
    assert n_steps >= MLP_CAST_CHUNKS
    chunk_map = lambda j, i: (jnp.minimum(j * n_i + i, MLP_CAST_CHUNKS - 1), 0)
    up_rows = w_up.shape[0] // MLP_CAST_CHUNKS
    dn_rows = w_down.shape[0] // MLP_CAST_CHUNKS
    return pl.pallas_call(
        functools.partial(_in_proj_kernel, n_i=n_i),
        grid=(N_GLU_TILES + P_COLS // IN_TILE, n_i),
        in_specs=[pl.BlockSpec((tm, d), lambda j, i: (i, 0)),
                  pl.BlockSpec((d, IN_HALF), wa_map),
                  pl.BlockSpec((d, IN_HALF), wb_map),
                  pl.BlockSpec((tm, DA_HEAD_DIM), lambda j, i: (i, 0)),
                  pl.BlockSpec((tm, DA_HEAD_DIM), lambda j, i: (i, 0)),
                  pl.BlockSpec((up_rows, w_up.shape[1]), chunk_map),
                  pl.BlockSpec((dn_rows, w_down.shape[1]), chunk_map)],
        out_specs=[pl.BlockSpec((tm, IN_HALF), u_map),
                   pl.BlockSpec((tm, IN_TILE), p_map),
                   pl.BlockSpec((up_rows, w_up.shape[1]), chunk_map),
                   pl.BlockSpec((dn_rows, w_down.shape[1]), chunk_map)],
        out_shape=[jax.ShapeDtypeStruct((t, CONV_CH), BF16),
                   jax.ShapeDtypeStruct((t, P_COLS), BF16),
                   jax.ShapeDtypeStruct(w_up.shape, BF16),
                   jax.ShapeDtypeStruct(w_down.shape, BF16)],
        scratch_shapes=[pltpu.VMEM((d, IN_TILE), BF16)],
        compiler_params=_cparams(2),
        name="in_proj",
    )(h, w_in, w_in, cos2, sin2, w_up, w_down)


def _conv_kernel(ucur_ref, uprev_ref, w_ref, bdw_ref, g_ref, b_ref, o_ref, buf_ref, sh_ref, y_ref, *, ts, rg, rc):
    si = pl.program_id(1)
    prev = uprev_ref[0].astype(F32)
    buf_ref[0:CONV_HALO, :] = jnp.where(si > 0, prev, 0.0)
    buf_ref[CONV_HALO:CONV_HALO + ts, :] = ucur_ref[0].astype(F32)
    buf_ref[CONV_HALO + ts:, :] = jnp.zeros((SUBLANES, CONV_CH), F32)
    n_sh = ts + CONV_HALO
    for r in range(1, SUBLANES):
        sh_ref[r - 1] = buf_ref[pl.ds(r, n_sh), :]

    lead = CONV_HALO - (CONV_WIDTH - 1)
    rows_per_iter = rg * SUBLANES

    for c in range(CONV_CH // LANES):
        lanes = slice(c * LANES, (c + 1) * LANES)
        taps = [jnp.broadcast_to(w_ref[k:k + 1, lanes], (SUBLANES, LANES)) for k in range(CONV_WIDTH)]
        bias = jnp.broadcast_to(bdw_ref[:, lanes], (SUBLANES, LANES))

        def rows_body(it, carry, lanes=lanes, taps=taps, bias=bias):
            r0 = pl.multiple_of(it * rows_per_iter, rows_per_iter)
            for gi in range(rg):
                acc = bias
                for k in range(CONV_WIDTH):
                    a, r = divmod(lead + k, SUBLANES)
                    row0 = pl.multiple_of(r0 + SUBLANES * (a + gi), SUBLANES)
                    if r == 0:
                        rows = buf_ref[pl.ds(row0, SUBLANES), lanes]
                    else:
                        rows = sh_ref[r - 1, pl.ds(row0, SUBLANES), lanes]
                    acc = acc + rows * taps[k]
                y_ref[pl.ds(pl.multiple_of(r0 + SUBLANES * gi, SUBLANES), SUBLANES), lanes] = acc
            return carry

        lax.fori_loop(0, ts // rows_per_iter, rows_body, 0)

    for ci in range(ts // rc):
        rows = slice(ci * rc, (ci + 1) * rc)
        acc = y_ref[rows, :]
        mu = jnp.mean(acc, axis=-1, keepdims=True)
        cen = acc - mu
        var = jnp.mean(cen * cen, axis=-1, keepdims=True)
        y = cen * lax.rsqrt(var + NORM_EPS) * g_ref[...] + b_ref[...]
        o_ref[0, rows, :] = (y * _sigmoid(y)).astype(o_ref.dtype)


def _conv_branch(u3, w_dw, b_dw, g_ln, b_ln, ts=512, rg=16, rc=128):
    b, s, c = u3.shape
    hb = ts // CONV_HALO
    return pl.pallas_call(
        functools.partial(_conv_kernel, ts=ts, rg=rg, rc=rc),
        grid=(b, s // ts),
        in_specs=[pl.BlockSpec((1, ts, c), lambda bi, si: (bi, si, 0)),
                  pl.BlockSpec((1, CONV_HALO, c), lambda bi, si: (bi, jnp.maximum(si * hb - 1, 0), 0)),
                  pl.BlockSpec((CONV_WIDTH, c), lambda bi, si: (0, 0)),
                  pl.BlockSpec((1, c), lambda bi, si: (0, 0)),
                  pl.BlockSpec((1, c), lambda bi, si: (0, 0)),
                  pl.BlockSpec((1, c), lambda bi, si: (0, 0))],
        out_specs=pl.BlockSpec((1, ts, c), lambda bi, si: (bi, si, 0)),
        out_shape=jax.ShapeDtypeStruct((b, s, c), BF16),
        scratch_shapes=[pltpu.VMEM((CONV_HALO + ts + SUBLANES, c), F32),
                        pltpu.VMEM((SUBLANES - 1, CONV_HALO + ts, c), F32),
                        pltpu.VMEM((ts, c), F32)],
        compiler_params=_cparams(2),
        name="conv_branch",
    )(u3, u3, w_dw, b_dw.reshape(1, c), g_ln.reshape(1, c), b_ln.reshape(1, c))


def _da_tiles(lq1_ref, lk1_ref, lq2_ref, lk2_ref, g_ref, q_ref, k_ref, v_ref, o_ref, *, tq, tiles):
    hd = DA_HEAD_DIM
    nt = (((1,), (1,)), ((), ()))
    lam = (jnp.exp(jnp.sum(lq1_ref[...] * lk1_ref[...], axis=-1, keepdims=True))
           - jnp.exp(jnp.sum(lq2_ref[...] * lk2_ref[...], axis=-1, keepdims=True))
           + LAMBDA_INIT)

    def tile(n, part):
        off = n * tq
        rows = slice(part * tq, (part + 1) * tq)
        q = q_ref[0, rows, :]
        causal = (lax.broadcasted_iota(jnp.int32, (tq, tq), 0)
                  >= lax.broadcasted_iota(jnp.int32, (tq, tq), 1))
        p_off, p_dg, inv_l = [], [], []
        for mp in range(2):
            cs = slice(mp * hd, (mp + 1) * hd)
            sd = lax.dot_general(q[:, cs], k_ref[0, off:off + tq, cs], nt, preferred_element_type=F32)
            sd = jnp.where(causal, sd, -jnp.inf)
            m = jnp.max(sd, axis=-1, keepdims=True)
            if n > 0:
                so = lax.dot_general(q[:, cs], k_ref[0, 0:off, cs], nt, preferred_element_type=F32)
                m = jnp.maximum(m, jnp.max(so, axis=-1, keepdims=True))
                po = jnp.exp2(so - m)
                l = jnp.sum(po, axis=-1, keepdims=True)
                p_off.append(po.astype(BF16))
            else:
                l = 0.0
            pd = jnp.exp2(sd - m)
            l = l + jnp.sum(pd, axis=-1, keepdims=True)
            p_dg.append(pd.astype(BF16))
            inv_l.append(1.0 / l)
        acc = jnp.dot(jnp.concatenate(p_dg, axis=0), v_ref[0, off:off + tq, :], preferred_element_type=F32)
        if n > 0:
            acc = acc + jnp.dot(jnp.concatenate(p_off, axis=0), v_ref[0, 0:off, :],
                                preferred_element_type=F32)
        dlt = acc[:tq] * inv_l[0] - lam * (acc[tq:] * inv_l[1])
        y = _rms_scale(dlt, g_ref[...]) * (1.0 - LAMBDA_INIT)
        o_ref[0, rows, :] = y.astype(o_ref.dtype)

    for n, part in tiles:
        tile(n, part)


def _da_kernel(*refs, tq, nq, tpg):
    if tpg == nq:
        _da_tiles(*refs, tq=tq, tiles=[(n, n) for n in range(nq)])
        return
    qi = pl.program_id(2)
    for gq in range(nq // tpg):
        tiles = [(gq * tpg + part, part) for part in range(tpg)]
        pl.when(qi == gq)(functools.partial(_da_tiles, *refs, tq=tq, tiles=tiles))


def _diff_attention(p3, lq1, lk1, lq2, lk2, g_subln, tq=256, tpg=8):
    b, s, _ = p3.shape
    hw = 2 * DA_HEAD_DIM
    nq = s // tq
    vec = lambda: pl.BlockSpec((1, DA_HEAD_DIM), lambda bi, h, qi: (0, 0))
    return pl.pallas_call(
        functools.partial(_da_kernel, tq=tq, nq=nq, tpg=tpg),
        grid=(b, DA_HEADS, nq // tpg),
        in_specs=[vec(), vec(), vec(), vec(),
                  pl.BlockSpec((1, hw), lambda bi, h, qi: (0, 0)),
                  pl.BlockSpec((1, tpg * tq, hw), lambda bi, h, qi: (bi, qi, Q_OFF // hw + h)),
                  pl.BlockSpec((1, s, hw), lambda bi, h, qi: (bi, 0, K_OFF // hw + h)),
                  pl.BlockSpec((1, s, hw), lambda bi, h, qi: (bi, 0, V_OFF // hw + h))],
        out_specs=pl.BlockSpec((1, tpg * tq, hw), lambda bi, h, qi: (bi, qi, h)),
        out_shape=jax.ShapeDtypeStruct((b, s, DA_WIDTH), BF16),
        compiler_params=_cparams(3),
        name="diff_attention",
    )(lq1.reshape(1, -1), lk1.reshape(1, -1), lq2.reshape(1, -1), lk2.reshape(1, -1),
      g_subln.reshape(1, hw), p3, p3, p3)


def _proj_kernel(a_ref, w_ref, o_ref, wbf_ref):
    @pl.when(pl.program_id(1) == 0)
    def _():
        wbf_ref[...] = w_ref[...].astype(BF16)

    o_ref[...] = jnp.dot(a_ref[...], wbf_ref[...], preferred_element_type=F32).astype(o_ref.dtype)


def _proj(a, w, tm=1024, tn=1024):
    m, k = a.shape
    n = w.shape[1]
    return pl.pallas_call(
        _proj_kernel,
        grid=(n // tn, m // tm),
        in_specs=[pl.BlockSpec((tm, k), lambda j, i: (i, 0)),
                  pl.BlockSpec((k, tn), lambda j, i: (0, j))],
        out_specs=pl.BlockSpec((tm, tn), lambda j, i: (i, j)),
        out_shape=jax.ShapeDtypeStruct((m, n), BF16),
        scratch_shapes=[pltpu.VMEM((k, tn), BF16)],
        compiler_params=_cparams(2),
        name="proj",
    )(a, w)


def _xa_kernel(q_ref, k_ref, v_ref, o_ref):
    nt = (((1,), (1,)), ((), ()))
    s = lax.dot_general(q_ref[0], k_ref[0], nt, preferred_element_type=F32)
    m = jnp.max(s, axis=-1, keepdims=True)
    p = jnp.exp2(s - m)
    l = jnp.sum(p, axis=-1, keepdims=True)
    o = jnp.dot(p.astype(BF16), v_ref[0], preferred_element_type=F32)
    o_ref[0] = (o * (1.0 / l)).astype(o_ref.dtype)


def _cross_attention(p3, kv3):
    b, s, _ = p3.shape
    mem_len = kv3.shape[1]
    hd = XA_HEAD_DIM
    return pl.pallas_call(
        _xa_kernel,
        grid=(b, XA_HEADS),
        in_specs=[pl.BlockSpec((1, s, hd), lambda bi, h: (bi, 0, XQ_OFF // hd + h)),
                  pl.BlockSpec((1, mem_len, hd), lambda bi, h: (bi, 0, h)),
                  pl.BlockSpec((1, mem_len, hd), lambda bi, h: (bi, 0, XA_HEADS + h))],
        out_specs=pl.BlockSpec((1, s, hd), lambda bi, h: (bi, 0, h)),
        out_shape=jax.ShapeDtypeStruct((b, s, XA_WIDTH), BF16),
        compiler_params=_cparams(2),
        name="cross_attention",
    )(p3, kv3, kv3)


def _merge_kernel(oc_ref, od_ref, ox_ref, g0_ref, g1_ref, g2_ref, wc_ref, wd_ref, wx_ref, o_ref, wbf_ref):
    @pl.when(pl.program_id(1) == 0)
    def _():
        wbf_ref[0] = wc_ref[...].astype(BF16)
        wbf_ref[1] = wd_ref[...].astype(BF16)
        wbf_ref[2] = wx_ref[...].astype(BF16)

    y = g0_ref[...].astype(F32) * jnp.dot(oc_ref[...], wbf_ref[0], preferred_element_type=F32)
    y = y + g1_ref[...].astype(F32) * jnp.dot(od_ref[...], wbf_ref[1], preferred_element_type=F32)
    y = y + g2_ref[...].astype(F32) * jnp.dot(ox_ref[...], wbf_ref[2], preferred_element_type=F32)
    o_ref[...] = y.astype(o_ref.dtype)


def _merge(o_conv, o_da, o_xa, p2, w_conv_out, w_da_out, w_xa_out, tm=1024, tn=1024):
    t, kk = o_conv.shape
    gcol = GATE_OFF // tn
    gstride = D_MODEL // tn
    act = lambda: pl.BlockSpec((tm, kk), lambda j, i: (i, 0))
    wsp = lambda: pl.BlockSpec((kk, tn), lambda j, i: (0, j), pipeline_mode=pl.Buffered(1))
    gate = lambda br: pl.BlockSpec((tm, tn), lambda j, i: (i, gcol + br * gstride + j))
    return pl.pallas_call(
        _merge_kernel,
        grid=(D_MODEL // tn, t // tm),
        in_specs=[act(), act(), act(), gate(0), gate(1), gate(2), wsp(), wsp(), wsp()],
        out_specs=pl.BlockSpec((tm, tn), lambda j, i: (i, j)),
        out_shape=jax.ShapeDtypeStruct((t, D_MODEL), BF16),
        scratch_shapes=[pltpu.VMEM((3, kk, tn), BF16)],
        compiler_params=_cparams(2),
        name="merge",
    )(o_conv, o_da, o_xa, p2, p2, p2, w_conv_out, w_da_out, w_xa_out)


def _mix_kernel(a_ref, w_ref, x_ref, g_ref, x1_ref, h2_ref, wbf_ref):
    @pl.when(pl.program_id(0) == 0)
    def _():
        wbf_ref[...] = w_ref[...].astype(BF16)

    x1 = x_ref[...] + jnp.dot(a_ref[...], wbf_ref[...], preferred_element_type=F32)
    x1_ref[...] = x1
    h2_ref[...] = _rms_scale(x1, g_ref[...]).astype(h2_ref.dtype)


def _mix(merged, w_mix, x2d, g_mlp, tm=512):
    t, k = merged.shape
    n = w_mix.shape[1]
    return pl.pallas_call(
        _mix_kernel,
        grid=(t // tm,),
        in_specs=[pl.BlockSpec((tm, k), lambda i: (i, 0)),
                  pl.BlockSpec((k, n), lambda i: (0, 0), pipeline_mode=pl.Buffered(1)),
                  pl.BlockSpec((tm, n), lambda i: (i, 0)),
                  pl.BlockSpec((1, n), lambda i: (0, 0))],
        out_specs=[pl.BlockSpec((tm, n), lambda i: (i, 0)),
                   pl.BlockSpec((tm, n), lambda i: (i, 0))],
        out_shape=[jax.ShapeDtypeStruct((t, n), F32),
                   jax.ShapeDtypeStruct((t, n), BF16)],
        scratch_shapes=[pltpu.VMEM((k, n), BF16)],
        compiler_params=_cparams(1),
        name="mix_out",
    )(merged, w_mix, x2d, g_mlp.reshape(1, n))


def _mlp_kernel(h_ref, x_hbm, wu_ref, wd_ref, gf_ref, o_ref, xbuf_ref, xsem, *, nf, tm):
    i = pl.program_id(0)
    f = pl.program_id(1)

    def residual_copy():
        rows = pl.ds(pl.multiple_of(i * tm, tm), tm)
        return pltpu.make_async_copy(x_hbm.at[rows, :], xbuf_ref, xsem)

    @pl.when(f == nf - 2)
    def _():
        residual_copy().start()

    def ff_tile():
        hm = jnp.dot(h_ref[...], wu_ref[...], preferred_element_type=F32)
        hm = jnp.square(jnp.maximum(hm, 0.0)).astype(BF16)
        return jnp.dot(hm, wd_ref[...], preferred_element_type=F32)

    @pl.when(f == 0)
    def _():
        o_ref[...] = ff_tile()

    @pl.when((f > 0) & (f < nf - 1))
    def _():
        o_ref[...] += ff_tile()

    @pl.when(f == nf - 1)
    def _():
        residual_copy().wait()
        o_ref[...] = _rms_scale(xbuf_ref[...] + o_ref[...] + ff_tile(), gf_ref[...])


def _mlp(h2, x1, w_up, w_down, g_final, tm=1024, tf=1024):
    t, d = x1.shape
    ff = w_up.shape[1]
    nf = ff // tf
    assert nf >= 2
    return pl.pallas_call(
        functools.partial(_mlp_kernel, nf=nf, tm=tm),
        grid=(t // tm, nf),
        in_specs=[pl.BlockSpec((tm, d), lambda i, f: (i, 0)),
                  pl.BlockSpec(memory_space=pl.ANY),
                  pl.BlockSpec((d, tf), lambda i, f: (0, f)),
                  pl.BlockSpec((tf, d), lambda i, f: (f, 0)),
                  pl.BlockSpec((1, d), lambda i, f: (0, 0))],
        out_specs=pl.BlockSpec((tm, d), lambda i, f: (i, 0)),
        out_shape=jax.ShapeDtypeStruct((t, d), F32),
        scratch_shapes=[pltpu.VMEM((tm, d), F32), pltpu.SemaphoreType.DMA(())],
        compiler_params=_cparams(2),
        name="mlp",
    )(h2, x1, w_up, w_down, g_final.reshape(1, d))


def kernel(x, mem, positions, g_mix, w_in, w_dw, b_dw, g_conv_ln, b_conv_ln, w_conv_out, lambda_q1, lambda_k1, lambda_q2, lambda_k2, g_subln, w_da_out, g_mem, w_mem_kv, w_xa_out, w_mix_out, g_mlp, w_up, w_down, g_final):
    bsz, seq, d = x.shape
    t = bsz * seq
    x2d = x.reshape(t, d)

    h = _rmsnorm(x2d, g_mix[0], BF16)
    cos2, sin2 = _rope_tables(positions.astype(F32).reshape(t, 1))
    u, p2, w_up_bf, w_down_bf = _in_proj(h, w_in[0], cos2, sin2, w_up[0], w_down[0])
    p3 = p2.reshape(bsz, seq, P_COLS)

    o_conv = _conv_branch(u.reshape(bsz, seq, CONV_CH), w_dw[0], b_dw[0], g_conv_ln[0], b_conv_ln[0])
    o_da = _diff_attention(p3, lambda_q1[0], lambda_k1[0], lambda_q2[0], lambda_k2[0], g_subln[0])

    mem_n = _rmsnorm(mem.reshape(-1, d), g_mem[0], BF16)
    kv = _proj(mem_n, w_mem_kv[0])
    o_xa = _cross_attention(p3, kv.reshape(bsz, -1, 2 * XA_WIDTH))

    merged = _merge(o_conv.reshape(t, CONV_CH), o_da.reshape(t, DA_WIDTH), o_xa.reshape(t, XA_WIDTH),
                    p2, w_conv_out[0], w_da_out[0], w_xa_out[0])
    x1, h2 = _mix(merged, w_mix_out[0], x2d, g_mlp[0])
    out = _mlp(h2, x1, w_up_bf, w_down_bf, g_final)
    return out.reshape(bsz, seq, d)
```

```python
import functools
import math

import jax
import jax.numpy as jnp
from jax import lax
from jax.experimental import pallas as pl
from jax.experimental.pallas import tpu as pltpu

F32 = jnp.float32
BF16 = jnp.bfloat16

D_MODEL = 2048
CONV_CH = 1024
CONV_WIDTH = 31
DA_HEADS = 4
DA_HEAD_DIM = 128
DA_WIDTH = 1024
XA_HEADS = 4
XA_HEAD_DIM = 256
XA_WIDTH = 1024
D_FF = 4 * D_MODEL
ROPE_THETA = 10000.0
NORM_EPS = 1e-6
LAMBDA_INIT = 0.8 - 0.6 * math.exp(-0.3 * 0)
LOG2E = math.log2(math.e)

IN_TILE = 1024
IN_HALF = IN_TILE // 2
N_GLU_TILES = CONV_CH // IN_HALF
P_COLS = 2 * 1024 + 1024 + XA_WIDTH + 3 * D_MODEL
Q_OFF, K_OFF, V_OFF, XQ_OFF, GATE_OFF = 0, 1024, 2048, 3072, 4096

V7X_VMEM_LIMIT = 62 * 1024 * 1024
CONV_HALO = 32
SUBLANES = 8
LANES = 128


def _cparams(n_axes):
    return pltpu.CompilerParams(
        dimension_semantics=("arbitrary",) * n_axes,
        vmem_limit_bytes=V7X_VMEM_LIMIT,
    )


def _sigmoid(x):
    return 0.5 * jnp.tanh(0.5 * x) + 0.5


def _rms_scale(x, g):
    ms = jnp.mean(x * x, axis=-1, keepdims=True)
    return x * lax.rsqrt(ms + NORM_EPS) * g


def _rmsnorm_kernel(x_ref, g_ref, o_ref):
    o_ref[...] = _rms_scale(x_ref[...], g_ref[...]).astype(o_ref.dtype)


def _rmsnorm(x2d, g, out_dtype, tm=512):
    m, d = x2d.shape
    return pl.pallas_call(
        _rmsnorm_kernel,
        grid=(m // tm,),
        in_specs=[pl.BlockSpec((tm, d), lambda i: (i, 0)),
                  pl.BlockSpec((1, d), lambda i: (0, 0))],
        out_specs=pl.BlockSpec((tm, d), lambda i: (i, 0)),
        out_shape=jax.ShapeDtypeStruct((m, d), out_dtype),
        compiler_params=_cparams(1),
        name="rmsnorm",
    )(x2d, g.reshape(1, d))


def _rope_kernel(pos_lo_ref, pos_hi_ref, f_ref, cos_ref, sin_ref, *, half_rows):
    half = DA_HEAD_DIM // 2
    lane = lax.broadcasted_iota(jnp.int32, (half_rows, DA_HEAD_DIM), 1)
    left = lane < half
    ang = jnp.where(left, pos_lo_ref[...], pos_hi_ref[...]) * f_ref[...]
    c = jnp.cos(ang)
    s = jnp.sin(ang)
    c_sw = pltpu.roll(c, half, axis=1)
    s_sw = pltpu.roll(s, half, axis=1)
    cos_ref[0:half_rows, :] = jnp.where(left, c, c_sw)
    cos_ref[half_rows:, :] = jnp.where(left, c_sw, c)
    sin_ref[0:half_rows, :] = jnp.where(left, -s, s_sw)
    sin_ref[half_rows:, :] = jnp.where(left, -s_sw, s)


def _rope_tables(pos_col, tm=1024):
    t = pos_col.shape[0]
    inv_freq = 1.0 / (ROPE_THETA ** (jnp.arange(0, DA_HEAD_DIM, 2, dtype=F32) / DA_HEAD_DIM))
    f2 = jnp.concatenate([inv_freq, inv_freq]).reshape(1, DA_HEAD_DIM)
    half_rows = tm // 2
    return pl.pallas_call(
        functools.partial(_rope_kernel, half_rows=half_rows),
        grid=(t // tm,),
        in_specs=[pl.BlockSpec((half_rows, 1), lambda i: (2 * i, 0)),
                  pl.BlockSpec((half_rows, 1), lambda i: (2 * i + 1, 0)),
                  pl.BlockSpec((1, DA_HEAD_DIM), lambda i: (0, 0))],
        out_specs=[pl.BlockSpec((tm, DA_HEAD_DIM), lambda i: (i, 0))] * 2,
        out_shape=[jax.ShapeDtypeStruct((t, DA_HEAD_DIM), F32)] * 2,
        compiler_params=_cparams(1),
        name="rope_tables",
    )(pos_col, pos_col, f2)


def _in_proj_kernel(h_ref, wa_ref, wb_ref, cos_ref, sin_ref, u_ref, p_ref, wbf_ref):
    j = pl.program_id(0)
    n = j - N_GLU_TILES

    @pl.when(pl.program_id(1) == 0)
    def _():
        wbf_ref[:, :IN_HALF] = wa_ref[...].astype(BF16)
        wbf_ref[:, IN_HALF:] = wb_ref[...].astype(BF16)

    def proj():
        return jnp.dot(h_ref[...], wbf_ref[...], preferred_element_type=F32)

    @pl.when(j < N_GLU_TILES)
    def _():
        acc = proj()
        u_ref[...] = (acc[:, :IN_HALF] * _sigmoid(acc[:, IN_HALF:])).astype(u_ref.dtype)

    @pl.when((n >= 0) & (n < 2))
    def _():
        acc = proj()
        scale = jnp.where(n == 0, DA_HEAD_DIM ** -0.5 * LOG2E, 1.0).astype(F32)
        c = cos_ref[...] * scale
        s = sin_ref[...] * scale
        for g in range(IN_TILE // DA_HEAD_DIM):
            sl = slice(g * DA_HEAD_DIM, (g + 1) * DA_HEAD_DIM)
            t = acc[:, sl]
            r = pltpu.roll(t, DA_HEAD_DIM // 2, axis=1)
            p_ref[:, sl] = (t * c + r * s).astype(p_ref.dtype)

    @pl.when((n >= 2) & (n < 4))
    def _():
        scale = jnp.where(n == 3, XA_HEAD_DIM ** -0.5 * LOG2E, 1.0).astype(F32)
        p_ref[...] = (proj() * scale).astype(p_ref.dtype)

    @pl.when(n >= 4)
    def _():
        p_ref[...] = _sigmoid(proj()).astype(p_ref.dtype)


def _in_proj(h, w_in, cos2, sin2, tm=1024):
    t, d = h.shape
    n_i = t // tm
    glu = lambda j: j < N_GLU_TILES
    wa_map = lambda j, i: (0, jnp.where(glu(j), j, 2 * j))
    wb_map = lambda j, i: (0, jnp.where(glu(j), j + N_GLU_TILES, 2 * j + 1))
    u_map = lambda j, i: (jnp.where(glu(j), i, n_i - 1), jnp.where(glu(j), j, N_GLU_TILES - 1))
    p_map = lambda j, i: (jnp.where(glu(j), 0, i), jnp.where(glu(j), 0, j - N_GLU_TILES))
    return pl.pallas_call(
        _in_proj_kernel,
        grid=(N_GLU_TILES + P_COLS // IN_TILE, n_i),
        in_specs=[pl.BlockSpec((tm, d), lambda j, i: (i, 0)),
                  pl.BlockSpec((d, IN_HALF), wa_map),
                  pl.BlockSpec((d, IN_HALF), wb_map),
                  pl.BlockSpec((tm, DA_HEAD_DIM), lambda j, i: (i, 0)),
                  pl.BlockSpec((tm, DA_HEAD_DIM), lambda j, i: (i, 0))],
        out_specs=[pl.BlockSpec((tm, IN_HALF), u_map),
                   pl.BlockSpec((tm, IN_TILE), p_map)],
        out_shape=[jax.ShapeDtypeStruct((t, CONV_CH), BF16),
                   jax.ShapeDtypeStruct((t, P_COLS), BF16)],
        scratch_shapes=[pltpu.VMEM((d, IN_TILE), BF16)],
        compiler_params=_cparams(2),
        name="in_proj",
    )(h, w_in, w_in, cos2, sin2)


def _conv_kernel(ucur_ref, uprev_ref, w_ref, bdw_ref, g_ref, b_ref, o_ref, buf_ref, sh_ref, y_ref, *, ts, rg, rc):
    si = pl.program_id(1)
    prev = uprev_ref[0].astype(F32)
    buf_ref[0:CONV_HALO, :] = jnp.where(si > 0, prev, 0.0)
    buf_ref[CONV_HALO:CONV_HALO + ts, :] = ucur_ref[0].astype(F32)
    buf_ref[CONV_HALO + ts:, :] = jnp.zeros((SUBLANES, CONV_CH), F32)
    n_sh = ts + CONV_HALO
    for r in range(1, SUBLANES):
        sh_ref[r - 1] = buf_ref[pl.ds(r, n_sh), :]

    lead = CONV_HALO - (CONV_WIDTH - 1)
    rows_per_iter = rg * SUBLANES

    for c in range(CONV_CH // LANES):
        lanes = slice(c * LANES, (c + 1) * LANES)
        taps = [jnp.broadcast_to(w_ref[k:k + 1, lanes], (SUBLANES, LANES)) for k in range(CONV_WIDTH)]
        bias = jnp.broadcast_to(bdw_ref[:, lanes], (SUBLANES, LANES))

        def rows_body(it, carry, lanes=lanes, taps=taps, bias=bias):
            r0 = pl.multiple_of(it * rows_per_iter, rows_per_iter)
            for gi in range(rg):
                acc = bias
                for k in range(CONV_WIDTH):
                    a, r = divmod(lead + k, SUBLANES)
                    row0 = pl.multiple_of(r0 + SUBLANES * (a + gi), SUBLANES)
                    if r == 0:
                        rows = buf_ref[pl.ds(row0, SUBLANES), lanes]
                    else:
                        rows = sh_ref[r - 1, pl.ds(row0, SUBLANES), lanes]
                    acc = acc + rows * taps[k]
                y_ref[pl.ds(pl.multiple_of(r0 + SUBLANES * gi, SUBLANES), SUBLANES), lanes] = acc
            return carry

        lax.fori_loop(0, ts // rows_per_iter, rows_body, 0)

    for ci in range(ts // rc):
        rows = slice(ci * rc, (ci + 1) * rc)
        acc = y_ref[rows, :]
        mu = jnp.mean(acc, axis=-1, keepdims=True)
        cen = acc - mu
        var = jnp.mean(cen * cen, axis=-1, keepdims=True)
        y = cen * lax.rsqrt(var + NORM_EPS) * g_ref[...] + b_ref[...]
        o_ref[0, rows, :] = (y * _sigmoid(y)).astype(o_ref.dtype)


def _conv_branch(u3, w_dw, b_dw, g_ln, b_ln, ts=512, rg=16, rc=128):
    b, s, c = u3.shape
    hb = ts // CONV_HALO
    return pl.pallas_call(
        functools.partial(_conv_kernel, ts=ts, rg=rg, rc=rc),
        grid=(b, s // ts),
        in_specs=[pl.BlockSpec((1, ts, c), lambda bi, si: (bi, si, 0)),
                  pl.BlockSpec((1, CONV_HALO, c), lambda bi, si: (bi, jnp.maximum(si * hb - 1, 0), 0)),
                  pl.BlockSpec((CONV_WIDTH, c), lambda bi, si: (0, 0)),
                  pl.BlockSpec((1, c), lambda bi, si: (0, 0)),
                  pl.BlockSpec((1, c), lambda bi, si: (0, 0)),
                  pl.BlockSpec((1, c), lambda bi, si: (0, 0))],
        out_specs=pl.BlockSpec((1, ts, c), lambda bi, si: (bi, si, 0)),
        out_shape=jax.ShapeDtypeStruct((b, s, c), BF16),
        scratch_shapes=[pltpu.VMEM((CONV_HALO + ts + SUBLANES, c), F32),
                        pltpu.VMEM((SUBLANES - 1, CONV_HALO + ts, c), F32),
                        pltpu.VMEM((ts, c), F32)],
        compiler_params=_cparams(2),
        name="conv_branch",
    )(u3, u3, w_dw, b_dw.reshape(1, c), g_ln.reshape(1, c), b_ln.reshape(1, c))


def _da_tiles(lq1_ref, lk1_ref, lq2_ref, lk2_ref, g_ref, q_ref, k_ref, v_ref, o_ref, *, tq, tiles):
    hd = DA_HEAD_DIM
    nt = (((1,), (1,)), ((), ()))
    lam = (jnp.exp(jnp.sum(lq1_ref[...] * lk1_ref[...], axis=-1, keepdims=True))
           - jnp.exp(jnp.sum(lq2_ref[...] * lk2_ref[...], axis=-1, keepdims=True))
           + LAMBDA_INIT)

    def tile(n, part):
        off = n * tq
        rows = slice(part * tq, (part + 1) * tq)
        q = q_ref[0, rows, :]
        causal = (lax.broadcasted_iota(jnp.int32, (tq, tq), 0)
                  >= lax.broadcasted_iota(jnp.int32, (tq, tq), 1))
        p_off, p_dg, inv_l = [], [], []
        for mp in range(2):
            cs = slice(mp * hd, (mp + 1) * hd)
            sd = lax.dot_general(q[:, cs], k_ref[0, off:off + tq, cs], nt, preferred_element_type=F32)
            sd = jnp.where(causal, sd, -jnp.inf)
            m = jnp.max(sd, axis=-1, keepdims=True)
            if n > 0:
                so = lax.dot_general(q[:, cs], k_ref[0, 0:off, cs], nt, preferred_element_type=F32)
                m = jnp.maximum(m, jnp.max(so, axis=-1, keepdims=True))
                po = jnp.exp2(so - m)
                l = jnp.sum(po, axis=-1, keepdims=True)
                p_off.append(po.astype(BF16))
            else:
                l = 0.0
            pd = jnp.exp2(sd - m)
            l = l + jnp.sum(pd, axis=-1, keepdims=True)
            p_dg.append(pd.astype(BF16))
            inv_l.append(1.0 / l)
        acc = jnp.dot(jnp.concatenate(p_dg, axis=0), v_ref[0, off:off + tq, :], preferred_element_type=F32)
        if n > 0:
            acc = acc + jnp.dot(jnp.concatenate(p_off, axis=0), v_ref[0, 0:off, :],
                                preferred_element_type=F32)
        dlt = acc[:tq] * inv_l[0] - lam * (acc[tq:] * inv_l[1])
        y = _rms_scale(dlt, g_ref[...]) * (1.0 - LAMBDA_INIT)
        o_ref[0, rows, :] = y.astype(o_ref.dtype)

    for n, part in tiles:
        tile(n, part)


def _da_kernel(lq1_ref, lk1_ref, lq2_ref, lk2_ref, g_ref, q_ref, k_ref, v_ref, wu_src_ref, wd_src_ref,
               o_ref, wu_bf_ref, wd_bf_ref, *, tq, nq):
    wu_bf_ref[...] = wu_src_ref[...].astype(BF16)
    wd_bf_ref[...] = wd_src_ref[...].astype(BF16)
    _da_tiles(lq1_ref, lk1_ref, lq2_ref, lk2_ref, g_ref, q_ref, k_ref, v_ref, o_ref,
              tq=tq, tiles=[(n, n) for n in range(nq)])


def _diff_attention(p3, lq1, lk1, lq2, lk2, g_subln, w_up, w_down, tq=256):
    b, s, _ = p3.shape
    hw = 2 * DA_HEAD_DIM
    nq = s // tq
    n_steps = b * DA_HEADS
    up_rows = w_up.shape[0] // n_steps
    dn_rows = w_down.shape[0] // n_steps
    vec = lambda: pl.BlockSpec((1, DA_HEAD_DIM), lambda bi, h: (0, 0))
    chunk = lambda bi, h: (bi * DA_HEADS + h, 0)
    return pl.pallas_call(
        functools.partial(_da_kernel, tq=tq, nq=nq),
        grid=(b, DA_HEADS),
        in_specs=[vec(), vec(), vec(), vec(),
                  pl.BlockSpec((1, hw), lambda bi, h: (0, 0)),
                  pl.BlockSpec((1, s, hw), lambda bi, h: (bi, 0, Q_OFF // hw + h)),
                  pl.BlockSpec((1, s, hw), lambda bi, h: (bi, 0, K_OFF // hw + h)),
                  pl.BlockSpec((1, s, hw), lambda bi, h: (bi, 0, V_OFF // hw + h)),
                  pl.BlockSpec((up_rows, w_up.shape[1]), chunk),
                  pl.BlockSpec((dn_rows, w_down.shape[1]), chunk)],
        out_specs=[pl.BlockSpec((1, s, hw), lambda bi, h: (bi, 0, h)),
                   pl.BlockSpec((up_rows, w_up.shape[1]), chunk),
                   pl.BlockSpec((dn_rows, w_down.shape[1]), chunk)],
        out_shape=[jax.ShapeDtypeStruct((b, s, DA_WIDTH), BF16),
                   jax.ShapeDtypeStruct(w_up.shape, BF16),
                   jax.ShapeDtypeStruct(w_down.shape, BF16)],
        compiler_params=_cparams(2),
        name="diff_attention",
    )(lq1.reshape(1, -1), lk1.reshape(1, -1), lq2.reshape(1, -1), lk2.reshape(1, -1),
      g_subln.reshape(1, hw), p3, p3, p3, w_up, w_down)


def _proj_kernel(a_ref, w_ref, o_ref, wbf_ref):
    @pl.when(pl.program_id(1) == 0)
    def _():
        wbf_ref[...] = w_ref[...].astype(BF16)

    o_ref[...] = jnp.dot(a_ref[...], wbf_ref[...], preferred_element_type=F32).astype(o_ref.dtype)


def _proj(a, w, tm=1024, tn=1024):
    m, k = a.shape
    n = w.shape[1]
    return pl.pallas_call(
        _proj_kernel,
        grid=(n // tn, m // tm),
        in_specs=[pl.BlockSpec((tm, k), lambda j, i: (i, 0)),
                  pl.BlockSpec((k, tn), lambda j, i: (0, j))],
        out_specs=pl.BlockSpec((tm, tn), lambda j, i: (i, j)),
        out_shape=jax.ShapeDtypeStruct((m, n), BF16),
        scratch_shapes=[pltpu.VMEM((k, tn), BF16)],
        compiler_params=_cparams(2),
        name="proj",
    )(a, w)


def _xa_kernel(q_ref, k_ref, v_ref, o_ref):
    nt = (((1,), (1,)), ((), ()))
    s = lax.dot_general(q_ref[0], k_ref[0], nt, preferred_element_type=F32)
    m = jnp.max(s, axis=-1, keepdims=True)
    p = jnp.exp2(s - m)
    l = jnp.sum(p, axis=-1, keepdims=True)
    o = jnp.dot(p.astype(BF16), v_ref[0], preferred_element_type=F32)
    o_ref[0] = (o * (1.0 / l)).astype(o_ref.dtype)


def _cross_attention(p3, kv3):
    b, s, _ = p3.shape
    mem_len = kv3.shape[1]
    hd = XA_HEAD_DIM
    return pl.pallas_call(
        _xa_kernel,
        grid=(b, XA_HEADS),
        in_specs=[pl.BlockSpec((1, s, hd), lambda bi, h: (bi, 0, XQ_OFF // hd + h)),
                  pl.BlockSpec((1, mem_len, hd), lambda bi, h: (bi, 0, h)),
                  pl.BlockSpec((1, mem_len, hd), lambda bi, h: (bi, 0, XA_HEADS + h))],
        out_specs=pl.BlockSpec((1, s, hd), lambda bi, h: (bi, 0, h)),
        out_shape=jax.ShapeDtypeStruct((b, s, XA_WIDTH), BF16),
        compiler_params=_cparams(2),
        name="cross_attention",
    )(p3, kv3, kv3)


def _merge_kernel(oc_ref, od_ref, ox_ref, g0_ref, g1_ref, g2_ref, wc_ref, wd_ref, wx_ref, o_ref, wbf_ref):
    @pl.when(pl.program_id(1) == 0)
    def _():
        wbf_ref[0] = wc_ref[...].astype(BF16)
        wbf_ref[1] = wd_ref[...].astype(BF16)
        wbf_ref[2] = wx_ref[...].astype(BF16)

    y = g0_ref[...].astype(F32) * jnp.dot(oc_ref[...], wbf_ref[0], preferred_element_type=F32)
    y = y + g1_ref[...].astype(F32) * jnp.dot(od_ref[...], wbf_ref[1], preferred_element_type=F32)
    y = y + g2_ref[...].astype(F32) * jnp.dot(ox_ref[...], wbf_ref[2], preferred_element_type=F32)
    o_ref[...] = y.astype(o_ref.dtype)


def _merge(o_conv, o_da, o_xa, p2, w_conv_out, w_da_out, w_xa_out, tm=1024, tn=1024):
    t, kk = o_conv.shape
    gcol = GATE_OFF // tn
    gstride = D_MODEL // tn
    act = lambda: pl.BlockSpec((tm, kk), lambda j, i: (i, 0))
    wsp = lambda: pl.BlockSpec((kk, tn), lambda j, i: (0, j), pipeline_mode=pl.Buffered(1))
    gate = lambda br: pl.BlockSpec((tm, tn), lambda j, i: (i, gcol + br * gstride + j))
    return pl.pallas_call(
        _merge_kernel,
        grid=(D_MODEL // tn, t // tm),
        in_specs=[act(), act(), act(), gate(0), gate(1), gate(2), wsp(), wsp(), wsp()],
        out_specs=pl.BlockSpec((tm, tn), lambda j, i: (i, j)),
        out_shape=jax.ShapeDtypeStruct((t, D_MODEL), BF16),
        scratch_shapes=[pltpu.VMEM((3, kk, tn), BF16)],
        compiler_params=_cparams(2),
        name="merge",
    )(o_conv, o_da, o_xa, p2, p2, p2, w_conv_out, w_da_out, w_xa_out)


def _mix_kernel(a_ref, w_ref, x_ref, g_ref, x1_ref, h2_ref, wbf_ref):
    @pl.when(pl.program_id(0) == 0)
    def _():
        wbf_ref[...] = w_ref[...].astype(BF16)

    x1 = x_ref[...] + jnp.dot(a_ref[...], wbf_ref[...], preferred_element_type=F32)
    x1_ref[...] = x1
    h2_ref[...] = _rms_scale(x1, g_ref[...]).astype(h2_ref.dtype)


def _mix(merged, w_mix, x2d, g_mlp, tm=512):
    t, k = merged.shape
    n = w_mix.shape[1]
    return pl.pallas_call(
        _mix_kernel,
        grid=(t // tm,),
        in_specs=[pl.BlockSpec((tm, k), lambda i: (i, 0)),
                  pl.BlockSpec((k, n), lambda i: (0, 0), pipeline_mode=pl.Buffered(1)),
                  pl.BlockSpec((tm, n), lambda i: (i, 0)),
                  pl.BlockSpec((1, n), lambda i: (0, 0))],
        out_specs=[pl.BlockSpec((tm, n), lambda i: (i, 0)),
                   pl.BlockSpec((tm, n), lambda i: (i, 0))],
        out_shape=[jax.ShapeDtypeStruct((t, n), F32),
                   jax.ShapeDtypeStruct((t, n), BF16)],
        scratch_shapes=[pltpu.VMEM((k, n), BF16)],
        compiler_params=_cparams(1),
        name="mix_out",
    )(merged, w_mix, x2d, g_mlp.reshape(1, n))


def _mlp_kernel(h_ref, x_hbm, wu_ref, wd_ref, gf_ref, o_ref, xbuf_ref, xsem, *, nf, tm):
    i = pl.program_id(0)
    f = pl.program_id(1)

    def residual_copy():
        rows = pl.ds(pl.multiple_of(i * tm, tm), tm)
        return pltpu.make_async_copy(x_hbm.at[rows, :], xbuf_ref, xsem)

    @pl.when(f == nf - 2)
    def _():
        residual_copy().start()

    def ff_tile():
        hm = jnp.dot(h_ref[...], wu_ref[...], preferred_element_type=F32)
        hm = jnp.square(jnp.maximum(hm, 0.0)).astype(BF16)
        return jnp.dot(hm, wd_ref[...], preferred_element_type=F32)

    @pl.when(f == 0)
    def _():
        o_ref[...] = ff_tile()

    @pl.when((f > 0) & (f < nf - 1))
    def _():
        o_ref[...] += ff_tile()

    @pl.when(f == nf - 1)
    def _():
        residual_copy().wait()
        o_ref[...] = _rms_scale(xbuf_ref[...] + o_ref[...] + ff_tile(), gf_ref[...])


def _mlp(h2, x1, w_up, w_down, g_final, tm=1024, tf=1024):
    t, d = x1.shape
    ff = w_up.shape[1]
    nf = ff // tf
    assert nf >= 2
    return pl.pallas_call(
        functools.partial(_mlp_kernel, nf=nf, tm=tm),
        grid=(t // tm, nf),
        in_specs=[pl.BlockSpec((tm, d), lambda i, f: (i, 0)),
                  pl.BlockSpec(memory_space=pl.ANY),
                  pl.BlockSpec((d, tf), lambda i, f: (0, f)),
                  pl.BlockSpec((tf, d), lambda i, f: (f, 0)),
                  pl.BlockSpec((1, d), lambda i, f: (0, 0))],
        out_specs=pl.BlockSpec((tm, d), lambda i, f: (i, 0)),
        out_shape=jax.ShapeDtypeStruct((t, d), F32),
        scratch_shapes=[pltpu.VMEM((tm, d), F32), pltpu.SemaphoreType.DMA(())],
        compiler_params=_cparams(2),
        name="mlp",
    )(h2, x1, w_up, w_down, g_final.reshape(1, d))


def kernel(x, mem, positions, g_mix, w_in, w_dw, b_dw, g_conv_ln, b_conv_ln, w_conv_out, lambda_q1, lambda_k1, lambda_q2, lambda_k2, g_subln, w_da_out, g_mem, w_mem_kv, w_xa_out, w_mix_out, g_mlp, w_up, w_down, g_final):
    bsz, seq, d = x.shape
    t = bsz * seq
    x2d = x.reshape(t, d)

    h = _rmsnorm(x2d, g_mix[0], BF16)
    cos2, sin2 = _rope_tables(positions.astype(F32).reshape(t, 1))
    u, p2 = _in_proj(h, w_in[0], cos2, sin2)
    p3 = p2.reshape(bsz, seq, P_COLS)

    o_conv = _conv_branch(u.reshape(bsz, seq, CONV_CH), w_dw[0], b_dw[0], g_conv_ln[0], b_conv_ln[0])
    o_da, w_up_bf, w_down_bf = _diff_attention(p3, lambda_q1[0], lambda_k1[0], lambda_q2[0], lambda_k2[0],
                                               g_subln[0], w_up[0], w_down[0])

    mem_n = _rmsnorm(mem.reshape(-1, d), g_mem[0], BF16)
    kv = _proj(mem_n, w_mem_kv[0])
    o_xa = _cross_attention(p3, kv.reshape(bsz, -1, 2 * XA_WIDTH))

    merged = _merge(o_conv.reshape(t, CONV_CH), o_da.reshape(t, DA_WIDTH), o_xa.reshape(t, XA_WIDTH),
                    p2, w_conv_out[0], w_da_out[0], w_xa_out[0])
    x1, h2 = _mix(merged, w_mix_out[0], x2d, g_mlp[0])
    out = _mlp(h2, x1, w_up_bf, w_down_bf, g_final)
    return out.reshape(bsz, seq, d)
```

```python
import functools
import math

import jax
import jax.numpy as jnp
from jax import lax
from jax.experimental import pallas as pl
from jax.experimental.pallas import tpu as pltpu

F32 = jnp.float32
BF16 = jnp.bfloat16

D_MODEL = 2048
CONV_CH = 1024
CONV_WIDTH = 31
DA_HEADS = 4
DA_HEAD_DIM = 128
DA_WIDTH = 1024
XA_HEADS = 4
XA_HEAD_DIM = 256
XA_WIDTH = 1024
D_FF = 4 * D_MODEL
ROPE_THETA = 10000.0
NORM_EPS = 1e-6
LAMBDA_INIT = 0.8 - 0.6 * math.exp(-0.3 * 0)
LOG2E = math.log2(math.e)

IN_TILE = 1024
IN_HALF = IN_TILE // 2
N_GLU_TILES = CONV_CH // IN_HALF
P_COLS = 2 * 1024 + 1024 + XA_WIDTH + 3 * D_MODEL
Q_OFF, K_OFF, V_OFF, XQ_OFF, GATE_OFF = 0, 1024, 2048, 3072, 4096

V7X_VMEM_LIMIT = 62 * 1024 * 1024
CONV_HALO = 32
SUBLANES = 8
LANES = 128


def _cparams(n_axes):
    return pltpu.CompilerParams(
        dimension_semantics=("arbitrary",) * n_axes,
        vmem_limit_bytes=V7X_VMEM_LIMIT,
    )


def _sigmoid(x):
    return 0.5 * jnp.tanh(0.5 * x) + 0.5


def _rms_scale(x, g):
    ms = jnp.mean(x * x, axis=-1, keepdims=True)
    return x * lax.rsqrt(ms + NORM_EPS) * g


def _rmsnorm_kernel(x_ref, g_ref, o_ref):
    o_ref[...] = _rms_scale(x_ref[...], g_ref[...]).astype(o_ref.dtype)


def _rmsnorm(x2d, g, out_dtype, tm=2048):
    m, d = x2d.shape
    tm = min(tm, m)
    return pl.pallas_call(
        _rmsnorm_kernel,
        grid=(m // tm,),
        in_specs=[pl.BlockSpec((tm, d), lambda i: (i, 0)),
                  pl.BlockSpec((1, d), lambda i: (0, 0))],
        out_specs=pl.BlockSpec((tm, d), lambda i: (i, 0)),
        out_shape=jax.ShapeDtypeStruct((m, d), out_dtype),
        compiler_params=_cparams(1),
        name="rmsnorm",
    )(x2d, g.reshape(1, d))


def _rope_kernel(pos_lo_ref, pos_hi_ref, f_ref, cos_ref, sin_ref, *, half_rows):
    half = DA_HEAD_DIM // 2
    lane = lax.broadcasted_iota(jnp.int32, (half_rows, DA_HEAD_DIM), 1)
    left = lane < half
    ang = jnp.where(left, pos_lo_ref[...], pos_hi_ref[...]) * f_ref[...]
    c = jnp.cos(ang)
    s = jnp.sin(ang)
    c_sw = pltpu.roll(c, half, axis=1)
    s_sw = pltpu.roll(s, half, axis=1)
    cos_ref[0:half_rows, :] = jnp.where(left, c, c_sw)
    cos_ref[half_rows:, :] = jnp.where(left, c_sw, c)
    sin_ref[0:half_rows, :] = jnp.where(left, -s, s_sw)
    sin_ref[half_rows:, :] = jnp.where(left, -s_sw, s)


def _rope_tables(pos_col, tm=2048):
    t = pos_col.shape[0]
    inv_freq = 1.0 / (ROPE_THETA ** (jnp.arange(0, DA_HEAD_DIM, 2, dtype=F32) / DA_HEAD_DIM))
    f2 = jnp.concatenate([inv_freq, inv_freq]).reshape(1, DA_HEAD_DIM)
    half_rows = tm // 2
    return pl.pallas_call(
        functools.partial(_rope_kernel, half_rows=half_rows),
        grid=(t // tm,),
        in_specs=[pl.BlockSpec((half_rows, 1), lambda i: (2 * i, 0)),
                  pl.BlockSpec((half_rows, 1), lambda i: (2 * i + 1, 0)),
                  pl.BlockSpec((1, DA_HEAD_DIM), lambda i: (0, 0))],
        out_specs=[pl.BlockSpec((tm, DA_HEAD_DIM), lambda i: (i, 0))] * 2,
        out_shape=[jax.ShapeDtypeStruct((t, DA_HEAD_DIM), F32)] * 2,
        compiler_params=_cparams(1),
        name="rope_tables",
    )(pos_col, pos_col, f2)


def _in_proj_kernel(h_ref, wa_ref, wb_ref, cos_ref, sin_ref, u_ref, p_ref, wbf_ref, *, sub):
    j = pl.program_id(0)
    n = j - N_GLU_TILES
    blocks = [slice(r0, r0 + sub) for r0 in range(0, h_ref.shape[0], sub)]

    @pl.when(pl.program_id(1) == 0)
    def _():
        wbf_ref[:, :IN_HALF] = wa_ref[...].astype(BF16)
        wbf_ref[:, IN_HALF:] = wb_ref[...].astype(BF16)

    def proj(rows):
        return jnp.dot(h_ref[rows, :], wbf_ref[...], preferred_element_type=F32)

    @pl.when(j < N_GLU_TILES)
    def _():
        for rows in blocks:
            acc = proj(rows)
            u_ref[rows, :] = (acc[:, :IN_HALF] * _sigmoid(acc[:, IN_HALF:])).astype(u_ref.dtype)

    @pl.when((n >= 0) & (n < 2))
    def _():
        scale = jnp.where(n == 0, DA_HEAD_DIM ** -0.5 * LOG2E, 1.0).astype(F32)
        for rows in blocks:
            acc = proj(rows)
            c = cos_ref[rows, :] * scale
            s = sin_ref[rows, :] * scale
            for g in range(IN_TILE // DA_HEAD_DIM):
                sl = slice(g * DA_HEAD_DIM, (g + 1) * DA_HEAD_DIM)
                t = acc[:, sl]
                r = pltpu.roll(t, DA_HEAD_DIM // 2, axis=1)
                p_ref[rows, sl] = (t * c + r * s).astype(p_ref.dtype)

    @pl.when((n >= 2) & (n < 4))
    def _():
        scale = jnp.where(n == 3, XA_HEAD_DIM ** -0.5 * LOG2E, 1.0).astype(F32)
        for rows in blocks:
            p_ref[rows, :] = (proj(rows) * scale).astype(p_ref.dtype)

    @pl.when(n >= 4)
    def _():
        for rows in blocks:
            p_ref[rows, :] = _sigmoid(proj(rows)).astype(p_ref.dtype)


def _in_proj(h, w_in, cos2, sin2, tm=2048, sub=512):
    t, d = h.shape
    n_i = t // tm
    glu = lambda j: j < N_GLU_TILES
    wa_map = lambda j, i: (0, jnp.where(glu(j), j, 2 * j))
    wb_map = lambda j, i: (0, jnp.where(glu(j), j + N_GLU_TILES, 2 * j + 1))
    u_map = lambda j, i: (jnp.where(glu(j), i, n_i - 1), jnp.where(glu(j), j, N_GLU_TILES - 1))
    p_map = lambda j, i: (jnp.where(glu(j), 0, i), jnp.where(glu(j), 0, j - N_GLU_TILES))
    return pl.pallas_call(
        functools.partial(_in_proj_kernel, sub=sub),
        grid=(N_GLU_TILES + P_COLS // IN_TILE, n_i),
        in_specs=[pl.BlockSpec((tm, d), lambda j, i: (i, 0)),
                  pl.BlockSpec((d, IN_HALF), wa_map),
                  pl.BlockSpec((d, IN_HALF), wb_map),
                  pl.BlockSpec((tm, DA_HEAD_DIM), lambda j, i: (i, 0)),
                  pl.BlockSpec((tm, DA_HEAD_DIM), lambda j, i: (i, 0))],
        out_specs=[pl.BlockSpec((tm, IN_HALF), u_map),
                   pl.BlockSpec((tm, IN_TILE), p_map)],
        out_shape=[jax.ShapeDtypeStruct((t, CONV_CH), BF16),
                   jax.ShapeDtypeStruct((t, P_COLS), BF16)],
        scratch_shapes=[pltpu.VMEM((d, IN_TILE), BF16)],
        compiler_params=_cparams(2),
        name="in_proj",
    )(h, w_in, w_in, cos2, sin2)


def _conv_kernel(ucur_ref, uprev_ref, w_ref, bdw_ref, g_ref, b_ref, o_ref, buf_ref, sh_ref, y_ref, *, ts, rg, rc):
    si = pl.program_id(1)
    prev = uprev_ref[0].astype(F32)
    buf_ref[0:CONV_HALO, :] = jnp.where(si > 0, prev, 0.0)
    buf_ref[CONV_HALO:CONV_HALO + ts, :] = ucur_ref[0].astype(F32)
    buf_ref[CONV_HALO + ts:, :] = jnp.zeros((SUBLANES, CONV_CH), F32)
    n_sh = ts + CONV_HALO
    for r in range(1, SUBLANES):
        sh_ref[r - 1] = buf_ref[pl.ds(r, n_sh), :]

    lead = CONV_HALO - (CONV_WIDTH - 1)
    rows_per_iter = rg * SUBLANES

    for c in range(CONV_CH // LANES):
        lanes = slice(c * LANES, (c + 1) * LANES)
        taps = [jnp.broadcast_to(w_ref[k:k + 1, lanes], (SUBLANES, LANES)) for k in range(CONV_WIDTH)]
        bias = jnp.broadcast_to(bdw_ref[:, lanes], (SUBLANES, LANES))

        def rows_body(it, carry, lanes=lanes, taps=taps, bias=bias):
            r0 = pl.multiple_of(it * rows_per_iter, rows_per_iter)
            for gi in range(rg):
                acc = bias
                for k in range(CONV_WIDTH):
                    a, r = divmod(lead + k, SUBLANES)
                    row0 = pl.multiple_of(r0 + SUBLANES * (a + gi), SUBLANES)
                    if r == 0:
                        rows = buf_ref[pl.ds(row0, SUBLANES), lanes]
                    else:
                        rows = sh_ref[r - 1, pl.ds(row0, SUBLANES), lanes]
                    acc = acc + rows * taps[k]
                y_ref[pl.ds(pl.multiple_of(r0 + SUBLANES * gi, SUBLANES), SUBLANES), lanes] = acc
            return carry

        lax.fori_loop(0, ts // rows_per_iter, rows_body, 0)

    for ci in range(ts // rc):
        rows = slice(ci * rc, (ci + 1) * rc)
        acc = y_ref[rows, :]
        mu = jnp.mean(acc, axis=-1, keepdims=True)
        cen = acc - mu
        var = jnp.mean(cen * cen, axis=-1, keepdims=True)
        y = cen * lax.rsqrt(var + NORM_EPS) * g_ref[...] + b_ref[...]
        o_ref[0, rows, :] = (y * _sigmoid(y)).astype(o_ref.dtype)


def _conv_branch(u3, w_dw, b_dw, g_ln, b_ln, ts=1024, rg=16, rc=128):
    b, s, c = u3.shape
    hb = ts // CONV_HALO
    return pl.pallas_call(
        functools.partial(_conv_kernel, ts=ts, rg=rg, rc=rc),
        grid=(b, s // ts),
        in_specs=[pl.BlockSpec((1, ts, c), lambda bi, si: (bi, si, 0)),
                  pl.BlockSpec((1, CONV_HALO, c), lambda bi, si: (bi, jnp.maximum(si * hb - 1, 0), 0)),
                  pl.BlockSpec((CONV_WIDTH, c), lambda bi, si: (0, 0)),
                  pl.BlockSpec((1, c), lambda bi, si: (0, 0)),
                  pl.BlockSpec((1, c), lambda bi, si: (0, 0)),
                  pl.BlockSpec((1, c), lambda bi, si: (0, 0))],
        out_specs=pl.BlockSpec((1, ts, c), lambda bi, si: (bi, si, 0)),
        out_shape=jax.ShapeDtypeStruct((b, s, c), BF16),
        scratch_shapes=[pltpu.VMEM((CONV_HALO + ts + SUBLANES, c), F32),
                        pltpu.VMEM((SUBLANES - 1, CONV_HALO + ts, c), F32),
                        pltpu.VMEM((ts, c), F32)],
        compiler_params=_cparams(2),
        name="conv_branch",
    )(u3, u3, w_dw, b_dw.reshape(1, c), g_ln.reshape(1, c), b_ln.reshape(1, c))


def _da_tiles(lq1_ref, lk1_ref, lq2_ref, lk2_ref, g_ref, q_ref, k_ref, v_ref, o_ref, *, tq, tiles):
    hd = DA_HEAD_DIM
    nt = (((1,), (1,)), ((), ()))
    lam = (jnp.exp(jnp.sum(lq1_ref[...] * lk1_ref[...], axis=-1, keepdims=True))
           - jnp.exp(jnp.sum(lq2_ref[...] * lk2_ref[...], axis=-1, keepdims=True))
           + LAMBDA_INIT)

    def tile(n, part):
        off = n * tq
        rows = slice(part * tq, (part + 1) * tq)
        q = q_ref[0, rows, :]
        causal = (lax.broadcasted_iota(jnp.int32, (tq, tq), 0)
                  >= lax.broadcasted_iota(jnp.int32, (tq, tq), 1))
        p_off, p_dg, inv_l = [], [], []
        for mp in range(2):
            cs = slice(mp * hd, (mp + 1) * hd)
            sd = lax.dot_general(q[:, cs], k_ref[0, off:off + tq, cs], nt, preferred_element_type=F32)
            sd = jnp.where(causal, sd, -jnp.inf)
            m = jnp.max(sd, axis=-1, keepdims=True)
            if n > 0:
                so = lax.dot_general(q[:, cs], k_ref[0, 0:off, cs], nt, preferred_element_type=F32)
                m = jnp.maximum(m, jnp.max(so, axis=-1, keepdims=True))
                po = jnp.exp2(so - m)
                l = jnp.sum(po, axis=-1, keepdims=True)
                p_off.append(po.astype(BF16))
            else:
                l = 0.0
            pd = jnp.exp2(sd - m)
            l = l + jnp.sum(pd, axis=-1, keepdims=True)
            p_dg.append(pd.astype(BF16))
            inv_l.append(1.0 / l)
        acc = jnp.dot(jnp.concatenate(p_dg, axis=0), v_ref[0, off:off + tq, :], preferred_element_type=F32)
        if n > 0:
            acc = acc + jnp.dot(jnp.concatenate(p_off, axis=0), v_ref[0, 0:off, :],
                                preferred_element_type=F32)
        dlt = acc[:tq] * inv_l[0] - lam * (acc[tq:] * inv_l[1])
        y = _rms_scale(dlt, g_ref[...]) * (1.0 - LAMBDA_INIT)
        o_ref[0, rows, :] = y.astype(o_ref.dtype)

    for n, part in tiles:
        tile(n, part)


def _da_kernel(lq1_ref, lk1_ref, lq2_ref, lk2_ref, g_ref, q_ref, k_ref, v_ref, wu_src_ref, wd_src_ref,
               o_ref, wu_bf_ref, wd_bf_ref, *, tq, nq):
    wu_bf_ref[...] = wu_src_ref[...].astype(BF16)
    wd_bf_ref[...] = wd_src_ref[...].astype(BF16)
    _da_tiles(lq1_ref, lk1_ref, lq2_ref, lk2_ref, g_ref, q_ref, k_ref, v_ref, o_ref,
              tq=tq, tiles=[(n, n) for n in range(nq)])


def _diff_attention(p3, lq1, lk1, lq2, lk2, g_subln, w_up, w_down, tq=256):
    b, s, _ = p3.shape
    hw = 2 * DA_HEAD_DIM
    nq = s // tq
    n_steps = b * DA_HEADS
    up_rows = w_up.shape[0] // n_steps
    dn_rows = w_down.shape[0] // n_steps
    vec = lambda: pl.BlockSpec((1, DA_HEAD_DIM), lambda bi, h: (0, 0))
    chunk = lambda bi, h: (bi * DA_HEADS + h, 0)
    return pl.pallas_call(
        functools.partial(_da_kernel, tq=tq, nq=nq),
        grid=(b, DA_HEADS),
        in_specs=[vec(), vec(), vec(), vec(),
                  pl.BlockSpec((1, hw), lambda bi, h: (0, 0)),
                  pl.BlockSpec((1, s, hw), lambda bi, h: (bi, 0, Q_OFF // hw + h)),
                  pl.BlockSpec((1, s, hw), lambda bi, h: (bi, 0, K_OFF // hw + h)),
                  pl.BlockSpec((1, s, hw), lambda bi, h: (bi, 0, V_OFF // hw + h)),
                  pl.BlockSpec((up_rows, w_up.shape[1]), chunk),
                  pl.BlockSpec((dn_rows, w_down.shape[1]), chunk)],
        out_specs=[pl.BlockSpec((1, s, hw), lambda bi, h: (bi, 0, h)),
                   pl.BlockSpec((up_rows, w_up.shape[1]), chunk),
                   pl.BlockSpec((dn_rows, w_down.shape[1]), chunk)],
        out_shape=[jax.ShapeDtypeStruct((b, s, DA_WIDTH), BF16),
                   jax.ShapeDtypeStruct(w_up.shape, BF16),
                   jax.ShapeDtypeStruct(w_down.shape, BF16)],
        compiler_params=_cparams(2),
        name="diff_attention",
    )(lq1.reshape(1, -1), lk1.reshape(1, -1), lq2.reshape(1, -1), lk2.reshape(1, -1),
      g_subln.reshape(1, hw), p3, p3, p3, w_up, w_down)


def _proj_kernel(a_ref, w_ref, o_ref, wbf_ref):
    @pl.when(pl.program_id(1) == 0)
    def _():
        wbf_ref[...] = w_ref[...].astype(BF16)

    o_ref[...] = jnp.dot(a_ref[...], wbf_ref[...], preferred_element_type=F32).astype(o_ref.dtype)


def _proj(a, w, tm=1024, tn=1024):
    m, k = a.shape
    n = w.shape[1]
    return pl.pallas_call(
        _proj_kernel,
        grid=(n // tn, m // tm),
        in_specs=[pl.BlockSpec((tm, k), lambda j, i: (i, 0)),
                  pl.BlockSpec((k, tn), lambda j, i: (0, j))],
        out_specs=pl.BlockSpec((tm, tn), lambda j, i: (i, j)),
        out_shape=jax.ShapeDtypeStruct((m, n), BF16),
        scratch_shapes=[pltpu.VMEM((k, tn), BF16)],
        compiler_params=_cparams(2),
        name="proj",
    )(a, w)


def _xa_kernel(q_ref, k_ref, v_ref, o_ref, *, heads):
    nt = (((1,), (1,)), ((), ()))
    hd = XA_HEAD_DIM
    for hh in range(heads):
        cs = slice(hh * hd, (hh + 1) * hd)
        s = lax.dot_general(q_ref[0, :, cs], k_ref[0, :, cs], nt, preferred_element_type=F32)
        m = jnp.max(s, axis=-1, keepdims=True)
        p = jnp.exp2(s - m)
        l = jnp.sum(p, axis=-1, keepdims=True)
        o = jnp.dot(p.astype(BF16), v_ref[0, :, cs], preferred_element_type=F32)
        o_ref[0, :, cs] = (o * (1.0 / l)).astype(o_ref.dtype)


def _cross_attention(p3, kv3, heads=2):
    b, s, _ = p3.shape
    mem_len = kv3.shape[1]
    hd = XA_HEAD_DIM * heads
    groups = XA_HEADS // heads
    return pl.pallas_call(
        functools.partial(_xa_kernel, heads=heads),
        grid=(b, groups),
        in_specs=[pl.BlockSpec((1, s, hd), lambda bi, h: (bi, 0, XQ_OFF // hd + h)),
                  pl.BlockSpec((1, mem_len, hd), lambda bi, h: (bi, 0, h)),
                  pl.BlockSpec((1, mem_len, hd), lambda bi, h: (bi, 0, groups + h))],
        out_specs=pl.BlockSpec((1, s, hd), lambda bi, h: (bi, 0, h)),
        out_shape=jax.ShapeDtypeStruct((b, s, XA_WIDTH), BF16),
        compiler_params=_cparams(2),
        name="cross_attention",
    )(p3, kv3, kv3)


def _merge_kernel(oc_ref, od_ref, ox_ref, g0_ref, g1_ref, g2_ref, wc_ref, wd_ref, wx_ref, o_ref, wbf_ref):
    @pl.when(pl.program_id(1) == 0)
    def _():
        wbf_ref[0] = wc_ref[...].astype(BF16)
        wbf_ref[1] = wd_ref[...].astype(BF16)
        wbf_ref[2] = wx_ref[...].astype(BF16)

    y = g0_ref[...].astype(F32) * jnp.dot(oc_ref[...], wbf_ref[0], preferred_element_type=F32)
    y = y + g1_ref[...].astype(F32) * jnp.dot(od_ref[...], wbf_ref[1], preferred_element_type=F32)
    y = y + g2_ref[...].astype(F32) * jnp.dot(ox_ref[...], wbf_ref[2], preferred_element_type=F32)
    o_ref[...] = y.astype(o_ref.dtype)


def _merge(o_conv, o_da, o_xa, p2, w_conv_out, w_da_out, w_xa_out, tm=1024, tn=1024):
    t, kk = o_conv.shape
    gcol = GATE_OFF // tn
    gstride = D_MODEL // tn
    act = lambda: pl.BlockSpec((tm, kk), lambda j, i: (i, 0))
    wsp = lambda: pl.BlockSpec((kk, tn), lambda j, i: (0, j), pipeline_mode=pl.Buffered(1))
    gate = lambda br: pl.BlockSpec((tm, tn), lambda j, i: (i, gcol + br * gstride + j))
    return pl.pallas_call(
        _merge_kernel,
        grid=(D_MODEL // tn, t // tm),
        in_specs=[act(), act(), act(), gate(0), gate(1), gate(2), wsp(), wsp(), wsp()],
        out_specs=pl.BlockSpec((tm, tn), lambda j, i: (i, j)),
        out_shape=jax.ShapeDtypeStruct((t, D_MODEL), BF16),
        scratch_shapes=[pltpu.VMEM((3, kk, tn), BF16)],
        compiler_params=_cparams(2),
        name="merge",
    )(o_conv, o_da, o_xa, p2, p2, p2, w_conv_out, w_da_out, w_xa_out)


def _mix_kernel(a_ref, w_ref, x_ref, g_ref, x1_ref, h2_ref, wbf_ref):
    @pl.when(pl.program_id(0) == 0)
    def _():
        wbf_ref[...] = w_ref[...].astype(BF16)

    x1 = x_ref[...] + jnp.dot(a_ref[...], wbf_ref[...], preferred_element_type=F32)
    x1_ref[...] = x1
    h2_ref[...] = _rms_scale(x1, g_ref[...]).astype(h2_ref.dtype)


def _mix(merged, w_mix, x2d, g_mlp, tm=512):
    t, k = merged.shape
    n = w_mix.shape[1]
    return pl.pallas_call(
        _mix_kernel,
        grid=(t // tm,),
        in_specs=[pl.BlockSpec((tm, k), lambda i: (i, 0)),
                  pl.BlockSpec((k, n), lambda i: (0, 0), pipeline_mode=pl.Buffered(1)),
                  pl.BlockSpec((tm, n), lambda i: (i, 0)),
                  pl.BlockSpec((1, n), lambda i: (0, 0))],
        out_specs=[pl.BlockSpec((tm, n), lambda i: (i, 0)),
                   pl.BlockSpec((tm, n), lambda i: (i, 0))],
        out_shape=[jax.ShapeDtypeStruct((t, n), F32),
                   jax.ShapeDtypeStruct((t, n), BF16)],
        scratch_shapes=[pltpu.VMEM((k, n), BF16)],
        compiler_params=_cparams(1),
        name="mix_out",
    )(merged, w_mix, x2d, g_mlp.reshape(1, n))


def _mlp_kernel(h_ref, x_hbm, wu_ref, wd_ref, gf_ref, o_ref, xbuf_ref, xsem, *, nf, tm):
    i = pl.program_id(0)
    f = pl.program_id(1)

    def residual_copy():
        rows = pl.ds(pl.multiple_of(i * tm, tm), tm)
        return pltpu.make_async_copy(x_hbm.at[rows, :], xbuf_ref, xsem)

    @pl.when(f == nf - 2)
    def _():
        residual_copy().start()

    def ff_tile():
        hm = jnp.dot(h_ref[...], wu_ref[...], preferred_element_type=F32)
        hm = jnp.square(jnp.maximum(hm, 0.0)).astype(BF16)
        return jnp.dot(hm, wd_ref[...], preferred_element_type=F32)

    @pl.when(f == 0)
    def _():
        o_ref[...] = ff_tile()

    @pl.when((f > 0) & (f < nf - 1))
    def _():
        o_ref[...] += ff_tile()

    @pl.when(f == nf - 1)
    def _():
        residual_copy().wait()
        o_ref[...] = _rms_scale(xbuf_ref[...] + o_ref[...] + ff_tile(), gf_ref[...])


def _mlp(h2, x1, w_up, w_down, g_final, tm=1024, tf=1024):
    t, d = x1.shape
    ff = w_up.shape[1]
    nf = ff // tf
    assert nf >= 2
    return pl.pallas_call(
        functools.partial(_mlp_kernel, nf=nf, tm=tm),
        grid=(t // tm, nf),
        in_specs=[pl.BlockSpec((tm, d), lambda i, f: (i, 0)),
                  pl.BlockSpec(memory_space=pl.ANY),
                  pl.BlockSpec((d, tf), lambda i, f: (0, f)),
                  pl.BlockSpec((tf, d), lambda i, f: (f, 0)),
                  pl.BlockSpec((1, d), lambda i, f: (0, 0))],
        out_specs=pl.BlockSpec((tm, d), lambda i, f: (i, 0)),
        out_shape=jax.ShapeDtypeStruct((t, d), F32),
        scratch_shapes=[pltpu.VMEM((tm, d), F32), pltpu.SemaphoreType.DMA(())],
        compiler_params=_cparams(2),
        name="mlp",
    )(h2, x1, w_up, w_down, g_final.reshape(1, d))


def kernel(x, mem, positions, g_mix, w_in, w_dw, b_dw, g_conv_ln, b_conv_ln, w_conv_out, lambda_q1, lambda_k1, lambda_q2, lambda_k2, g_subln, w_da_out, g_mem, w_mem_kv, w_xa_out, w_mix_out, g_mlp, w_up, w_down, g_final):
    bsz, seq, d = x.shape
    t = bsz * seq
    x2d = x.reshape(t, d)

    h = _rmsnorm(x2d, g_mix[0], BF16)
    cos2, sin2 = _rope_tables(positions.astype(F32).reshape(t, 1))
    u, p2 = _in_proj(h, w_in[0], cos2, sin2)
    p3 = p2.reshape(bsz, seq, P_COLS)

    o_conv = _conv_branch(u.reshape(bsz, seq, CONV_CH), w_dw[0], b_dw[0], g_conv_ln[0], b_conv_ln[0])
    o_da, w_up_bf, w_down_bf = _diff_attention(p3, lambda_q1[0], lambda_k1[0], lambda_q2[0], lambda_k2[0],
                                               g_subln[0], w_up[0], w_down[0])

    mem_n = _rmsnorm(mem.reshape(-1, d), g_mem[0], BF16)
    kv = _proj(mem_n, w_mem_kv[0])
    o_xa = _cross_attention(p3, kv.reshape(bsz, -1, 2 * XA_WIDTH))

    merged = _merge(o_conv.reshape(t, CONV_CH), o_da.reshape(t, DA_WIDTH), o_xa.reshape(t, XA_WIDTH),
                    p2, w_conv_out[0], w_da_out[0], w_xa_out[0])
    x1, h2 = _mix(merged, w_mix_out[0], x2d, g_mlp[0])
    out = _mlp(h2, x1, w_up_bf, w_down_bf, g_final)
    return out.reshape(bsz, seq, d)
```

```python
import functools
import math

import jax
import jax.numpy as jnp
from jax import lax
from jax.experimental import pallas as pl
from jax.experimental.pallas import tpu as pltpu

F32 = jnp.float32
BF16 = jnp.bfloat16

D_MODEL = 2048
CONV_CH = 1024
CONV_WIDTH = 31
DA_HEADS = 4
DA_HEAD_DIM = 128
DA_WIDTH = 1024
XA_HEADS = 4
XA_HEAD_DIM = 256
XA_WIDTH = 1024
D_FF = 4 * D_MODEL
ROPE_THETA = 10000.0
NORM_EPS = 1e-6
LAMBDA_INIT = 0.8 - 0.6 * math.exp(-0.3 * 0)
LOG2E = math.log2(math.e)

IN_TILE = 1024
IN_HALF = IN_TILE // 2
N_GLU_TILES = CONV_CH // IN_HALF
P_COLS = 2 * 1024 + 1024 + XA_WIDTH + 3 * D_MODEL
Q_OFF, K_OFF, V_OFF, XQ_OFF, GATE_OFF = 0, 1024, 2048, 3072, 4096

V7X_VMEM_LIMIT = 62 * 1024 * 1024
CONV_HALO = 32
SUBLANES = 8
LANES = 128


def _cparams(n_axes):
    return pltpu.CompilerParams(
        dimension_semantics=("arbitrary",) * n_axes,
        vmem_limit_bytes=V7X_VMEM_LIMIT,
    )


def _sigmoid(x):
    return 0.5 * jnp.tanh(0.5 * x) + 0.5


def _rms_scale(x, g):
    ms = jnp.mean(x * x, axis=-1, keepdims=True)
    return x * lax.rsqrt(ms + NORM_EPS) * g


def _rmsnorm_kernel(x_ref, g_ref, o_ref):
    o_ref[...] = _rms_scale(x_ref[...], g_ref[...]).astype(o_ref.dtype)


def _rmsnorm(x2d, g, out_dtype, tm=2048):
    m, d = x2d.shape
    tm = min(tm, m)
    return pl.pallas_call(
        _rmsnorm_kernel,
        grid=(m // tm,),
        in_specs=[pl.BlockSpec((tm, d), lambda i: (i, 0)),
                  pl.BlockSpec((1, d), lambda i: (0, 0))],
        out_specs=pl.BlockSpec((tm, d), lambda i: (i, 0)),
        out_shape=jax.ShapeDtypeStruct((m, d), out_dtype),
        compiler_params=_cparams(1),
        name="rmsnorm",
    )(x2d, g.reshape(1, d))


def _rope_kernel(pos_lo_ref, pos_hi_ref, f_ref, cos_ref, sin_ref, *, half_rows):
    half = DA_HEAD_DIM // 2
    lane = lax.broadcasted_iota(jnp.int32, (half_rows, DA_HEAD_DIM), 1)
    left = lane < half
    ang = jnp.where(left, pos_lo_ref[...], pos_hi_ref[...]) * f_ref[...]
    c = jnp.cos(ang)
    s = jnp.sin(ang)
    c_sw = pltpu.roll(c, half, axis=1)
    s_sw = pltpu.roll(s, half, axis=1)
    cos_ref[0:half_rows, :] = jnp.where(left, c, c_sw)
    cos_ref[half_rows:, :] = jnp.where(left, c_sw, c)
    sin_ref[0:half_rows, :] = jnp.where(left, -s, s_sw)
    sin_ref[half_rows:, :] = jnp.where(left, -s_sw, s)


def _rope_tables(pos_col, tm=2048):
    t = pos_col.shape[0]
    inv_freq = 1.0 / (ROPE_THETA ** (jnp.arange(0, DA_HEAD_DIM, 2, dtype=F32) / DA_HEAD_DIM))
    f2 = jnp.concatenate([inv_freq, inv_freq]).reshape(1, DA_HEAD_DIM)
    half_rows = tm // 2
    return pl.pallas_call(
        functools.partial(_rope_kernel, half_rows=half_rows),
        grid=(t // tm,),
        in_specs=[pl.BlockSpec((half_rows, 1), lambda i: (2 * i, 0)),
                  pl.BlockSpec((half_rows, 1), lambda i: (2 * i + 1, 0)),
                  pl.BlockSpec((1, DA_HEAD_DIM), lambda i: (0, 0))],
        out_specs=[pl.BlockSpec((tm, DA_HEAD_DIM), lambda i: (i, 0))] * 2,
        out_shape=[jax.ShapeDtypeStruct((t, DA_HEAD_DIM), F32)] * 2,
        compiler_params=_cparams(1),
        name="rope_tables",
    )(pos_col, pos_col, f2)


def _in_proj_kernel(h_ref, wa_ref, wb_ref, cos_ref, sin_ref, u_ref, p_ref, wbf_ref, *, sub):
    j = pl.program_id(0)
    n = j - N_GLU_TILES
    blocks = [slice(r0, r0 + sub) for r0 in range(0, h_ref.shape[0], sub)]

    @pl.when(pl.program_id(1) == 0)
    def _():
        wbf_ref[:, :IN_HALF] = wa_ref[...].astype(BF16)
        wbf_ref[:, IN_HALF:] = wb_ref[...].astype(BF16)

    def proj(rows):
        return jnp.dot(h_ref[rows, :], wbf_ref[...], preferred_element_type=F32)

    @pl.when(j < N_GLU_TILES)
    def _():
        for rows in blocks:
            acc = proj(rows)
            u_ref[rows, :] = (acc[:, :IN_HALF] * _sigmoid(acc[:, IN_HALF:])).astype(u_ref.dtype)

    @pl.when((n >= 0) & (n < 2))
    def _():
        scale = jnp.where(n == 0, DA_HEAD_DIM ** -0.5 * LOG2E, 1.0).astype(F32)
        for rows in blocks:
            acc = proj(rows)
            c = cos_ref[rows, :] * scale
            s = sin_ref[rows, :] * scale
            for g in range(IN_TILE // DA_HEAD_DIM):
                sl = slice(g * DA_HEAD_DIM, (g + 1) * DA_HEAD_DIM)
                t = acc[:, sl]
                r = pltpu.roll(t, DA_HEAD_DIM // 2, axis=1)
                p_ref[rows, sl] = (t * c + r * s).astype(p_ref.dtype)

    @pl.when((n >= 2) & (n < 4))
    def _():
        scale = jnp.where(n == 3, XA_HEAD_DIM ** -0.5 * LOG2E, 1.0).astype(F32)
        for rows in blocks:
            p_ref[rows, :] = (proj(rows) * scale).astype(p_ref.dtype)

    @pl.when(n >= 4)
    def _():
        for rows in blocks:
            p_ref[rows, :] = _sigmoid(proj(rows)).astype(p_ref.dtype)


def _in_proj(h, w_in, cos2, sin2, tm=2048, sub=1024):
    t, d = h.shape
    n_i = t // tm
    glu = lambda j: j < N_GLU_TILES
    wa_map = lambda j, i: (0, jnp.where(glu(j), j, 2 * j))
    wb_map = lambda j, i: (0, jnp.where(glu(j), j + N_GLU_TILES, 2 * j + 1))
    u_map = lambda j, i: (jnp.where(glu(j), i, n_i - 1), jnp.where(glu(j), j, N_GLU_TILES - 1))
    p_map = lambda j, i: (jnp.where(glu(j), 0, i), jnp.where(glu(j), 0, j - N_GLU_TILES))
    return pl.pallas_call(
        functools.partial(_in_proj_kernel, sub=sub),
        grid=(N_GLU_TILES + P_COLS // IN_TILE, n_i),
        in_specs=[pl.BlockSpec((tm, d), lambda j, i: (i, 0)),
                  pl.BlockSpec((d, IN_HALF), wa_map),
                  pl.BlockSpec((d, IN_HALF), wb_map),
                  pl.BlockSpec((tm, DA_HEAD_DIM), lambda j, i: (i, 0)),
                  pl.BlockSpec((tm, DA_HEAD_DIM), lambda j, i: (i, 0))],
        out_specs=[pl.BlockSpec((tm, IN_HALF), u_map),
                   pl.BlockSpec((tm, IN_TILE), p_map)],
        out_shape=[jax.ShapeDtypeStruct((t, CONV_CH), BF16),
                   jax.ShapeDtypeStruct((t, P_COLS), BF16)],
        scratch_shapes=[pltpu.VMEM((d, IN_TILE), BF16)],
        compiler_params=_cparams(2),
        name="in_proj",
    )(h, w_in, w_in, cos2, sin2)


def _conv_kernel(ucur_ref, uprev_ref, w_ref, bdw_ref, g_ref, b_ref, o_ref, buf_ref, sh_ref, y_ref, *, ts, rg, rc):
    si = pl.program_id(1)
    prev = uprev_ref[0].astype(F32)
    buf_ref[0:CONV_HALO, :] = jnp.where(si > 0, prev, 0.0)
    buf_ref[CONV_HALO:CONV_HALO + ts, :] = ucur_ref[0].astype(F32)
    buf_ref[CONV_HALO + ts:, :] = jnp.zeros((SUBLANES, CONV_CH), F32)
    n_sh = ts + CONV_HALO
    for r in range(1, SUBLANES):
        sh_ref[r - 1] = buf_ref[pl.ds(r, n_sh), :]

    lead = CONV_HALO - (CONV_WIDTH - 1)
    rows_per_iter = rg * SUBLANES

    for c in range(CONV_CH // LANES):
        lanes = slice(c * LANES, (c + 1) * LANES)
        taps = [jnp.broadcast_to(w_ref[k:k + 1, lanes], (SUBLANES, LANES)) for k in range(CONV_WIDTH)]
        bias = jnp.broadcast_to(bdw_ref[:, lanes], (SUBLANES, LANES))

        def rows_body(it, carry, lanes=lanes, taps=taps, bias=bias):
            r0 = pl.multiple_of(it * rows_per_iter, rows_per_iter)
            for gi in range(rg):
                acc = bias
                for k in range(CONV_WIDTH):
                    a, r = divmod(lead + k, SUBLANES)
                    row0 = pl.multiple_of(r0 + SUBLANES * (a + gi), SUBLANES)
                    if r == 0:
                        rows = buf_ref[pl.ds(row0, SUBLANES), lanes]
                    else:
                        rows = sh_ref[r - 1, pl.ds(row0, SUBLANES), lanes]
                    acc = acc + rows * taps[k]
                y_ref[pl.ds(pl.multiple_of(r0 + SUBLANES * gi, SUBLANES), SUBLANES), lanes] = acc
            return carry

        lax.fori_loop(0, ts // rows_per_iter, rows_body, 0)

    for ci in range(ts // rc):
        rows = slice(ci * rc, (ci + 1) * rc)
        acc = y_ref[rows, :]
        mu = jnp.mean(acc, axis=-1, keepdims=True)
        cen = acc - mu
        var = jnp.mean(cen * cen, axis=-1, keepdims=True)
        y = cen * lax.rsqrt(var + NORM_EPS) * g_ref[...] + b_ref[...]
        o_ref[0, rows, :] = (y * _sigmoid(y)).astype(o_ref.dtype)


def _conv_branch(u3, w_dw, b_dw, g_ln, b_ln, ts=1024, rg=16, rc=128):
    b, s, c = u3.shape
    hb = ts // CONV_HALO
    return pl.pallas_call(
        functools.partial(_conv_kernel, ts=ts, rg=rg, rc=rc),
        grid=(b, s // ts),
        in_specs=[pl.BlockSpec((1, ts, c), lambda bi, si: (bi, si, 0)),
                  pl.BlockSpec((1, CONV_HALO, c), lambda bi, si: (bi, jnp.maximum(si * hb - 1, 0), 0)),
                  pl.BlockSpec((CONV_WIDTH, c), lambda bi, si: (0, 0)),
                  pl.BlockSpec((1, c), lambda bi, si: (0, 0)),
                  pl.BlockSpec((1, c), lambda bi, si: (0, 0)),
                  pl.BlockSpec((1, c), lambda bi, si: (0, 0))],
        out_specs=pl.BlockSpec((1, ts, c), lambda bi, si: (bi, si, 0)),
        out_shape=jax.ShapeDtypeStruct((b, s, c), BF16),
        scratch_shapes=[pltpu.VMEM((CONV_HALO + ts + SUBLANES, c), F32),
                        pltpu.VMEM((SUBLANES - 1, CONV_HALO + ts, c), F32),
                        pltpu.VMEM((ts, c), F32)],
        compiler_params=_cparams(2),
        name="conv_branch",
    )(u3, u3, w_dw, b_dw.reshape(1, c), g_ln.reshape(1, c), b_ln.reshape(1, c))


def _da_tiles(lq1_ref, lk1_ref, lq2_ref, lk2_ref, g_ref, q_ref, k_ref, v_ref, o_ref, *, tq, tiles):
    hd = DA_HEAD_DIM
    nt = (((1,), (1,)), ((), ()))
    lam = (jnp.exp(jnp.sum(lq1_ref[...] * lk1_ref[...], axis=-1, keepdims=True))
           - jnp.exp(jnp.sum(lq2_ref[...] * lk2_ref[...], axis=-1, keepdims=True))
           + LAMBDA_INIT)

    def tile(n, part):
        off = n * tq
        rows = slice(part * tq, (part + 1) * tq)
        q = q_ref[0, rows, :]
        causal = (lax.broadcasted_iota(jnp.int32, (tq, tq), 0)
                  >= lax.broadcasted_iota(jnp.int32, (tq, tq), 1))
        p_off, p_dg, inv_l = [], [], []
        for mp in range(2):
            cs = slice(mp * hd, (mp + 1) * hd)
            sd = lax.dot_general(q[:, cs], k_ref[0, off:off + tq, cs], nt, preferred_element_type=F32)
            sd = jnp.where(causal, sd, -jnp.inf)
            m = jnp.max(sd, axis=-1, keepdims=True)
            if n > 0:
                so = lax.dot_general(q[:, cs], k_ref[0, 0:off, cs], nt, preferred_element_type=F32)
                m = jnp.maximum(m, jnp.max(so, axis=-1, keepdims=True))
                po = jnp.exp2(so - m)
                l = jnp.sum(po, axis=-1, keepdims=True)
                p_off.append(po.astype(BF16))
            else:
                l = 0.0
            pd = jnp.exp2(sd - m)
            l = l + jnp.sum(pd, axis=-1, keepdims=True)
            p_dg.append(pd.astype(BF16))
            inv_l.append(1.0 / l)
        acc = jnp.dot(jnp.concatenate(p_dg, axis=0), v_ref[0, off:off + tq, :], preferred_element_type=F32)
        if n > 0:
            acc = acc + jnp.dot(jnp.concatenate(p_off, axis=0), v_ref[0, 0:off, :],
                                preferred_element_type=F32)
        dlt = acc[:tq] * inv_l[0] - lam * (acc[tq:] * inv_l[1])
        y = _rms_scale(dlt, g_ref[...]) * (1.0 - LAMBDA_INIT)
        o_ref[0, rows, :] = y.astype(o_ref.dtype)

    for n, part in tiles:
        tile(n, part)


def _da_kernel(lq1_ref, lk1_ref, lq2_ref, lk2_ref, g_ref, q_ref, k_ref, v_ref, *rest, tq, nq, n_cast):
    src_refs, o_ref, dst_refs = rest[:n_cast], rest[n_cast], rest[n_cast + 1:]
    for src_ref, dst_ref in zip(src_refs, dst_refs):
        dst_ref[...] = src_ref[...].astype(BF16)
    _da_tiles(lq1_ref, lk1_ref, lq2_ref, lk2_ref, g_ref, q_ref, k_ref, v_ref, o_ref,
              tq=tq, tiles=[(n, n) for n in range(nq)])


def _diff_attention(p3, lq1, lk1, lq2, lk2, g_subln, cast_weights, tq=256):
    b, s, _ = p3.shape
    hw = 2 * DA_HEAD_DIM
    nq = s // tq
    n_steps = b * DA_HEADS
    vec = lambda: pl.BlockSpec((1, DA_HEAD_DIM), lambda bi, h: (0, 0))
    chunk = lambda bi, h: (bi * DA_HEADS + h, 0)
    cast_specs = [pl.BlockSpec((w.shape[0] // n_steps, w.shape[1]), chunk) for w in cast_weights]
    return pl.pallas_call(
        functools.partial(_da_kernel, tq=tq, nq=nq, n_cast=len(cast_weights)),
        grid=(b, DA_HEADS),
        in_specs=[vec(), vec(), vec(), vec(),
                  pl.BlockSpec((1, hw), lambda bi, h: (0, 0)),
                  pl.BlockSpec((1, s, hw), lambda bi, h: (bi, 0, Q_OFF // hw + h)),
                  pl.BlockSpec((1, s, hw), lambda bi, h: (bi, 0, K_OFF // hw + h)),
                  pl.BlockSpec((1, s, hw), lambda bi, h: (bi, 0, V_OFF // hw + h))] + cast_specs,
        out_specs=[pl.BlockSpec((1, s, hw), lambda bi, h: (bi, 0, h))] + cast_specs,
        out_shape=[jax.ShapeDtypeStruct((b, s, DA_WIDTH), BF16)]
                  + [jax.ShapeDtypeStruct(w.shape, BF16) for w in cast_weights],
        compiler_params=_cparams(2),
        name="diff_attention",
    )(lq1.reshape(1, -1), lk1.reshape(1, -1), lq2.reshape(1, -1), lk2.reshape(1, -1),
      g_subln.reshape(1, hw), p3, p3, p3, *cast_weights)


def _mem_kv_kernel(mem_ref, g_ref, w_ref, o_ref, wbf_ref):
    @pl.when(pl.program_id(1) == 0)
    def _():
        wbf_ref[...] = w_ref[...].astype(BF16)

    mem_n = _rms_scale(mem_ref[0], g_ref[...]).astype(BF16)
    o_ref[0] = jnp.dot(mem_n, wbf_ref[...], preferred_element_type=F32).astype(o_ref.dtype)


def _mem_kv(mem, g_mem, w_kv, tn=1024):
    b, m, k = mem.shape
    n = w_kv.shape[1]
    return pl.pallas_call(
        _mem_kv_kernel,
        grid=(n // tn, b),
        in_specs=[pl.BlockSpec((1, m, k), lambda j, bi: (bi, 0, 0)),
                  pl.BlockSpec((1, k), lambda j, bi: (0, 0)),
                  pl.BlockSpec((k, tn), lambda j, bi: (0, j))],
        out_specs=pl.BlockSpec((1, m, tn), lambda j, bi: (bi, 0, j)),
        out_shape=jax.ShapeDtypeStruct((b, m, n), BF16),
        scratch_shapes=[pltpu.VMEM((k, tn), BF16)],
        compiler_params=_cparams(2),
        name="mem_kv",
    )(mem, g_mem.reshape(1, k), w_kv)


def _xa_kernel(q_ref, k_ref, v_ref, o_ref, *, heads):
    nt = (((1,), (1,)), ((), ()))
    hd = XA_HEAD_DIM
    for hh in range(heads):
        cs = slice(hh * hd, (hh + 1) * hd)
        s = lax.dot_general(q_ref[0, :, cs], k_ref[0, :, cs], nt, preferred_element_type=F32)
        m = jnp.max(s, axis=-1, keepdims=True)
        p = jnp.exp2(s - m)
        l = jnp.sum(p, axis=-1, keepdims=True)
        o = jnp.dot(p.astype(BF16), v_ref[0, :, cs], preferred_element_type=F32)
        o_ref[0, :, cs] = (o * (1.0 / l)).astype(o_ref.dtype)


def _cross_attention(p3, kv3, heads=2):
    b, s, _ = p3.shape
    mem_len = kv3.shape[1]
    hd = XA_HEAD_DIM * heads
    groups = XA_HEADS // heads
    return pl.pallas_call(
        functools.partial(_xa_kernel, heads=heads),
        grid=(b, groups),
        in_specs=[pl.BlockSpec((1, s, hd), lambda bi, h: (bi, 0, XQ_OFF // hd + h)),
                  pl.BlockSpec((1, mem_len, hd), lambda bi, h: (bi, 0, h)),
                  pl.BlockSpec((1, mem_len, hd), lambda bi, h: (bi, 0, groups + h))],
        out_specs=pl.BlockSpec((1, s, hd), lambda bi, h: (bi, 0, h)),
        out_shape=jax.ShapeDtypeStruct((b, s, XA_WIDTH), BF16),
        compiler_params=_cparams(2),
        name="cross_attention",
    )(p3, kv3, kv3)


def _merge_kernel(oc_ref, od_ref, ox_ref, g0_ref, g1_ref, g2_ref, wc_ref, wd_ref, wx_ref, o_ref, *, sub):
    branches = ((oc_ref, wc_ref, g0_ref), (od_ref, wd_ref, g1_ref), (ox_ref, wx_ref, g2_ref))
    for r0 in range(0, o_ref.shape[0], sub):
        rows = slice(r0, r0 + sub)
        y = None
        for a_ref, w_ref, g_ref in branches:
            term = g_ref[rows, :].astype(F32) * jnp.dot(a_ref[rows, :], w_ref[...], preferred_element_type=F32)
            y = term if y is None else y + term
        o_ref[rows, :] = y.astype(o_ref.dtype)


def _merge(o_conv, o_da, o_xa, p2, wc_bf, wd_bf, wx_bf, tm=512, sub=256):
    t, kk = o_conv.shape
    n = wc_bf.shape[1]
    gcol = GATE_OFF // n
    act = lambda: pl.BlockSpec((tm, kk), lambda i: (i, 0))
    wsp = lambda: pl.BlockSpec((kk, n), lambda i: (0, 0), pipeline_mode=pl.Buffered(1))
    gate = lambda br: pl.BlockSpec((tm, n), lambda i: (i, gcol + br))
    return pl.pallas_call(
        functools.partial(_merge_kernel, sub=sub),
        grid=(t // tm,),
        in_specs=[act(), act(), act(), gate(0), gate(1), gate(2), wsp(), wsp(), wsp()],
        out_specs=pl.BlockSpec((tm, n), lambda i: (i, 0)),
        out_shape=jax.ShapeDtypeStruct((t, n), BF16),
        compiler_params=_cparams(1),
        name="merge",
    )(o_conv, o_da, o_xa, p2, p2, p2, wc_bf, wd_bf, wx_bf)


def _mix_kernel(a_ref, w_ref, x_ref, g_ref, x1_ref, h2_ref, *, sub):
    for r0 in range(0, a_ref.shape[0], sub):
        rows = slice(r0, r0 + sub)
        x1 = x_ref[rows, :] + jnp.dot(a_ref[rows, :], w_ref[...], preferred_element_type=F32)
        x1_ref[rows, :] = x1
        h2_ref[rows, :] = _rms_scale(x1, g_ref[...]).astype(h2_ref.dtype)


def _mix(merged, w_mix_bf, x2d, g_mlp, tm=1024, sub=512):
    t, k = merged.shape
    n = w_mix_bf.shape[1]
    return pl.pallas_call(
        functools.partial(_mix_kernel, sub=sub),
        grid=(t // tm,),
        in_specs=[pl.BlockSpec((tm, k), lambda i: (i, 0)),
                  pl.BlockSpec((k, n), lambda i: (0, 0), pipeline_mode=pl.Buffered(1)),
                  pl.BlockSpec((tm, n), lambda i: (i, 0)),
                  pl.BlockSpec((1, n), lambda i: (0, 0))],
        out_specs=[pl.BlockSpec((tm, n), lambda i: (i, 0)),
                   pl.BlockSpec((tm, n), lambda i: (i, 0))],
        out_shape=[jax.ShapeDtypeStruct((t, n), F32),
                   jax.ShapeDtypeStruct((t, n), BF16)],
        compiler_params=_cparams(1),
        name="mix_out",
    )(merged, w_mix_bf, x2d, g_mlp.reshape(1, n))


def _mlp_kernel(h_ref, x_hbm, wu_ref, wd_ref, gf_ref, o_ref, xbuf_ref, xsem, *, nf, tm):
    i = pl.program_id(0)
    f = pl.program_id(1)

    def residual_copy():
        rows = pl.ds(pl.multiple_of(i * tm, tm), tm)
        return pltpu.make_async_copy(x_hbm.at[rows, :], xbuf_ref, xsem)

    @pl.when(f == nf - 2)
    def _():
        residual_copy().start()

    def ff_tile():
        hm = jnp.dot(h_ref[...], wu_ref[...], preferred_element_type=F32)
        hm = jnp.square(jnp.maximum(hm, 0.0)).astype(BF16)
        return jnp.dot(hm, wd_ref[...], preferred_element_type=F32)

    @pl.when(f == 0)
    def _():
        o_ref[...] = ff_tile()

    @pl.when((f > 0) & (f < nf - 1))
    def _():
        o_ref[...] += ff_tile()

    @pl.when(f == nf - 1)
    def _():
        residual_copy().wait()
        o_ref[...] = _rms_scale(xbuf_ref[...] + o_ref[...] + ff_tile(), gf_ref[...])


def _mlp(h2, x1, w_up, w_down, g_final, tm=1024, tf=1024):
    t, d = x1.shape
    ff = w_up.shape[1]
    nf = ff // tf
    assert nf >= 2
    return pl.pallas_call(
        functools.partial(_mlp_kernel, nf=nf, tm=tm),
        grid=(t // tm, nf),
        in_specs=[pl.BlockSpec((tm, d), lambda i, f: (i, 0)),
                  pl.BlockSpec(memory_space=pl.ANY),
                  pl.BlockSpec((d, tf), lambda i, f: (0, f)),
                  pl.BlockSpec((tf, d), lambda i, f: (f, 0)),
                  pl.BlockSpec((1, d), lambda i, f: (0, 0))],
        out_specs=pl.BlockSpec((tm, d), lambda i, f: (i, 0)),
        out_shape=jax.ShapeDtypeStruct((t, d), F32),
        scratch_shapes=[pltpu.VMEM((tm, d), F32), pltpu.SemaphoreType.DMA(())],
        compiler_params=_cparams(2),
        name="mlp",
    )(h2, x1, w_up, w_down, g_final.reshape(1, d))


def kernel(x, mem, positions, g_mix, w_in, w_dw, b_dw, g_conv_ln, b_conv_ln, w_conv_out, lambda_q1, lambda_k1, lambda_q2, lambda_k2, g_subln, w_da_out, g_mem, w_mem_kv, w_xa_out, w_mix_out, g_mlp, w_up, w_down, g_final):
    bsz, seq, d = x.shape
    t = bsz * seq
    x2d = x.reshape(t, d)

    h = _rmsnorm(x2d, g_mix[0], BF16)
    cos2, sin2 = _rope_tables(positions.astype(F32).reshape(t, 1))
    u, p2 = _in_proj(h, w_in[0], cos2, sin2)
    p3 = p2.reshape(bsz, seq, P_COLS)

    o_conv = _conv_branch(u.reshape(bsz, seq, CONV_CH), w_dw[0], b_dw[0], g_conv_ln[0], b_conv_ln[0])
    o_da, w_up_bf, w_down_bf, wc_bf, wd_bf, wx_bf, w_mix_bf = _diff_attention(
        p3, lambda_q1[0], lambda_k1[0], lambda_q2[0], lambda_k2[0], g_subln[0],
        (w_up[0], w_down[0], w_conv_out[0], w_da_out[0], w_xa_out[0], w_mix_out[0]))

    kv3 = _mem_kv(mem, g_mem[0], w_mem_kv[0])
    o_xa = _cross_attention(p3, kv3)

    merged = _merge(o_conv.reshape(t, CONV_CH), o_da.reshape(t, DA_WIDTH), o_xa.reshape(t, XA_WIDTH),
                    p2, wc_bf, wd_bf, wx_bf)
    x1, h2 = _mix(merged, w_mix_bf, x2d, g_mlp[0])
    out = _mlp(h2, x1, w_up_bf, w_down_bf, g_final)
    return out.reshape(bsz, seq, d)
```

```python
import functools
import math

import jax
import jax.numpy as jnp
from jax import lax
from jax.experimental import pallas as pl
from jax.experimental.pallas import tpu as pltpu

F32 = jnp.float32
BF16 = jnp.bfloat16

D_MODEL = 2048
CONV_CH = 1024
CONV_WIDTH = 31
DA_HEADS = 4
DA_HEAD_DIM = 128
DA_WIDTH = 1024
XA_HEADS = 4
XA_HEAD_DIM = 256
XA_WIDTH = 1024
D_FF = 4 * D_MODEL
ROPE_THETA = 10000.0
NORM_EPS = 1e-6
LAMBDA_INIT = 0.8 - 0.6 * math.exp(-0.3 * 0)
LOG2E = math.log2(math.e)

IN_TILE = 1024
IN_HALF = IN_TILE // 2
N_GLU_TILES = CONV_CH // IN_HALF
P_COLS = 2 * 1024 + 1024 + XA_WIDTH + 3 * D_MODEL
Q_OFF, K_OFF, V_OFF, XQ_OFF, GATE_OFF = 0, 1024, 2048, 3072, 4096

V7X_VMEM_LIMIT = 62 * 1024 * 1024
CONV_HALO = 32
SUBLANES = 8
LANES = 128


def _cparams(n_axes):
    return pltpu.CompilerParams(
        dimension_semantics=("arbitrary",) * n_axes,
        vmem_limit_bytes=V7X_VMEM_LIMIT,
    )


def _sigmoid(x):
    return 0.5 * jnp.tanh(0.5 * x) + 0.5


def _rms_scale(x, g):
    ms = jnp.mean(x * x, axis=-1, keepdims=True)
    return x * lax.rsqrt(ms + NORM_EPS) * g


def _rmsnorm_kernel(x_ref, g_ref, o_ref):
    o_ref[...] = _rms_scale(x_ref[...], g_ref[...]).astype(o_ref.dtype)


def _rmsnorm(x2d, g, out_dtype, tm=512):
    m, d = x2d.shape
    tm = min(tm, m)
    return pl.pallas_call(
        _rmsnorm_kernel,
        grid=(m // tm,),
        in_specs=[pl.BlockSpec((tm, d), lambda i: (i, 0)),
                  pl.BlockSpec((1, d), lambda i: (0, 0))],
        out_specs=pl.BlockSpec((tm, d), lambda i: (i, 0)),
        out_shape=jax.ShapeDtypeStruct((m, d), out_dtype),
        compiler_params=_cparams(1),
        name="rmsnorm",
    )(x2d, g.reshape(1, d))


def _rope_kernel(pos_lo_ref, pos_hi_ref, f_ref, cos_ref, sin_ref, *, half_rows):
    half = DA_HEAD_DIM // 2
    lane = lax.broadcasted_iota(jnp.int32, (half_rows, DA_HEAD_DIM), 1)
    left = lane < half
    ang = jnp.where(left, pos_lo_ref[...], pos_hi_ref[...]) * f_ref[...]
    c = jnp.cos(ang)
    s = jnp.sin(ang)
    c_sw = pltpu.roll(c, half, axis=1)
    s_sw = pltpu.roll(s, half, axis=1)
    cos_ref[0:half_rows, :] = jnp.where(left, c, c_sw)
    cos_ref[half_rows:, :] = jnp.where(left, c_sw, c)
    sin_ref[0:half_rows, :] = jnp.where(left, -s, s_sw)
    sin_ref[half_rows:, :] = jnp.where(left, -s_sw, s)


def _rope_tables(pos_col, tm=1024):
    t = pos_col.shape[0]
    inv_freq = 1.0 / (ROPE_THETA ** (jnp.arange(0, DA_HEAD_DIM, 2, dtype=F32) / DA_HEAD_DIM))
    f2 = jnp.concatenate([inv_freq, inv_freq]).reshape(1, DA_HEAD_DIM)
    half_rows = tm // 2
    return pl.pallas_call(
        functools.partial(_rope_kernel, half_rows=half_rows),
        grid=(t // tm,),
        in_specs=[pl.BlockSpec((half_rows, 1), lambda i: (2 * i, 0)),
                  pl.BlockSpec((half_rows, 1), lambda i: (2 * i + 1, 0)),
                  pl.BlockSpec((1, DA_HEAD_DIM), lambda i: (0, 0))],
        out_specs=[pl.BlockSpec((tm, DA_HEAD_DIM), lambda i: (i, 0))] * 2,
        out_shape=[jax.ShapeDtypeStruct((t, DA_HEAD_DIM), F32)] * 2,
        compiler_params=_cparams(1),
        name="rope_tables",
    )(pos_col, pos_col, f2)


def _in_proj_kernel(h_ref, wa_ref, wb_ref, cos_ref, sin_ref, u_ref, p_ref, wbf_ref, *, sub):
    j = pl.program_id(0)
    n = j - N_GLU_TILES
    blocks = [slice(r0, r0 + sub) for r0 in range(0, h_ref.shape[0], sub)]

    @pl.when(pl.program_id(1) == 0)
    def _():
        wbf_ref[:, :IN_HALF] = wa_ref[...].astype(BF16)
        wbf_ref[:, IN_HALF:] = wb_ref[...].astype(BF16)

    def proj(rows):
        return jnp.dot(h_ref[rows, :], wbf_ref[...], preferred_element_type=F32)

    @pl.when(j < N_GLU_TILES)
    def _():
        for rows in blocks:
            acc = proj(rows)
            u_ref[rows, :] = (acc[:, :IN_HALF] * _sigmoid(acc[:, IN_HALF:])).astype(u_ref.dtype)

    @pl.when((n >= 0) & (n < 2))
    def _():
        scale = jnp.where(n == 0, DA_HEAD_DIM ** -0.5 * LOG2E, 1.0).astype(F32)
        for rows in blocks:
            acc = proj(rows)
            c = cos_ref[rows, :] * scale
            s = sin_ref[rows, :] * scale
            for g in range(IN_TILE // DA_HEAD_DIM):
                sl = slice(g * DA_HEAD_DIM, (g + 1) * DA_HEAD_DIM)
                t = acc[:, sl]
                r = pltpu.roll(t, DA_HEAD_DIM // 2, axis=1)
                p_ref[rows, sl] = (t * c + r * s).astype(p_ref.dtype)

    @pl.when((n >= 2) & (n < 4))
    def _():
        scale = jnp.where(n == 3, XA_HEAD_DIM ** -0.5 * LOG2E, 1.0).astype(F32)
        for rows in blocks:
            p_ref[rows, :] = (proj(rows) * scale).astype(p_ref.dtype)

    @pl.when(n >= 4)
    def _():
        for rows in blocks:
            p_ref[rows, :] = _sigmoid(proj(rows)).astype(p_ref.dtype)


def _in_proj(h, w_in, cos2, sin2, tm=2048, sub=1024):
    t, d = h.shape
    n_i = t // tm
    glu = lambda j: j < N_GLU_TILES
    wa_map = lambda j, i: (0, jnp.where(glu(j), j, 2 * j))
    wb_map = lambda j, i: (0, jnp.where(glu(j), j + N_GLU_TILES, 2 * j + 1))
    u_map = lambda j, i: (jnp.where(glu(j), i, n_i - 1), jnp.where(glu(j), j, N_GLU_TILES - 1))
    p_map = lambda j, i: (jnp.where(glu(j), 0, i), jnp.where(glu(j), 0, j - N_GLU_TILES))
    return pl.pallas_call(
        functools.partial(_in_proj_kernel, sub=sub),
        grid=(N_GLU_TILES + P_COLS // IN_TILE, n_i),
        in_specs=[pl.BlockSpec((tm, d), lambda j, i: (i, 0)),
                  pl.BlockSpec((d, IN_HALF), wa_map),
                  pl.BlockSpec((d, IN_HALF), wb_map),
                  pl.BlockSpec((tm, DA_HEAD_DIM), lambda j, i: (i, 0)),
                  pl.BlockSpec((tm, DA_HEAD_DIM), lambda j, i: (i, 0))],
        out_specs=[pl.BlockSpec((tm, IN_HALF), u_map),
                   pl.BlockSpec((tm, IN_TILE), p_map)],
        out_shape=[jax.ShapeDtypeStruct((t, CONV_CH), BF16),
                   jax.ShapeDtypeStruct((t, P_COLS), BF16)],
        scratch_shapes=[pltpu.VMEM((d, IN_TILE), BF16)],
        compiler_params=_cparams(2),
        name="in_proj",
    )(h, w_in, w_in, cos2, sin2)


def _conv_kernel(ucur_ref, uprev_ref, w_ref, bdw_ref, g_ref, b_ref, wu_src_ref, wd_src_ref,
                 o_ref, wu_bf_ref, wd_bf_ref, buf_ref, sh_ref, y_ref, *, ts, rg, rc):
    si = pl.program_id(1)
    prev = uprev_ref[0].astype(F32)
    buf_ref[0:CONV_HALO, :] = jnp.where(si > 0, prev, 0.0)
    buf_ref[CONV_HALO:CONV_HALO + ts, :] = ucur_ref[0].astype(F32)
    buf_ref[CONV_HALO + ts:, :] = jnp.zeros((SUBLANES, CONV_CH), F32)
    n_sh = ts + CONV_HALO
    for r in range(1, SUBLANES):
        sh_ref[r - 1] = buf_ref[pl.ds(r, n_sh), :]

    lead = CONV_HALO - (CONV_WIDTH - 1)
    rows_per_iter = rg * SUBLANES

    for c in range(CONV_CH // LANES):
        lanes = slice(c * LANES, (c + 1) * LANES)
        taps = [jnp.broadcast_to(w_ref[k:k + 1, lanes], (SUBLANES, LANES)) for k in range(CONV_WIDTH)]
        bias = jnp.broadcast_to(bdw_ref[:, lanes], (SUBLANES, LANES))

        def rows_body(it, carry, lanes=lanes, taps=taps, bias=bias):
            r0 = pl.multiple_of(it * rows_per_iter, rows_per_iter)
            for gi in range(rg):
                acc = bias
                for k in range(CONV_WIDTH):
                    a, r = divmod(lead + k, SUBLANES)
                    row0 = pl.multiple_of(r0 + SUBLANES * (a + gi), SUBLANES)
                    if r == 0:
                        rows = buf_ref[pl.ds(row0, SUBLANES), lanes]
                    else:
                        rows = sh_ref[r - 1, pl.ds(row0, SUBLANES), lanes]
                    acc = acc + rows * taps[k]
                y_ref[pl.ds(pl.multiple_of(r0 + SUBLANES * gi, SUBLANES), SUBLANES), lanes] = acc
            return carry

        lax.fori_loop(0, ts // rows_per_iter, rows_body, 0)

    wu_bf_ref[...] = wu_src_ref[...].astype(BF16)
    wd_bf_ref[...] = wd_src_ref[...].astype(BF16)

    for ci in range(ts // rc):
        rows = slice(ci * rc, (ci + 1) * rc)
        acc = y_ref[rows, :]
        mu = jnp.mean(acc, axis=-1, keepdims=True)
        cen = acc - mu
        var = jnp.mean(cen * cen, axis=-1, keepdims=True)
        y = cen * lax.rsqrt(var + NORM_EPS) * g_ref[...] + b_ref[...]
        o_ref[0, rows, :] = (y * _sigmoid(y)).astype(o_ref.dtype)


def _conv_branch(u3, w_dw, b_dw, g_ln, b_ln, w_up, w_down, ts=512, rg=16, rc=128):
    b, s, c = u3.shape
    hb = ts // CONV_HALO
    n_s = s // ts
    chunk = lambda bi, si: (bi * n_s + si, 0)
    up_spec = pl.BlockSpec((w_up.shape[0] // (b * n_s), w_up.shape[1]), chunk)
    dn_spec = pl.BlockSpec((w_down.shape[0] // (b * n_s), w_down.shape[1]), chunk)
    return pl.pallas_call(
        functools.partial(_conv_kernel, ts=ts, rg=rg, rc=rc),
        grid=(b, n_s),
        in_specs=[pl.BlockSpec((1, ts, c), lambda bi, si: (bi, si, 0)),
                  pl.BlockSpec((1, CONV_HALO, c), lambda bi, si: (bi, jnp.maximum(si * hb - 1, 0), 0)),
                  pl.BlockSpec((CONV_WIDTH, c), lambda bi, si: (0, 0)),
                  pl.BlockSpec((1, c), lambda bi, si: (0, 0)),
                  pl.BlockSpec((1, c), lambda bi, si: (0, 0)),
                  pl.BlockSpec((1, c), lambda bi, si: (0, 0)),
                  up_spec, dn_spec],
        out_specs=[pl.BlockSpec((1, ts, c), lambda bi, si: (bi, si, 0)), up_spec, dn_spec],
        out_shape=[jax.ShapeDtypeStruct((b, s, c), BF16),
                   jax.ShapeDtypeStruct(w_up.shape, BF16),
                   jax.ShapeDtypeStruct(w_down.shape, BF16)],
        scratch_shapes=[pltpu.VMEM((CONV_HALO + ts + SUBLANES, c), F32),
                        pltpu.VMEM((SUBLANES - 1, CONV_HALO + ts, c), F32),
                        pltpu.VMEM((ts, c), F32)],
        compiler_params=_cparams(2),
        name="conv_branch",
    )(u3, u3, w_dw, b_dw.reshape(1, c), g_ln.reshape(1, c), b_ln.reshape(1, c), w_up, w_down)


def _da_tiles(lq1_ref, lk1_ref, lq2_ref, lk2_ref, g_ref, q_ref, k_ref, v_ref, o_ref, *, tq, tiles):
    hd = DA_HEAD_DIM
    nt = (((1,), (1,)), ((), ()))
    lam = (jnp.exp(jnp.sum(lq1_ref[...] * lk1_ref[...], axis=-1, keepdims=True))
           - jnp.exp(jnp.sum(lq2_ref[...] * lk2_ref[...], axis=-1, keepdims=True))
           + LAMBDA_INIT)

    def tile(n, part):
        off = n * tq
        rows = slice(part * tq, (part + 1) * tq)
        q = q_ref[0, rows, :]
        causal = (lax.broadcasted_iota(jnp.int32, (tq, tq), 0)
                  >= lax.broadcasted_iota(jnp.int32, (tq, tq), 1))
        p_off, p_dg, inv_l = [], [], []
        for mp in range(2):
            cs = slice(mp * hd, (mp + 1) * hd)
            sd = lax.dot_general(q[:, cs], k_ref[0, off:off + tq, cs], nt, preferred_element_type=F32)
            sd = jnp.where(causal, sd, -jnp.inf)
            m = jnp.max(sd, axis=-1, keepdims=True)
            if n > 0:
                so = lax.dot_general(q[:, cs], k_ref[0, 0:off, cs], nt, preferred_element_type=F32)
                m = jnp.maximum(m, jnp.max(so, axis=-1, keepdims=True))
                po = jnp.exp2(so - m)
                l = jnp.sum(po, axis=-1, keepdims=True)
                p_off.append(po.astype(BF16))
            else:
                l = 0.0
            pd = jnp.exp2(sd - m)
            l = l + jnp.sum(pd, axis=-1, keepdims=True)
            p_dg.append(pd.astype(BF16))
            inv_l.append(1.0 / l)
        acc = jnp.dot(jnp.concatenate(p_dg, axis=0), v_ref[0, off:off + tq, :], preferred_element_type=F32)
        if n > 0:
            acc = acc + jnp.dot(jnp.concatenate(p_off, axis=0), v_ref[0, 0:off, :],
                                preferred_element_type=F32)
        dlt = acc[:tq] * inv_l[0] - lam * (acc[tq:] * inv_l[1])
        y = _rms_scale(dlt, g_ref[...]) * (1.0 - LAMBDA_INIT)
        o_ref[0, rows, :] = y.astype(o_ref.dtype)

    for n, part in tiles:
        tile(n, part)


def _da_kernel(lq1_ref, lk1_ref, lq2_ref, lk2_ref, g_ref, q_ref, k_ref, v_ref, *rest, tq, nq, n_cast):
    src_refs, o_ref, dst_refs = rest[:n_cast], rest[n_cast], rest[n_cast + 1:]
    for src_ref, dst_ref in zip(src_refs, dst_refs):
        dst_ref[...] = src_ref[...].astype(BF16)
    _da_tiles(lq1_ref, lk1_ref, lq2_ref, lk2_ref, g_ref, q_ref, k_ref, v_ref, o_ref,
              tq=tq, tiles=[(n, n) for n in range(nq)])


def _diff_attention(p3, lq1, lk1, lq2, lk2, g_subln, cast_weights, tq=256):
    b, s, _ = p3.shape
    hw = 2 * DA_HEAD_DIM
    nq = s // tq
    n_steps = b * DA_HEADS
    vec = lambda: pl.BlockSpec((1, DA_HEAD_DIM), lambda bi, h: (0, 0))
    chunk = lambda bi, h: (bi * DA_HEADS + h, 0)
    cast_specs = [pl.BlockSpec((w.shape[0] // n_steps, w.shape[1]), chunk) for w in cast_weights]
    return pl.pallas_call(
        functools.partial(_da_kernel, tq=tq, nq=nq, n_cast=len(cast_weights)),
        grid=(b, DA_HEADS),
        in_specs=[vec(), vec(), vec(), vec(),
                  pl.BlockSpec((1, hw), lambda bi, h: (0, 0)),
                  pl.BlockSpec((1, s, hw), lambda bi, h: (bi, 0, Q_OFF // hw + h)),
                  pl.BlockSpec((1, s, hw), lambda bi, h: (bi, 0, K_OFF // hw + h)),
                  pl.BlockSpec((1, s, hw), lambda bi, h: (bi, 0, V_OFF // hw + h))] + cast_specs,
        out_specs=[pl.BlockSpec((1, s, hw), lambda bi, h: (bi, 0, h))] + cast_specs,
        out_shape=[jax.ShapeDtypeStruct((b, s, DA_WIDTH), BF16)]
                  + [jax.ShapeDtypeStruct(w.shape, BF16) for w in cast_weights],
        compiler_params=_cparams(2),
        name="diff_attention",
    )(lq1.reshape(1, -1), lk1.reshape(1, -1), lq2.reshape(1, -1), lk2.reshape(1, -1),
      g_subln.reshape(1, hw), p3, p3, p3, *cast_weights)


def _mem_kv_kernel(mem_ref, g_ref, w_ref, o_ref, wbf_ref):
    @pl.when(pl.program_id(1) == 0)
    def _():
        wbf_ref[...] = w_ref[...].astype(BF16)

    mem_n = _rms_scale(mem_ref[0], g_ref[...]).astype(BF16)
    o_ref[0] = jnp.dot(mem_n, wbf_ref[...], preferred_element_type=F32).astype(o_ref.dtype)


def _mem_kv(mem, g_mem, w_kv, tn=1024):
    b, m, k = mem.shape
    n = w_kv.shape[1]
    return pl.pallas_call(
        _mem_kv_kernel,
        grid=(n // tn, b),
        in_specs=[pl.BlockSpec((1, m, k), lambda j, bi: (bi, 0, 0)),
                  pl.BlockSpec((1, k), lambda j, bi: (0, 0)),
                  pl.BlockSpec((k, tn), lambda j, bi: (0, j))],
        out_specs=pl.BlockSpec((1, m, tn), lambda j, bi: (bi, 0, j)),
        out_shape=jax.ShapeDtypeStruct((b, m, n), BF16),
        scratch_shapes=[pltpu.VMEM((k, tn), BF16)],
        compiler_params=_cparams(2),
        name="mem_kv",
    )(mem, g_mem.reshape(1, k), w_kv)


def _xa_kernel(q_ref, k_ref, v_ref, o_ref, *, heads):
    nt = (((1,), (1,)), ((), ()))
    hd = XA_HEAD_DIM
    for hh in range(heads):
        cs = slice(hh * hd, (hh + 1) * hd)
        s = lax.dot_general(q_ref[0, :, cs], k_ref[0, :, cs], nt, preferred_element_type=F32)
        m = jnp.max(s, axis=-1, keepdims=True)
        p = jnp.exp2(s - m)
        l = jnp.sum(p, axis=-1, keepdims=True)
        o = jnp.dot(p.astype(BF16), v_ref[0, :, cs], preferred_element_type=F32)
        o_ref[0, :, cs] = (o * (1.0 / l)).astype(o_ref.dtype)


def _cross_attention(p3, kv3, heads=2):
    b, s, _ = p3.shape
    mem_len = kv3.shape[1]
    hd = XA_HEAD_DIM * heads
    groups = XA_HEADS // heads
    return pl.pallas_call(
        functools.partial(_xa_kernel, heads=heads),
        grid=(b, groups),
        in_specs=[pl.BlockSpec((1, s, hd), lambda bi, h: (bi, 0, XQ_OFF // hd + h)),
                  pl.BlockSpec((1, mem_len, hd), lambda bi, h: (bi, 0, h)),
                  pl.BlockSpec((1, mem_len, hd), lambda bi, h: (bi, 0, groups + h))],
        out_specs=pl.BlockSpec((1, s, hd), lambda bi, h: (bi, 0, h)),
        out_shape=jax.ShapeDtypeStruct((b, s, XA_WIDTH), BF16),
        compiler_params=_cparams(2),
        name="cross_attention",
    )(p3, kv3, kv3)


def _merge_kernel(oc_ref, od_ref, ox_ref, g0_ref, g1_ref, g2_ref, wc_ref, wd_ref, wx_ref, o_ref, *, sub):
    branches = ((oc_ref, wc_ref, g0_ref), (od_ref, wd_ref, g1_ref), (ox_ref, wx_ref, g2_ref))
    for r0 in range(0, o_ref.shape[0], sub):
        rows = slice(r0, r0 + sub)
        y = None
        for a_ref, w_ref, g_ref in branches:
            term = g_ref[rows, :].astype(F32) * jnp.dot(a_ref[rows, :], w_ref[...], preferred_element_type=F32)
            y = term if y is None else y + term
        o_ref[rows, :] = y.astype(o_ref.dtype)


def _merge(o_conv, o_da, o_xa, p2, wc_bf, wd_bf, wx_bf, tm=512, sub=256):
    t, kk = o_conv.shape
    n = wc_bf.shape[1]
    gcol = GATE_OFF // n
    act = lambda: pl.BlockSpec((tm, kk), lambda i: (i, 0))
    wsp = lambda: pl.BlockSpec((kk, n), lambda i: (0, 0), pipeline_mode=pl.Buffered(1))
    gate = lambda br: pl.BlockSpec((tm, n), lambda i: (i, gcol + br))
    return pl.pallas_call(
        functools.partial(_merge_kernel, sub=sub),
        grid=(t // tm,),
        in_specs=[act(), act(), act(), gate(0), gate(1), gate(2), wsp(), wsp(), wsp()],
        out_specs=pl.BlockSpec((tm, n), lambda i: (i, 0)),
        out_shape=jax.ShapeDtypeStruct((t, n), BF16),
        compiler_params=_cparams(1),
        name="merge",
    )(o_conv, o_da, o_xa, p2, p2, p2, wc_bf, wd_bf, wx_bf)


def _mix_kernel(a_ref, w_ref, x_ref, g_ref, x1_ref, h2_ref, *, sub):
    for r0 in range(0, a_ref.shape[0], sub):
        rows = slice(r0, r0 + sub)
        x1 = x_ref[rows, :] + jnp.dot(a_ref[rows, :], w_ref[...], preferred_element_type=F32)
        x1_ref[rows, :] = x1
        h2_ref[rows, :] = _rms_scale(x1, g_ref[...]).astype(h2_ref.dtype)


def _mix(merged, w_mix_bf, x2d, g_mlp, tm=1024, sub=512):
    t, k = merged.shape
    n = w_mix_bf.shape[1]
    return pl.pallas_call(
        functools.partial(_mix_kernel, sub=sub),
        grid=(t // tm,),
        in_specs=[pl.BlockSpec((tm, k), lambda i: (i, 0)),
                  pl.BlockSpec((k, n), lambda i: (0, 0), pipeline_mode=pl.Buffered(1)),
                  pl.BlockSpec((tm, n), lambda i: (i, 0)),
                  pl.BlockSpec((1, n), lambda i: (0, 0))],
        out_specs=[pl.BlockSpec((tm, n), lambda i: (i, 0)),
                   pl.BlockSpec((tm, n), lambda i: (i, 0))],
        out_shape=[jax.ShapeDtypeStruct((t, n), F32),
                   jax.ShapeDtypeStruct((t, n), BF16)],
        compiler_params=_cparams(1),
        name="mix_out",
    )(merged, w_mix_bf, x2d, g_mlp.reshape(1, n))


def _mlp_kernel(h_ref, x_hbm, wu_ref, wd_ref, gf_ref, o_ref, xbuf_ref, xsem, *, nf, tm):
    i = pl.program_id(0)
    f = pl.program_id(1)

    def residual_copy():
        rows = pl.ds(pl.multiple_of(i * tm, tm), tm)
        return pltpu.make_async_copy(x_hbm.at[rows, :], xbuf_ref, xsem)

    @pl.when(f == nf - 2)
    def _():
        residual_copy().start()

    def ff_tile():
        hm = jnp.dot(h_ref[...], wu_ref[...], preferred_element_type=F32)
        hm = jnp.square(jnp.maximum(hm, 0.0)).astype(BF16)
        return jnp.dot(hm, wd_ref[...], preferred_element_type=F32)

    @pl.when(f == 0)
    def _():
        o_ref[...] = ff_tile()

    @pl.when((f > 0) & (f < nf - 1))
    def _():
        o_ref[...] += ff_tile()

    @pl.when(f == nf - 1)
    def _():
        residual_copy().wait()
        o_ref[...] = _rms_scale(xbuf_ref[...] + o_ref[...] + ff_tile(), gf_ref[...])


def _mlp(h2, x1, w_up, w_down, g_final, tm=1024, tf=1024):
    t, d = x1.shape
    ff = w_up.shape[1]
    nf = ff // tf
    assert nf >= 2
    return pl.pallas_call(
        functools.partial(_mlp_kernel, nf=nf, tm=tm),
        grid=(t // tm, nf),
        in_specs=[pl.BlockSpec((tm, d), lambda i, f: (i, 0)),
                  pl.BlockSpec(memory_space=pl.ANY),
                  pl.BlockSpec((d, tf), lambda i, f: (0, f)),
                  pl.BlockSpec((tf, d), lambda i, f: (f, 0)),
                  pl.BlockSpec((1, d), lambda i, f: (0, 0))],
        out_specs=pl.BlockSpec((tm, d), lambda i, f: (i, 0)),
        out_shape=jax.ShapeDtypeStruct((t, d), F32),
        scratch_shapes=[pltpu.VMEM((tm, d), F32), pltpu.SemaphoreType.DMA(())],
        compiler_params=_cparams(2),
        name="mlp",
    )(h2, x1, w_up, w_down, g_final.reshape(1, d))


def kernel(x, mem, positions, g_mix, w_in, w_dw, b_dw, g_conv_ln, b_conv_ln, w_conv_out, lambda_q1, lambda_k1, lambda_q2, lambda_k2, g_subln, w_da_out, g_mem, w_mem_kv, w_xa_out, w_mix_out, g_mlp, w_up, w_down, g_final):
    bsz, seq, d = x.shape
    t = bsz * seq
    x2d = x.reshape(t, d)

    h = _rmsnorm(x2d, g_mix[0], BF16)
    cos2, sin2 = _rope_tables(positions.astype(F32).reshape(t, 1))
    u, p2 = _in_proj(h, w_in[0], cos2, sin2)
    p3 = p2.reshape(bsz, seq, P_COLS)

    o_conv, w_up_bf, w_down_bf = _conv_branch(u.reshape(bsz, seq, CONV_CH), w_dw[0], b_dw[0], g_conv_ln[0],
                                              b_conv_ln[0], w_up[0], w_down[0])
    o_da, wc_bf, wd_bf, wx_bf, w_mix_bf = _diff_attention(
        p3, lambda_q1[0], lambda_k1[0], lambda_q2[0], lambda_k2[0], g_subln[0],
        (w_conv_out[0], w_da_out[0], w_xa_out[0], w_mix_out[0]))

    kv3 = _mem_kv(mem, g_mem[0], w_mem_kv[0])
    o_xa = _cross_attention(p3, kv3)

    merged = _merge(o_conv.reshape(t, CONV_CH), o_da.reshape(t, DA_WIDTH), o_xa.reshape(t, XA_WIDTH),
                    p2, wc_bf, wd_bf, wx_bf)
    x1, h2 = _mix(merged, w_mix_bf, x2d, g_mlp[0])
    out = _mlp(h2, x1, w_up_bf, w_down_bf, g_final)
    return out.reshape(bsz, seq, d)
```

```python
import functools
import math

import jax
import jax.numpy as jnp
from jax import lax
from jax.experimental import pallas as pl
from jax.experimental.pallas import tpu as pltpu

F32 = jnp.float32
BF16 = jnp.bfloat16

D_MODEL = 2048
CONV_CH = 1024
CONV_WIDTH = 31
DA_HEADS = 4
DA_HEAD_DIM = 128
DA_WIDTH = 1024
XA_HEADS = 4
XA_HEAD_DIM = 256
XA_WIDTH = 1024
ROPE_THETA = 10000.0
NORM_EPS = 1e-6
LAMBDA_INIT = 0.8 - 0.6 * math.exp(-0.3 * 0)
LOG2E = math.log2(math.e)

IN_TILE = 1024
IN_HALF = IN_TILE // 2
N_GLU_TILES = CONV_CH // IN_HALF
P_COLS = 2 * 1024 + 1024 + XA_WIDTH + 3 * D_MODEL
Q_OFF, K_OFF, V_OFF, XQ_OFF, GATE_OFF = 0, 1024, 2048, 3072, 4096

V7X_VMEM_LIMIT = 62 * 1024 * 1024
CONV_HALO = 32
SUBLANES = 8
LANES = 128


def _cparams(n_axes):
    return pltpu.CompilerParams(
        dimension_semantics=("arbitrary",) * n_axes,
        vmem_limit_bytes=V7X_VMEM_LIMIT,
    )


def _sigmoid(x):
    return 0.5 * jnp.tanh(0.5 * x) + 0.5


def _rms_scale(x, g):
    ms = jnp.mean(x * x, axis=-1, keepdims=True)
    return x * lax.rsqrt(ms + NORM_EPS) * g


def _rmsnorm_kernel(x_ref, g_ref, o_ref):
    o_ref[...] = _rms_scale(x_ref[...], g_ref[...]).astype(o_ref.dtype)


def _rmsnorm(x2d, g, out_dtype, tm=2048):
    m, d = x2d.shape
    return pl.pallas_call(
        _rmsnorm_kernel,
        grid=(m // tm,),
        in_specs=[pl.BlockSpec((tm, d), lambda i: (i, 0)),
                  pl.BlockSpec((1, d), lambda i: (0, 0))],
        out_specs=pl.BlockSpec((tm, d), lambda i: (i, 0)),
        out_shape=jax.ShapeDtypeStruct((m, d), out_dtype),
        compiler_params=_cparams(1),
        name="rmsnorm",
    )(x2d, g.reshape(1, d))


def _rope_kernel(pos_lo_ref, pos_hi_ref, f_ref, cos_ref, sin_ref, *, half_rows):
    half = DA_HEAD_DIM // 2
    lane = lax.broadcasted_iota(jnp.int32, (half_rows, DA_HEAD_DIM), 1)
    left = lane < half
    ang = jnp.where(left, pos_lo_ref[...], pos_hi_ref[...]) * f_ref[...]
    c = jnp.cos(ang)
    s = jnp.sin(ang)
    c_sw = pltpu.roll(c, half, axis=1)
    s_sw = pltpu.roll(s, half, axis=1)
    cos_ref[0:half_rows, :] = jnp.where(left, c, c_sw)
    cos_ref[half_rows:, :] = jnp.where(left, c_sw, c)
    sin_ref[0:half_rows, :] = jnp.where(left, -s, s_sw)
    sin_ref[half_rows:, :] = jnp.where(left, -s_sw, s)


def _rope_tables(pos_col, tm=2048):
    t = pos_col.shape[0]
    inv_freq = 1.0 / (ROPE_THETA ** (jnp.arange(0, DA_HEAD_DIM, 2, dtype=F32) / DA_HEAD_DIM))
    f2 = jnp.concatenate([inv_freq, inv_freq]).reshape(1, DA_HEAD_DIM)
    half_rows = tm // 2
    return pl.pallas_call(
        functools.partial(_rope_kernel, half_rows=half_rows),
        grid=(t // tm,),
        in_specs=[pl.BlockSpec((half_rows, 1), lambda i: (2 * i, 0)),
                  pl.BlockSpec((half_rows, 1), lambda i: (2 * i + 1, 0)),
                  pl.BlockSpec((1, DA_HEAD_DIM), lambda i: (0, 0))],
        out_specs=[pl.BlockSpec((tm, DA_HEAD_DIM), lambda i: (i, 0))] * 2,
        out_shape=[jax.ShapeDtypeStruct((t, DA_HEAD_DIM), F32)] * 2,
        compiler_params=_cparams(1),
        name="rope_tables",
    )(pos_col, pos_col, f2)


def _in_proj_kernel(h_ref, wa_ref, wb_ref, cos_ref, sin_ref, u_ref, p_ref, wbf_ref, *, sub):
    j = pl.program_id(0)
    n = j - N_GLU_TILES
    blocks = [slice(r0, r0 + sub) for r0 in range(0, h_ref.shape[0], sub)]

    @pl.when(pl.program_id(1) == 0)
    def _():
        wbf_ref[:, :IN_HALF] = wa_ref[...].astype(BF16)
        wbf_ref[:, IN_HALF:] = wb_ref[...].astype(BF16)

    def proj(rows):
        return jnp.dot(h_ref[rows, :], wbf_ref[...], preferred_element_type=F32)

    @pl.when(j < N_GLU_TILES)
    def _():
        for rows in blocks:
            acc = proj(rows)
            u_ref[rows, :] = (acc[:, :IN_HALF] * _sigmoid(acc[:, IN_HALF:])).astype(u_ref.dtype)

    @pl.when((n >= 0) & (n < 2))
    def _():
        scale = jnp.where(n == 0, DA_HEAD_DIM ** -0.5 * LOG2E, 1.0).astype(F32)
        for rows in blocks:
            acc = proj(rows)
            c = cos_ref[rows, :] * scale
            s = sin_ref[rows, :] * scale
            for g in range(IN_TILE // DA_HEAD_DIM):
                sl = slice(g * DA_HEAD_DIM, (g + 1) * DA_HEAD_DIM)
                t = acc[:, sl]
                r = pltpu.roll(t, DA_HEAD_DIM // 2, axis=1)
                p_ref[rows, sl] = (t * c + r * s).astype(p_ref.dtype)

    @pl.when((n >= 2) & (n < 4))
    def _():
        scale = jnp.where(n == 3, XA_HEAD_DIM ** -0.5 * LOG2E, 1.0).astype(F32)
        for rows in blocks:
            p_ref[rows, :] = (proj(rows) * scale).astype(p_ref.dtype)

    @pl.when(n >= 4)
    def _():
        for rows in blocks:
            p_ref[rows, :] = _sigmoid(proj(rows)).astype(p_ref.dtype)


def _in_proj(h, w_in, cos2, sin2, tm=2048, sub=1024):
    t, d = h.shape
    n_i = t // tm
    glu = lambda j: j < N_GLU_TILES
    wa_map = lambda j, i: (0, jnp.where(glu(j), j, 2 * j))
    wb_map = lambda j, i: (0, jnp.where(glu(j), j + N_GLU_TILES, 2 * j + 1))
    u_map = lambda j, i: (jnp.where(glu(j), i, n_i - 1), jnp.where(glu(j), j, N_GLU_TILES - 1))
    p_map = lambda j, i: (jnp.where(glu(j), 0, i), jnp.where(glu(j), 0, j - N_GLU_TILES))
    return pl.pallas_call(
        functools.partial(_in_proj_kernel, sub=sub),
        grid=(N_GLU_TILES + P_COLS // IN_TILE, n_i),
        in_specs=[pl.BlockSpec((tm, d), lambda j, i: (i, 0)),
                  pl.BlockSpec((d, IN_HALF), wa_map),
                  pl.BlockSpec((d, IN_HALF), wb_map),
                  pl.BlockSpec((tm, DA_HEAD_DIM), lambda j, i: (i, 0)),
                  pl.BlockSpec((tm, DA_HEAD_DIM), lambda j, i: (i, 0))],
        out_specs=[pl.BlockSpec((tm, IN_HALF), u_map),
                   pl.BlockSpec((tm, IN_TILE), p_map)],
        out_shape=[jax.ShapeDtypeStruct((t, CONV_CH), BF16),
                   jax.ShapeDtypeStruct((t, P_COLS), BF16)],
        scratch_shapes=[pltpu.VMEM((d, IN_TILE), BF16)],
        compiler_params=_cparams(2),
        name="in_proj",
    )(h, w_in, w_in, cos2, sin2)


def _conv_kernel(ucur_ref, uprev_ref, w_ref, bdw_ref, g_ref, b_ref, wu_src_ref, wd_src_ref,
                 o_ref, wu_bf_ref, wd_bf_ref, buf_ref, sh_ref, y_ref, *, ts, rg, rc):
    si = pl.program_id(1)
    prev = uprev_ref[0].astype(F32)
    buf_ref[0:CONV_HALO, :] = jnp.where(si > 0, prev, 0.0)
    buf_ref[CONV_HALO:CONV_HALO + ts, :] = ucur_ref[0].astype(F32)
    buf_ref[CONV_HALO + ts:, :] = jnp.zeros((SUBLANES, CONV_CH), F32)
    n_sh = ts + CONV_HALO
    for r in range(1, SUBLANES):
        sh_ref[r - 1] = buf_ref[pl.ds(r, n_sh), :]

    lead = CONV_HALO - (CONV_WIDTH - 1)
    rows_per_iter = rg * SUBLANES

    for c in range(CONV_CH // LANES):
        lanes = slice(c * LANES, (c + 1) * LANES)
        taps = [jnp.broadcast_to(w_ref[k:k + 1, lanes], (SUBLANES, LANES)) for k in range(CONV_WIDTH)]
        bias = jnp.broadcast_to(bdw_ref[:, lanes], (SUBLANES, LANES))

        def rows_body(it, carry, lanes=lanes, taps=taps, bias=bias):
            r0 = pl.multiple_of(it * rows_per_iter, rows_per_iter)
            for gi in range(rg):
                acc = bias
                for k in range(CONV_WIDTH):
                    a, r = divmod(lead + k, SUBLANES)
                    row0 = pl.multiple_of(r0 + SUBLANES * (a + gi), SUBLANES)
                    if r == 0:
                        rows = buf_ref[pl.ds(row0, SUBLANES), lanes]
                    else:
                        rows = sh_ref[r - 1, pl.ds(row0, SUBLANES), lanes]
                    acc = acc + rows * taps[k]
                y_ref[pl.ds(pl.multiple_of(r0 + SUBLANES * gi, SUBLANES), SUBLANES), lanes] = acc
            return carry

        lax.fori_loop(0, ts // rows_per_iter, rows_body, 0)

    wu_bf_ref[...] = wu_src_ref[...].astype(BF16)
    wd_bf_ref[...] = wd_src_ref[...].astype(BF16)

    for ci in range(ts // rc):
        rows = slice(ci * rc, (ci + 1) * rc)
        acc = y_ref[rows, :]
        mu = jnp.mean(acc, axis=-1, keepdims=True)
        cen = acc - mu
        var = jnp.mean(cen * cen, axis=-1, keepdims=True)
        y = cen * lax.rsqrt(var + NORM_EPS) * g_ref[...] + b_ref[...]
        o_ref[0, rows, :] = (y * _sigmoid(y)).astype(o_ref.dtype)


def _conv_branch(u3, w_dw, b_dw, g_ln, b_ln, w_up, w_down, ts=512, rg=16, rc=128):
    b, s, c = u3.shape
    hb = ts // CONV_HALO
    n_s = s // ts
    chunk = lambda bi, si: (bi * n_s + si, 0)
    up_spec = pl.BlockSpec((w_up.shape[0] // (b * n_s), w_up.shape[1]), chunk)
    dn_spec = pl.BlockSpec((w_down.shape[0] // (b * n_s), w_down.shape[1]), chunk)
    return pl.pallas_call(
        functools.partial(_conv_kernel, ts=ts, rg=rg, rc=rc),
        grid=(b, n_s),
        in_specs=[pl.BlockSpec((1, ts, c), lambda bi, si: (bi, si, 0)),
                  pl.BlockSpec((1, CONV_HALO, c), lambda bi, si: (bi, jnp.maximum(si * hb - 1, 0), 0)),
                  pl.BlockSpec((CONV_WIDTH, c), lambda bi, si: (0, 0)),
                  pl.BlockSpec((1, c), lambda bi, si: (0, 0)),
                  pl.BlockSpec((1, c), lambda bi, si: (0, 0)),
                  pl.BlockSpec((1, c), lambda bi, si: (0, 0)),
                  up_spec, dn_spec],
        out_specs=[pl.BlockSpec((1, ts, c), lambda bi, si: (bi, si, 0)), up_spec, dn_spec],
        out_shape=[jax.ShapeDtypeStruct((b, s, c), BF16),
                   jax.ShapeDtypeStruct(w_up.shape, BF16),
                   jax.ShapeDtypeStruct(w_down.shape, BF16)],
        scratch_shapes=[pltpu.VMEM((CONV_HALO + ts + SUBLANES, c), F32),
                        pltpu.VMEM((SUBLANES - 1, CONV_HALO + ts, c), F32),
                        pltpu.VMEM((ts, c), F32)],
        compiler_params=_cparams(2),
        name="conv_branch",
    )(u3, u3, w_dw, b_dw.reshape(1, c), g_ln.reshape(1, c), b_ln.reshape(1, c), w_up, w_down)


def _da_tiles(lq1_ref, lk1_ref, lq2_ref, lk2_ref, g_ref, q_ref, k_ref, v_ref, o_ref, *, tq, tiles):
    hd = DA_HEAD_DIM
    nt = (((1,), (1,)), ((), ()))
    lam = (jnp.exp(jnp.sum(lq1_ref[...] * lk1_ref[...], axis=-1, keepdims=True))
           - jnp.exp(jnp.sum(lq2_ref[...] * lk2_ref[...], axis=-1, keepdims=True))
           + LAMBDA_INIT)

    def tile(n, part):
        off = n * tq
        rows = slice(part * tq, (part + 1) * tq)
        q = q_ref[0, rows, :]
        causal = (lax.broadcasted_iota(jnp.int32, (tq, tq), 0)
                  >= lax.broadcasted_iota(jnp.int32, (tq, tq), 1))
        p_off, p_dg, inv_l = [], [], []
        for mp in range(2):
            cs = slice(mp * hd, (mp + 1) * hd)
            sd = lax.dot_general(q[:, cs], k_ref[0, off:off + tq, cs], nt, preferred_element_type=F32)
            sd = jnp.where(causal, sd, -jnp.inf)
            m = jnp.max(sd, axis=-1, keepdims=True)
            if n > 0:
                so = lax.dot_general(q[:, cs], k_ref[0, 0:off, cs], nt, preferred_element_type=F32)
                m = jnp.maximum(m, jnp.max(so, axis=-1, keepdims=True))
                po = jnp.exp2(so - m)
                l = jnp.sum(po, axis=-1, keepdims=True)
                p_off.append(po.astype(BF16))
            else:
                l = 0.0
            pd = jnp.exp2(sd - m)
            l = l + jnp.sum(pd, axis=-1, keepdims=True)
            p_dg.append(pd.astype(BF16))
            inv_l.append(1.0 / l)
        acc = jnp.dot(jnp.concatenate(p_dg, axis=0), v_ref[0, off:off + tq, :], preferred_element_type=F32)
        if n > 0:
            acc = acc + jnp.dot(jnp.concatenate(p_off, axis=0), v_ref[0, 0:off, :],
                                preferred_element_type=F32)
        dlt = acc[:tq] * inv_l[0] - lam * (acc[tq:] * inv_l[1])
        y = _rms_scale(dlt, g_ref[...]) * (1.0 - LAMBDA_INIT)
        o_ref[0, rows, :] = y.astype(o_ref.dtype)

    for n, part in tiles:
        tile(n, part)


def _da_kernel(lq1_ref, lk1_ref, lq2_ref, lk2_ref, g_ref, q_ref, k_ref, v_ref, *rest, tq, nq, n_cast):
    src_refs, o_ref, dst_refs = rest[:n_cast], rest[n_cast], rest[n_cast + 1:]
    for src_ref, dst_ref in zip(src_refs, dst_refs):
        dst_ref[...] = src_ref[...].astype(BF16)
    _da_tiles(lq1_ref, lk1_ref, lq2_ref, lk2_ref, g_ref, q_ref, k_ref, v_ref, o_ref,
              tq=tq, tiles=[(n, n) for n in range(nq)])


def _diff_attention(p3, lq1, lk1, lq2, lk2, g_subln, cast_weights, tq=256):
    b, s, _ = p3.shape
    hw = 2 * DA_HEAD_DIM
    nq = s // tq
    n_steps = b * DA_HEADS
    vec = lambda: pl.BlockSpec((1, DA_HEAD_DIM), lambda bi, h: (0, 0))
    chunk = lambda bi, h: (bi * DA_HEADS + h, 0)
    cast_specs = [pl.BlockSpec((w.shape[0] // n_steps, w.shape[1]), chunk) for w in cast_weights]
    return pl.pallas_call(
        functools.partial(_da_kernel, tq=tq, nq=nq, n_cast=len(cast_weights)),
        grid=(b, DA_HEADS),
        in_specs=[vec(), vec(), vec(), vec(),
                  pl.BlockSpec((1, hw), lambda bi, h: (0, 0)),
                  pl.BlockSpec((1, s, hw), lambda bi, h: (bi, 0, Q_OFF // hw + h)),
                  pl.BlockSpec((1, s, hw), lambda bi, h: (bi, 0, K_OFF // hw + h)),
                  pl.BlockSpec((1, s, hw), lambda bi, h: (bi, 0, V_OFF // hw + h))] + cast_specs,
        out_specs=[pl.BlockSpec((1, s, hw), lambda bi, h: (bi, 0, h))] + cast_specs,
        out_shape=[jax.ShapeDtypeStruct((b, s, DA_WIDTH), BF16)]
                  + [jax.ShapeDtypeStruct(w.shape, BF16) for w in cast_weights],
        compiler_params=_cparams(2),
        name="diff_attention",
    )(lq1.reshape(1, -1), lk1.reshape(1, -1), lq2.reshape(1, -1), lk2.reshape(1, -1),
      g_subln.reshape(1, hw), p3, p3, p3, *cast_weights)


def _mem_kv_kernel(mem_ref, g_ref, w_ref, o_ref, wbf_ref):
    @pl.when(pl.program_id(1) == 0)
    def _():
        wbf_ref[...] = w_ref[...].astype(BF16)

    mem_n = _rms_scale(mem_ref[0], g_ref[...]).astype(BF16)
    o_ref[0] = jnp.dot(mem_n, wbf_ref[...], preferred_element_type=F32).astype(o_ref.dtype)


def _mem_kv(mem, g_mem, w_kv, tn=1024):
    b, m, k = mem.shape
    n = w_kv.shape[1]
    return pl.pallas_call(
        _mem_kv_kernel,
        grid=(n // tn, b),
        in_specs=[pl.BlockSpec((1, m, k), lambda j, bi: (bi, 0, 0)),
                  pl.BlockSpec((1, k), lambda j, bi: (0, 0)),
                  pl.BlockSpec((k, tn), lambda j, bi: (0, j))],
        out_specs=pl.BlockSpec((1, m, tn), lambda j, bi: (bi, 0, j)),
        out_shape=jax.ShapeDtypeStruct((b, m, n), BF16),
        scratch_shapes=[pltpu.VMEM((k, tn), BF16)],
        compiler_params=_cparams(2),
        name="mem_kv",
    )(mem, g_mem.reshape(1, k), w_kv)


def _xa_kernel(q_ref, k_ref, v_ref, o_ref, *, heads):
    nt = (((1,), (1,)), ((), ()))
    hd = XA_HEAD_DIM
    for hh in range(heads):
        cs = slice(hh * hd, (hh + 1) * hd)
        s = lax.dot_general(q_ref[0, :, cs], k_ref[0, :, cs], nt, preferred_element_type=F32)
        m = jnp.max(s, axis=-1, keepdims=True)
        p = jnp.exp2(s - m)
        l = jnp.sum(p, axis=-1, keepdims=True)
        o = jnp.dot(p.astype(BF16), v_ref[0, :, cs], preferred_element_type=F32)
        o_ref[0, :, cs] = (o * (1.0 / l)).astype(o_ref.dtype)


def _cross_attention(p3, kv3, heads=2):
    b, s, _ = p3.shape
    mem_len = kv3.shape[1]
    hd = XA_HEAD_DIM * heads
    groups = XA_HEADS // heads
    return pl.pallas_call(
        functools.partial(_xa_kernel, heads=heads),
        grid=(b, groups),
        in_specs=[pl.BlockSpec((1, s, hd), lambda bi, h: (bi, 0, XQ_OFF // hd + h)),
                  pl.BlockSpec((1, mem_len, hd), lambda bi, h: (bi, 0, h)),
                  pl.BlockSpec((1, mem_len, hd), lambda bi, h: (bi, 0, groups + h))],
        out_specs=pl.BlockSpec((1, s, hd), lambda bi, h: (bi, 0, h)),
        out_shape=jax.ShapeDtypeStruct((b, s, XA_WIDTH), BF16),
        compiler_params=_cparams(2),
        name="cross_attention",
    )(p3, kv3, kv3)


def _merge_kernel(oc_ref, od_ref, ox_ref, g0_ref, g1_ref, g2_ref, wc_ref, wd_ref, wx_ref, o_ref, *, sub):
    branches = ((oc_ref, wc_ref, g0_ref), (od_ref, wd_ref, g1_ref), (ox_ref, wx_ref, g2_ref))
    for r0 in range(0, o_ref.shape[0], sub):
        rows = slice(r0, r0 + sub)
        y = None
        for a_ref, w_ref, g_ref in branches:
            term = g_ref[rows, :].astype(F32) * jnp.dot(a_ref[rows, :], w_ref[...], preferred_element_type=F32)
            y = term if y is None else y + term
        o_ref[rows, :] = y.astype(o_ref.dtype)


def _merge(o_conv, o_da, o_xa, p2, wc_bf, wd_bf, wx_bf, tm=512, sub=256):
    t, kk = o_conv.shape
    n = wc_bf.shape[1]
    gcol = GATE_OFF // n
    act = lambda: pl.BlockSpec((tm, kk), lambda i: (i, 0))
    wsp = lambda: pl.BlockSpec((kk, n), lambda i: (0, 0), pipeline_mode=pl.Buffered(1))
    gate = lambda br: pl.BlockSpec((tm, n), lambda i: (i, gcol + br))
    return pl.pallas_call(
        functools.partial(_merge_kernel, sub=sub),
        grid=(t // tm,),
        in_specs=[act(), act(), act(), gate(0), gate(1), gate(2), wsp(), wsp(), wsp()],
        out_specs=pl.BlockSpec((tm, n), lambda i: (i, 0)),
        out_shape=jax.ShapeDtypeStruct((t, n), BF16),
        compiler_params=_cparams(1),
        name="merge",
    )(o_conv, o_da, o_xa, p2, p2, p2, wc_bf, wd_bf, wx_bf)


def _mix_kernel(a_ref, w_ref, x_ref, g_ref, x1_ref, h2_ref, *, sub):
    for r0 in range(0, a_ref.shape[0], sub):
        rows = slice(r0, r0 + sub)
        x1 = x_ref[rows, :] + jnp.dot(a_ref[rows, :], w_ref[...], preferred_element_type=F32)
        x1_ref[rows, :] = x1
        h2_ref[rows, :] = _rms_scale(x1, g_ref[...]).astype(h2_ref.dtype)


def _mix(merged, w_mix_bf, x2d, g_mlp, tm=1024, sub=512):
    t, k = merged.shape
    n = w_mix_bf.shape[1]
    return pl.pallas_call(
        functools.partial(_mix_kernel, sub=sub),
        grid=(t // tm,),
        in_specs=[pl.BlockSpec((tm, k), lambda i: (i, 0)),
                  pl.BlockSpec((k, n), lambda i: (0, 0), pipeline_mode=pl.Buffered(1)),
                  pl.BlockSpec((tm, n), lambda i: (i, 0)),
                  pl.BlockSpec((1, n), lambda i: (0, 0))],
        out_specs=[pl.BlockSpec((tm, n), lambda i: (i, 0)),
                   pl.BlockSpec((tm, n), lambda i: (i, 0))],
        out_shape=[jax.ShapeDtypeStruct((t, n), F32),
                   jax.ShapeDtypeStruct((t, n), BF16)],
        compiler_params=_cparams(1),
        name="mix_out",
    )(merged, w_mix_bf, x2d, g_mlp.reshape(1, n))


def _mlp_kernel(h_ref, x_hbm, wu_ref, wd_ref, gf_ref, o_ref, xbuf_ref, xsem, *, nf, tm):
    i = pl.program_id(0)
    f = pl.program_id(1)

    def residual_copy():
        rows = pl.ds(pl.multiple_of(i * tm, tm), tm)
        return pltpu.make_async_copy(x_hbm.at[rows, :], xbuf_ref, xsem)

    @pl.when(f == nf - 2)
    def _():
        residual_copy().start()

    def ff_tile():
        hm = jnp.dot(h_ref[...], wu_ref[...], preferred_element_type=F32)
        hm = jnp.square(jnp.maximum(hm, 0.0)).astype(BF16)
        return jnp.dot(hm, wd_ref[...], preferred_element_type=F32)

    @pl.when(f == 0)
    def _():
        o_ref[...] = ff_tile()

    @pl.when((f > 0) & (f < nf - 1))
    def _():
        o_ref[...] += ff_tile()

    @pl.when(f == nf - 1)
    def _():
        residual_copy().wait()
        o_ref[...] = _rms_scale(xbuf_ref[...] + o_ref[...] + ff_tile(), gf_ref[...])


def _mlp(h2, x1, w_up, w_down, g_final, tm=1024, tf=1024):
    t, d = x1.shape
    ff = w_up.shape[1]
    nf = ff // tf
    assert nf >= 2
    return pl.pallas_call(
        functools.partial(_mlp_kernel, nf=nf, tm=tm),
        grid=(t // tm, nf),
        in_specs=[pl.BlockSpec((tm, d), lambda i, f: (i, 0)),
                  pl.BlockSpec(memory_space=pl.ANY),
                  pl.BlockSpec((d, tf), lambda i, f: (0, f)),
                  pl.BlockSpec((tf, d), lambda i, f: (f, 0)),
                  pl.BlockSpec((1, d), lambda i, f: (0, 0))],
        out_specs=pl.BlockSpec((tm, d), lambda i, f: (i, 0)),
        out_shape=jax.ShapeDtypeStruct((t, d), F32),
        scratch_shapes=[pltpu.VMEM((tm, d), F32), pltpu.SemaphoreType.DMA(())],
        compiler_params=_cparams(2),
        name="mlp",
    )(h2, x1, w_up, w_down, g_final.reshape(1, d))


def kernel(x, mem, positions, g_mix, w_in, w_dw, b_dw, g_conv_ln, b_conv_ln, w_conv_out, lambda_q1, lambda_k1, lambda_q2, lambda_k2, g_subln, w_da_out, g_mem, w_mem_kv, w_xa_out, w_mix_out, g_mlp, w_up, w_down, g_final):
    bsz, seq, d = x.shape
    t = bsz * seq
    x2d = x.reshape(t, d)

    h = _rmsnorm(x2d, g_mix[0], BF16)
    cos2, sin2 = _rope_tables(positions.astype(F32).reshape(t, 1))
    u, p2 = _in_proj(h, w_in[0], cos2, sin2)
    p3 = p2.reshape(bsz, seq, P_COLS)

    o_conv, w_up_bf, w_down_bf = _conv_branch(u.reshape(bsz, seq, CONV_CH), w_dw[0], b_dw[0], g_conv_ln[0],
                                              b_conv_ln[0], w_up[0], w_down[0])
    o_da, wc_bf, wd_bf, wx_bf, w_mix_bf = _diff_attention(
        p3, lambda_q1[0], lambda_k1[0], lambda_q2[0], lambda_k2[0], g_subln[0],
        (w_conv_out[0], w_da_out[0], w_xa_out[0], w_mix_out[0]))

    kv3 = _mem_kv(mem, g_mem[0], w_mem_kv[0])
    o_xa = _cross_attention(p3, kv3)

    merged = _merge(o_conv.reshape(t, CONV_CH), o_da.reshape(t, DA_WIDTH), o_xa.reshape(t, XA_WIDTH),
                    p2, wc_bf, wd_bf, wx_bf)
    x1, h2 = _mix(merged, w_mix_bf, x2d, g_mlp[0])
    out = _mlp(h2, x1, w_up_bf, w_down_bf, g_final)
    return out.reshape(bsz, seq, d)
```

```python
import functools
import math

import jax
import jax.numpy as jnp
from jax import lax
from jax.experimental import pallas as pl
from jax.experimental.pallas import tpu as pltpu

F32 = jnp.float32
BF16 = jnp.bfloat16

D_MODEL = 2048
CONV_CH = 1024
CONV_WIDTH = 31
DA_HEADS = 4
DA_HEAD_DIM = 128
DA_WIDTH = 1024
XA_HEADS = 4
XA_HEAD_DIM = 256
XA_WIDTH = 1024
ROPE_THETA = 10000.0
NORM_EPS = 1e-6
LAMBDA_INIT = 0.8 - 0.6 * math.exp(-0.3 * 0)
LOG2E = math.log2(math.e)

IN_TILE = 1024
IN_HALF = IN_TILE // 2
N_GLU_TILES = CONV_CH // IN_HALF
P_COLS = 2 * 1024 + 1024 + XA_WIDTH + 3 * D_MODEL
Q_OFF, K_OFF, V_OFF, XQ_OFF, GATE_OFF = 0, 1024, 2048, 3072, 4096

V7X_VMEM_LIMIT = 62 * 1024 * 1024
CONV_HALO = 32
SUBLANES = 8
LANES = 128


def _cparams(n_axes):
    return pltpu.CompilerParams(
        dimension_semantics=("arbitrary",) * n_axes,
        vmem_limit_bytes=V7X_VMEM_LIMIT,
    )


def _sigmoid(x):
    return 0.5 * jnp.tanh(0.5 * x) + 0.5


def _rms_scale(x, g):
    ms = jnp.mean(x * x, axis=-1, keepdims=True)
    return x * lax.rsqrt(ms + NORM_EPS) * g


def _norm_rope_kernel(x_ref, g_ref, pos_lo_ref, pos_hi_ref, f_ref, h_ref, cos_ref, sin_ref, *, half_rows):
    h_ref[...] = _rms_scale(x_ref[...], g_ref[...]).astype(h_ref.dtype)
    half = DA_HEAD_DIM // 2
    lane = lax.broadcasted_iota(jnp.int32, (half_rows, DA_HEAD_DIM), 1)
    left = lane < half
    ang = jnp.where(left, pos_lo_ref[...], pos_hi_ref[...]) * f_ref[...]
    c = jnp.cos(ang)
    s = jnp.sin(ang)
    c_sw = pltpu.roll(c, half, axis=1)
    s_sw = pltpu.roll(s, half, axis=1)
    cos_ref[0:half_rows, :] = jnp.where(left, c, c_sw)
    cos_ref[half_rows:, :] = jnp.where(left, c_sw, c)
    sin_ref[0:half_rows, :] = jnp.where(left, -s, s_sw)
    sin_ref[half_rows:, :] = jnp.where(left, -s_sw, s)


def _norm_and_rope(x2d, g, pos_col, tm=2048):
    t, d = x2d.shape
    inv_freq = 1.0 / (ROPE_THETA ** (jnp.arange(0, DA_HEAD_DIM, 2, dtype=F32) / DA_HEAD_DIM))
    f2 = jnp.concatenate([inv_freq, inv_freq]).reshape(1, DA_HEAD_DIM)
    half_rows = tm // 2
    table = pl.BlockSpec((tm, DA_HEAD_DIM), lambda i: (i, 0))
    return pl.pallas_call(
        functools.partial(_norm_rope_kernel, half_rows=half_rows),
        grid=(t // tm,),
        in_specs=[pl.BlockSpec((tm, d), lambda i: (i, 0)),
                  pl.BlockSpec((1, d), lambda i: (0, 0)),
                  pl.BlockSpec((half_rows, 1), lambda i: (2 * i, 0)),
                  pl.BlockSpec((half_rows, 1), lambda i: (2 * i + 1, 0)),
                  pl.BlockSpec((1, DA_HEAD_DIM), lambda i: (0, 0))],
        out_specs=[pl.BlockSpec((tm, d), lambda i: (i, 0)), table, table],
        out_shape=[jax.ShapeDtypeStruct((t, d), BF16),
                   jax.ShapeDtypeStruct((t, DA_HEAD_DIM), F32),
                   jax.ShapeDtypeStruct((t, DA_HEAD_DIM), F32)],
        compiler_params=_cparams(1),
        name="norm_rope",
    )(x2d, g.reshape(1, d), pos_col, pos_col, f2)


def _in_proj_kernel(h_ref, wa_ref, wb_ref, cos_ref, sin_ref, u_ref, p_ref, wbf_ref, *, sub):
    j = pl.program_id(0)
    n = j - N_GLU_TILES
    blocks = [slice(r0, r0 + sub) for r0 in range(0, h_ref.shape[0], sub)]

    @pl.when(pl.program_id(1) == 0)
    def _():
        wbf_ref[:, :IN_HALF] = wa_ref[...].astype(BF16)
        wbf_ref[:, IN_HALF:] = wb_ref[...].astype(BF16)

    def proj(rows):
        return jnp.dot(h_ref[rows, :], wbf_ref[...], preferred_element_type=F32)

    @pl.when(j < N_GLU_TILES)
    def _():
        for rows in blocks:
            acc = proj(rows)
            u_ref[rows, :] = (acc[:, :IN_HALF] * _sigmoid(acc[:, IN_HALF:])).astype(u_ref.dtype)

    @pl.when((n >= 0) & (n < 2))
    def _():
        scale = jnp.where(n == 0, DA_HEAD_DIM ** -0.5 * LOG2E, 1.0).astype(F32)
        for rows in blocks:
            acc = proj(rows)
            c = cos_ref[rows, :] * scale
            s = sin_ref[rows, :] * scale
            for g in range(IN_TILE // DA_HEAD_DIM):
                sl = slice(g * DA_HEAD_DIM, (g + 1) * DA_HEAD_DIM)
                t = acc[:, sl]
                r = pltpu.roll(t, DA_HEAD_DIM // 2, axis=1)
                p_ref[rows, sl] = (t * c + r * s).astype(p_ref.dtype)

    @pl.when((n >= 2) & (n < 4))
    def _():
        scale = jnp.where(n == 3, XA_HEAD_DIM ** -0.5 * LOG2E, 1.0).astype(F32)
        for rows in blocks:
            p_ref[rows, :] = (proj(rows) * scale).astype(p_ref.dtype)

    @pl.when(n >= 4)
    def _():
        for rows in blocks:
            p_ref[rows, :] = _sigmoid(proj(rows)).astype(p_ref.dtype)


def _in_proj(h, w_in, cos2, sin2, tm=2048, sub=1024):
    t, d = h.shape
    n_i = t // tm
    glu = lambda j: j < N_GLU_TILES
    wa_map = lambda j, i: (0, jnp.where(glu(j), j, 2 * j))
    wb_map = lambda j, i: (0, jnp.where(glu(j), j + N_GLU_TILES, 2 * j + 1))
    u_map = lambda j, i: (jnp.where(glu(j), i, n_i - 1), jnp.where(glu(j), j, N_GLU_TILES - 1))
    p_map = lambda j, i: (jnp.where(glu(j), 0, i), jnp.where(glu(j), 0, j - N_GLU_TILES))
    return pl.pallas_call(
        functools.partial(_in_proj_kernel, sub=sub),
        grid=(N_GLU_TILES + P_COLS // IN_TILE, n_i),
        in_specs=[pl.BlockSpec((tm, d), lambda j, i: (i, 0)),
                  pl.BlockSpec((d, IN_HALF), wa_map),
                  pl.BlockSpec((d, IN_HALF), wb_map),
                  pl.BlockSpec((tm, DA_HEAD_DIM), lambda j, i: (i, 0)),
                  pl.BlockSpec((tm, DA_HEAD_DIM), lambda j, i: (i, 0))],
        out_specs=[pl.BlockSpec((tm, IN_HALF), u_map),
                   pl.BlockSpec((tm, IN_TILE), p_map)],
        out_shape=[jax.ShapeDtypeStruct((t, CONV_CH), BF16),
                   jax.ShapeDtypeStruct((t, P_COLS), BF16)],
        scratch_shapes=[pltpu.VMEM((d, IN_TILE), BF16)],
        compiler_params=_cparams(2),
        name="in_proj",
    )(h, w_in, w_in, cos2, sin2)


def _conv_kernel(ucur_ref, uprev_ref, w_ref, bdw_ref, g_ref, b_ref, wu_src_ref, wd_src_ref,
                 o_ref, wu_bf_ref, wd_bf_ref, buf_ref, sh_ref, y_ref, *, ts, rg, rc):
    si = pl.program_id(1)
    prev = uprev_ref[0].astype(F32)
    buf_ref[0:CONV_HALO, :] = jnp.where(si > 0, prev, 0.0)
    buf_ref[CONV_HALO:CONV_HALO + ts, :] = ucur_ref[0].astype(F32)
    buf_ref[CONV_HALO + ts:, :] = jnp.zeros((SUBLANES, CONV_CH), F32)
    n_sh = ts + CONV_HALO
    for r in range(1, SUBLANES):
        sh_ref[r - 1] = buf_ref[pl.ds(r, n_sh), :]

    lead = CONV_HALO - (CONV_WIDTH - 1)
    rows_per_iter = rg * SUBLANES

    for c in range(CONV_CH // LANES):
        lanes = slice(c * LANES, (c + 1) * LANES)
        taps = [jnp.broadcast_to(w_ref[k:k + 1, lanes], (SUBLANES, LANES)) for k in range(CONV_WIDTH)]
        bias = jnp.broadcast_to(bdw_ref[:, lanes], (SUBLANES, LANES))

        def rows_body(it, carry, lanes=lanes, taps=taps, bias=bias):
            r0 = pl.multiple_of(it * rows_per_iter, rows_per_iter)
            for gi in range(rg):
                acc = bias
                for k in range(CONV_WIDTH):
                    a, r = divmod(lead + k, SUBLANES)
                    row0 = pl.multiple_of(r0 + SUBLANES * (a + gi), SUBLANES)
                    if r == 0:
                        rows = buf_ref[pl.ds(row0, SUBLANES), lanes]
                    else:
                        rows = sh_ref[r - 1, pl.ds(row0, SUBLANES), lanes]
                    acc = acc + rows * taps[k]
                y_ref[pl.ds(pl.multiple_of(r0 + SUBLANES * gi, SUBLANES), SUBLANES), lanes] = acc
            return carry

        lax.fori_loop(0, ts // rows_per_iter, rows_body, 0)

    wu_bf_ref[...] = wu_src_ref[...].astype(BF16)
    wd_bf_ref[...] = wd_src_ref[...].astype(BF16)

    for ci in range(ts // rc):
        rows = slice(ci * rc, (ci + 1) * rc)
        acc = y_ref[rows, :]
        mu = jnp.mean(acc, axis=-1, keepdims=True)
        cen = acc - mu
        var = jnp.mean(cen * cen, axis=-1, keepdims=True)
        y = cen * lax.rsqrt(var + NORM_EPS) * g_ref[...] + b_ref[...]
        o_ref[0, rows, :] = (y * _sigmoid(y)).astype(o_ref.dtype)


def _conv_branch(u3, w_dw, b_dw, g_ln, b_ln, w_up, w_down, ts=512, rg=16, rc=128):
    b, s, c = u3.shape
    hb = ts // CONV_HALO
    n_s = s // ts
    chunk = lambda bi, si: (bi * n_s + si, 0)
    up_spec = pl.BlockSpec((w_up.shape[0] // (b * n_s), w_up.shape[1]), chunk)
    dn_spec = pl.BlockSpec((w_down.shape[0] // (b * n_s), w_down.shape[1]), chunk)
    return pl.pallas_call(
        functools.partial(_conv_kernel, ts=ts, rg=rg, rc=rc),
        grid=(b, n_s),
        in_specs=[pl.BlockSpec((1, ts, c), lambda bi, si: (bi, si, 0)),
                  pl.BlockSpec((1, CONV_HALO, c), lambda bi, si: (bi, jnp.maximum(si * hb - 1, 0), 0)),
                  pl.BlockSpec((CONV_WIDTH, c), lambda bi, si: (0, 0)),
                  pl.BlockSpec((1, c), lambda bi, si: (0, 0)),
                  pl.BlockSpec((1, c), lambda bi, si: (0, 0)),
                  pl.BlockSpec((1, c), lambda bi, si: (0, 0)),
                  up_spec, dn_spec],
        out_specs=[pl.BlockSpec((1, ts, c), lambda bi, si: (bi, si, 0)), up_spec, dn_spec],
        out_shape=[jax.ShapeDtypeStruct((b, s, c), BF16),
                   jax.ShapeDtypeStruct(w_up.shape, BF16),
                   jax.ShapeDtypeStruct(w_down.shape, BF16)],
        scratch_shapes=[pltpu.VMEM((CONV_HALO + ts + SUBLANES, c), F32),
                        pltpu.VMEM((SUBLANES - 1, CONV_HALO + ts, c), F32),
                        pltpu.VMEM((ts, c), F32)],
        compiler_params=_cparams(2),
        name="conv_branch",
    )(u3, u3, w_dw, b_dw.reshape(1, c), g_ln.reshape(1, c), b_ln.reshape(1, c), w_up, w_down)


def _da_tiles(lq1_ref, lk1_ref, lq2_ref, lk2_ref, g_ref, q_ref, k_ref, v_ref, o_ref, *, tq, tiles):
    hd = DA_HEAD_DIM
    nt = (((1,), (1,)), ((), ()))
    lam = (jnp.exp(jnp.sum(lq1_ref[...] * lk1_ref[...], axis=-1, keepdims=True))
           - jnp.exp(jnp.sum(lq2_ref[...] * lk2_ref[...], axis=-1, keepdims=True))
           + LAMBDA_INIT)

    def tile(n, part):
        off = n * tq
        rows = slice(part * tq, (part + 1) * tq)
        q = q_ref[0, rows, :]
        causal = (lax.broadcasted_iota(jnp.int32, (tq, tq), 0)
                  >= lax.broadcasted_iota(jnp.int32, (tq, tq), 1))
        p_off, p_dg, inv_l = [], [], []
        for mp in range(2):
            cs = slice(mp * hd, (mp + 1) * hd)
            sd = lax.dot_general(q[:, cs], k_ref[0, off:off + tq, cs], nt, preferred_element_type=F32)
            sd = jnp.where(causal, sd, -jnp.inf)
            m = jnp.max(sd, axis=-1, keepdims=True)
            if n > 0:
                so = lax.dot_general(q[:, cs], k_ref[0, 0:off, cs], nt, preferred_element_type=F32)
                m = jnp.maximum(m, jnp.max(so, axis=-1, keepdims=True))
                po = jnp.exp2(so - m)
                l = jnp.sum(po, axis=-1, keepdims=True)
                p_off.append(po.astype(BF16))
            else:
                l = 0.0
            pd = jnp.exp2(sd - m)
            l = l + jnp.sum(pd, axis=-1, keepdims=True)
            p_dg.append(pd.astype(BF16))
            inv_l.append(1.0 / l)
        acc = jnp.dot(jnp.concatenate(p_dg, axis=0), v_ref[0, off:off + tq, :], preferred_element_type=F32)
        if n > 0:
            acc = acc + jnp.dot(jnp.concatenate(p_off, axis=0), v_ref[0, 0:off, :],
                                preferred_element_type=F32)
        dlt = acc[:tq] * inv_l[0] - lam * (acc[tq:] * inv_l[1])
        y = _rms_scale(dlt, g_ref[...]) * (1.0 - LAMBDA_INIT)
        o_ref[0, rows, :] = y.astype(o_ref.dtype)

    for n, part in tiles:
        tile(n, part)


def _da_kernel(lq1_ref, lk1_ref, lq2_ref, lk2_ref, g_ref, q_ref, k_ref, v_ref, *rest, tq, nq, n_cast):
    src_refs, o_ref, dst_refs = rest[:n_cast], rest[n_cast], rest[n_cast + 1:]
    for src_ref, dst_ref in zip(src_refs, dst_refs):
        dst_ref[...] = src_ref[...].astype(BF16)
    _da_tiles(lq1_ref, lk1_ref, lq2_ref, lk2_ref, g_ref, q_ref, k_ref, v_ref, o_ref,
              tq=tq, tiles=[(n, n) for n in range(nq)])


def _diff_attention(p3, lq1, lk1, lq2, lk2, g_subln, cast_weights, tq=256):
    b, s, _ = p3.shape
    hw = 2 * DA_HEAD_DIM
    nq = s // tq
    n_steps = b * DA_HEADS
    vec = lambda: pl.BlockSpec((1, DA_HEAD_DIM), lambda bi, h: (0, 0))
    chunk = lambda bi, h: (bi * DA_HEADS + h, 0)
    cast_specs = [pl.BlockSpec((w.shape[0] // n_steps, w.shape[1]), chunk) for w in cast_weights]
    return pl.pallas_call(
        functools.partial(_da_kernel, tq=tq, nq=nq, n_cast=len(cast_weights)),
        grid=(b, DA_HEADS),
        in_specs=[vec(), vec(), vec(), vec(),
                  pl.BlockSpec((1, hw), lambda bi, h: (0, 0)),
                  pl.BlockSpec((1, s, hw), lambda bi, h: (bi, 0, Q_OFF // hw + h)),
                  pl.BlockSpec((1, s, hw), lambda bi, h: (bi, 0, K_OFF // hw + h)),
                  pl.BlockSpec((1, s, hw), lambda bi, h: (bi, 0, V_OFF // hw + h))] + cast_specs,
        out_specs=[pl.BlockSpec((1, s, hw), lambda bi, h: (bi, 0, h))] + cast_specs,
        out_shape=[jax.ShapeDtypeStruct((b, s, DA_WIDTH), BF16)]
                  + [jax.ShapeDtypeStruct(w.shape, BF16) for w in cast_weights],
        compiler_params=_cparams(2),
        name="diff_attention",
    )(lq1.reshape(1, -1), lk1.reshape(1, -1), lq2.reshape(1, -1), lk2.reshape(1, -1),
      g_subln.reshape(1, hw), p3, p3, p3, *cast_weights)


def _mem_kv_kernel(mem_ref, g_ref, w_ref, o_ref, wbf_ref):
    @pl.when(pl.program_id(1) == 0)
    def _():
        wbf_ref[...] = w_ref[...].astype(BF16)

    mem_n = _rms_scale(mem_ref[0], g_ref[...]).astype(BF16)
    o_ref[0] = jnp.dot(mem_n, wbf_ref[...], preferred_element_type=F32).astype(o_ref.dtype)


def _mem_kv(mem, g_mem, w_kv, tn=1024):
    b, m, k = mem.shape
    n = w_kv.shape[1]
    return pl.pallas_call(
        _mem_kv_kernel,
        grid=(n // tn, b),
        in_specs=[pl.BlockSpec((1, m, k), lambda j, bi: (bi, 0, 0)),
                  pl.BlockSpec((1, k), lambda j, bi: (0, 0)),
                  pl.BlockSpec((k, tn), lambda j, bi: (0, j))],
        out_specs=pl.BlockSpec((1, m, tn), lambda j, bi: (bi, 0, j)),
        out_shape=jax.ShapeDtypeStruct((b, m, n), BF16),
        scratch_shapes=[pltpu.VMEM((k, tn), BF16)],
        compiler_params=_cparams(2),
        name="mem_kv",
    )(mem, g_mem.reshape(1, k), w_kv)


def _xa_kernel(q_ref, k_ref, v_ref, o_ref, *, heads):
    nt = (((1,), (1,)), ((), ()))
    hd = XA_HEAD_DIM
    for hh in range(heads):
        cs = slice(hh * hd, (hh + 1) * hd)
        s = lax.dot_general(q_ref[0, :, cs], k_ref[0, :, cs], nt, preferred_element_type=F32)
        m = jnp.max(s, axis=-1, keepdims=True)
        p = jnp.exp2(s - m)
        l = jnp.sum(p, axis=-1, keepdims=True)
        o = jnp.dot(p.astype(BF16), v_ref[0, :, cs], preferred_element_type=F32)
        o_ref[0, :, cs] = (o * (1.0 / l)).astype(o_ref.dtype)


def _cross_attention(p3, kv3, heads=2):
    b, s, _ = p3.shape
    mem_len = kv3.shape[1]
    hd = XA_HEAD_DIM * heads
    groups = XA_HEADS // heads
    return pl.pallas_call(
        functools.partial(_xa_kernel, heads=heads),
        grid=(b, groups),
        in_specs=[pl.BlockSpec((1, s, hd), lambda bi, h: (bi, 0, XQ_OFF // hd + h)),
                  pl.BlockSpec((1, mem_len, hd), lambda bi, h: (bi, 0, h)),
                  pl.BlockSpec((1, mem_len, hd), lambda bi, h: (bi, 0, groups + h))],
        out_specs=pl.BlockSpec((1, s, hd), lambda bi, h: (bi, 0, h)),
        out_shape=jax.ShapeDtypeStruct((b, s, XA_WIDTH), BF16),
        compiler_params=_cparams(2),
        name="cross_attention",
    )(p3, kv3, kv3)


def _merge_kernel(oc_ref, od_ref, ox_ref, g0_ref, g1_ref, g2_ref, wc_ref, wd_ref, wx_ref, o_ref, *, sub):
    branches = ((oc_ref, wc_ref, g0_ref), (od_ref, wd_ref, g1_ref), (ox_ref, wx_ref, g2_ref))
    for r0 in range(0, o_ref.shape[0], sub):
        rows = slice(r0, r0 + sub)
        y = None
        for a_ref, w_ref, g_ref in branches:
            term = g_ref[rows, :].astype(F32) * jnp.dot(a_ref[rows, :], w_ref[...], preferred_element_type=F32)
            y = term if y is None else y + term
        o_ref[rows, :] = y.astype(o_ref.dtype)


def _merge(o_conv, o_da, o_xa, p2, wc_bf, wd_bf, wx_bf, tm=512, sub=256):
    t, kk = o_conv.shape
    n = wc_bf.shape[1]
    gcol = GATE_OFF // n
    act = lambda: pl.BlockSpec((tm, kk), lambda i: (i, 0))
    wsp = lambda: pl.BlockSpec((kk, n), lambda i: (0, 0), pipeline_mode=pl.Buffered(1))
    gate = lambda br: pl.BlockSpec((tm, n), lambda i: (i, gcol + br))
    return pl.pallas_call(
        functools.partial(_merge_kernel, sub=sub),
        grid=(t // tm,),
        in_specs=[act(), act(), act(), gate(0), gate(1), gate(2), wsp(), wsp(), wsp()],
        out_specs=pl.BlockSpec((tm, n), lambda i: (i, 0)),
        out_shape=jax.ShapeDtypeStruct((t, n), BF16),
        compiler_params=_cparams(1),
        name="merge",
    )(o_conv, o_da, o_xa, p2, p2, p2, wc_bf, wd_bf, wx_bf)


def _mix_kernel(a_ref, w_ref, x_ref, g_ref, x1_ref, h2_ref, *, sub):
    for r0 in range(0, a_ref.shape[0], sub):
        rows = slice(r0, r0 + sub)
        x1 = x_ref[rows, :] + jnp.dot(a_ref[rows, :], w_ref[...], preferred_element_type=F32)
        x1_ref[rows, :] = x1
        h2_ref[rows, :] = _rms_scale(x1, g_ref[...]).astype(h2_ref.dtype)


def _mix(merged, w_mix_bf, x2d, g_mlp, tm=1024, sub=512):
    t, k = merged.shape
    n = w_mix_bf.shape[1]
    return pl.pallas_call(
        functools.partial(_mix_kernel, sub=sub),
        grid=(t // tm,),
        in_specs=[pl.BlockSpec((tm, k), lambda i: (i, 0)),
                  pl.BlockSpec((k, n), lambda i: (0, 0), pipeline_mode=pl.Buffered(1)),
                  pl.BlockSpec((tm, n), lambda i: (i, 0)),
                  pl.BlockSpec((1, n), lambda i: (0, 0))],
        out_specs=[pl.BlockSpec((tm, n), lambda i: (i, 0)),
                   pl.BlockSpec((tm, n), lambda i: (i, 0))],
        out_shape=[jax.ShapeDtypeStruct((t, n), F32),
                   jax.ShapeDtypeStruct((t, n), BF16)],
        compiler_params=_cparams(1),
        name="mix_out",
    )(merged, w_mix_bf, x2d, g_mlp.reshape(1, n))


def _mlp_kernel(h_ref, x_hbm, wu_ref, wd_ref, gf_ref, o_ref, xbuf_ref, xsem, *, nf, tm):
    i = pl.program_id(0)
    f = pl.program_id(1)

    def residual_copy():
        rows = pl.ds(pl.multiple_of(i * tm, tm), tm)
        return pltpu.make_async_copy(x_hbm.at[rows, :], xbuf_ref, xsem)

    @pl.when(f == nf - 2)
    def _():
        residual_copy().start()

    def ff_tile():
        hm = jnp.dot(h_ref[...], wu_ref[...], preferred_element_type=F32)
        hm = jnp.square(jnp.maximum(hm, 0.0)).astype(BF16)
        return jnp.dot(hm, wd_ref[...], preferred_element_type=F32)

    @pl.when(f == 0)
    def _():
        o_ref[...] = ff_tile()

    @pl.when((f > 0) & (f < nf - 1))
    def _():
        o_ref[...] += ff_tile()

    @pl.when(f == nf - 1)
    def _():
        residual_copy().wait()
        o_ref[...] = _rms_scale(xbuf_ref[...] + o_ref[...] + ff_tile(), gf_ref[...])


def _mlp(h2, x1, w_up, w_down, g_final, tm=1024, tf=1024):
    t, d = x1.shape
    ff = w_up.shape[1]
    nf = ff // tf
    assert nf >= 2
    return pl.pallas_call(
        functools.partial(_mlp_kernel, nf=nf, tm=tm),
        grid=(t // tm, nf),
        in_specs=[pl.BlockSpec((tm, d), lambda i, f: (i, 0)),
                  pl.BlockSpec(memory_space=pl.ANY),
                  pl.BlockSpec((d, tf), lambda i, f: (0, f)),
                  pl.BlockSpec((tf, d), lambda i, f: (f, 0)),
                  pl.BlockSpec((1, d), lambda i, f: (0, 0))],
        out_specs=pl.BlockSpec((tm, d), lambda i, f: (i, 0)),
        out_shape=jax.ShapeDtypeStruct((t, d), F32),
        scratch_shapes=[pltpu.VMEM((tm, d), F32), pltpu.SemaphoreType.DMA(())],
        compiler_params=_cparams(2),
        name="mlp",
    )(h2, x1, w_up, w_down, g_final.reshape(1, d))


def kernel(x, mem, positions, g_mix, w_in, w_dw, b_dw, g_conv_ln, b_conv_ln, w_conv_out, lambda_q1, lambda_k1, lambda_q2, lambda_k2, g_subln, w_da_out, g_mem, w_mem_kv, w_xa_out, w_mix_out, g_mlp, w_up, w_down, g_final):
    bsz, seq, d = x.shape
    t = bsz * seq
    x2d = x.reshape(t, d)

    h, cos2, sin2 = _norm_and_rope(x2d, g_mix[0], positions.astype(F32).reshape(t, 1))
    u, p2 = _in_proj(h, w_in[0], cos2, sin2)
    p3 = p2.reshape(bsz, seq, P_COLS)

    o_conv, w_up_bf, w_down_bf = _conv_branch(u.reshape(bsz, seq, CONV_CH), w_dw[0], b_dw[0], g_conv_ln[0],
                                              b_conv_ln[0], w_up[0], w_down[0])
    o_da, wc_bf, wd_bf, wx_bf, w_mix_bf = _diff_attention(
        p3, lambda_q1[0], lambda_k1[0], lambda_q2[0], lambda_k2[0], g_subln[0],
        (w_conv_out[0], w_da_out[0], w_xa_out[0], w_mix_out[0]))

    kv3 = _mem_kv(mem, g_mem[0], w_mem_kv[0])
    o_xa = _cross_attention(p3, kv3)

    merged = _merge(o_conv.reshape(t, CONV_CH), o_da.reshape(t, DA_WIDTH), o_xa.reshape(t, XA_WIDTH),
                    p2, wc_bf, wd_bf, wx_bf)
    x1, h2 = _mix(merged, w_mix_bf, x2d, g_mlp[0])
    out = _mlp(h2, x1, w_up_bf, w_down_bf, g_final)
    return out.reshape(bsz, seq, d)
```
